```python
import math
import jax
import jax.numpy as jnp
from jax import lax
import numpy as np

D_MODEL = 1024
BATCH = 8
SEQ = 4096
DEPTH = 4
DEC_BATCH = 16
DEC_SEQ = 16
PAST_LEN = 1024

CHUNK = 64
Q_BLOCK = 128
N_EVEN = (DEPTH + 1) // 2
N_ODD = DEPTH // 2
D_FF = ((8 * D_MODEL // 3 + 127) // 128) * 128
EPS = 1e-6
NEG_INF = -1e30
LRU_W = D_MODEL // 2
LRU_BLOCKS = 8
LRU_BLOCK = LRU_W // LRU_BLOCKS
LRU_C = 8.0
CONV_W = 4
FOX_HEADS = 8
FOX_DH = (D_MODEL // 2) // FOX_HEADS
FOX_W = FOX_HEADS * FOX_DH
EVEN_MIX_W = LRU_W + FOX_W
EVEN_PROJ = 2 * LRU_W + 3 * FOX_W + FOX_HEADS
DIFF_HEADS = 8
DIFF_DH = D_MODEL // (2 * DIFF_HEADS)
DIFF_W = DIFF_HEADS * 2 * DIFF_DH
ROT_DIM = DIFF_DH // 4
ROPE_THETA = 500000.0

kernel_name = 'streaming_hybrid_rglru_fox_diffattn_step'


def _rmsnorm(x, g):
    xf = x.astype(jnp.float32)
    y = xf * lax.rsqrt(jnp.mean(xf * xf, axis=-1, keepdims=True) + EPS)
    return (y * g.astype(jnp.float32)).astype(x.dtype)


def _swiglu(x, w_in, w_out):
    g, u = jnp.split(x @ w_in, 2, axis=-1)
    return (jax.nn.silu(g) * u) @ w_out


def _rope(x, pos):
    half = ROT_DIM // 2
    inv = ROPE_THETA ** (-jnp.arange(half, dtype=jnp.float32) * (2.0 / ROT_DIM))
    ang = pos.astype(jnp.float32)[:, None] * inv[None, :]
    shape = (ang.shape[0],) + (1,) * (x.ndim - 3) + (half,)
    cos = jnp.cos(ang).reshape(shape).astype(x.dtype)
    sin = jnp.sin(ang).reshape(shape).astype(x.dtype)
    x1 = x[..., :half]
    x2 = x[..., half:ROT_DIM]
    return jnp.concatenate([x1 * cos - x2 * sin, x2 * cos + x1 * sin, x[..., ROT_DIM:]], axis=-1)


def _over_query_blocks(fn, qpos, *qs):
    T = qpos.shape[0]
    if T <= Q_BLOCK:
        return fn(qpos, *qs)
    nb = T // Q_BLOCK

    def split(a):
        return jnp.moveaxis(a.reshape(a.shape[0], nb, Q_BLOCK, *a.shape[2:]), 1, 0)

    out = lax.map(lambda args: fn(args[0], *args[1:]),
                  (qpos.reshape(nb, Q_BLOCK),) + tuple(split(a) for a in qs))
    out = jnp.moveaxis(out, 0, 1)
    return out.reshape(out.shape[0], T, *out.shape[3:])


def _causal_conv(u, buf, w, b):
    T = u.shape[1]
    full = jnp.concatenate([buf.astype(u.dtype), u], axis=1)
    y = b
    for j in range(CONV_W):
        y = y + full[:, j:j + T] * w[j]
    return y, full[:, T:]


def _rglru(u, h0, w_gates, b_gates, lam):
    B, T, C = u.shape
    ub = u.reshape(B, T, LRU_BLOCKS, LRU_BLOCK)
    gates = (jnp.einsum('btnc,ncg->btng', ub, w_gates) + b_gates).astype(jnp.float32)
    r = jax.nn.sigmoid(gates[..., :LRU_BLOCK]).reshape(B, T, C)
    i = jax.nn.sigmoid(gates[..., LRU_BLOCK:]).reshape(B, T, C)
    log_a = -LRU_C * r * jax.nn.softplus(-lam.astype(jnp.float32))
    a = jnp.exp(log_a)
    bx = jnp.sqrt(-jnp.expm1(2.0 * log_a)) * (i * u.astype(jnp.float32))
    bx = bx.at[:, 0].add(a[:, 0] * h0.astype(jnp.float32))

    def combine(left, right):
        a_l, b_l = left
        a_r, b_r = right
        return a_l * a_r, a_r * b_l + b_r

    _, h = lax.associative_scan(combine, (a, bx), axis=1)
    return h, h[:, -1]


def _even_mixer(h, pos, fk_past, fv_past, flogf_past, lru_h0, lru_buf,
                w_in, b_f, conv_w, conv_b, w_gates, b_gates, lam, w_out):
    B, T, _ = h.shape
    cuts = [LRU_W, 2 * LRU_W, 2 * LRU_W + FOX_W, 2 * LRU_W + 2 * FOX_W, 2 * LRU_W + 3 * FOX_W]
    u, gate, q, k, v, f_logit = jnp.split(h @ w_in, cuts, axis=-1)
    uc, new_buf = _causal_conv(u, lru_buf, conv_w, conv_b)
    hs, h_last = _rglru(uc, lru_h0, w_gates, b_gates, lam)
    y_a = hs.astype(h.dtype) * jax.nn.gelu(gate)
    q = q.reshape(B, T, FOX_HEADS, FOX_DH) * (FOX_DH ** -0.5)
    k = k.reshape(B, T, FOX_HEADS, FOX_DH)
    v = v.reshape(B, T, FOX_HEADS, FOX_DH)
    logf = jax.nn.log_sigmoid((f_logit + b_f).astype(jnp.float32))
    P = fk_past.shape[1]
    k_all = jnp.concatenate([fk_past.astype(k.dtype), k], axis=1)
    v_all = jnp.concatenate([fv_past.astype(v.dtype), v], axis=1)
    c_all = jnp.cumsum(jnp.concatenate([flogf_past.astype(jnp.float32), logf], axis=1), axis=1)
    c_keys = c_all.transpose(0, 2, 1)[:, :, None, :]
    c_q = c_all[:, P:]
    kpos = jnp.arange(P + T, dtype=jnp.int32)

    def block(qp, qb, cqb):
        s = jnp.einsum('bqhd,bkhd->bhqk', qb, k_all).astype(jnp.float32)
        s = s + cqb.transpose(0, 2, 1)[..., None] - c_keys
        s = jnp.where(qp[:, None] >= kpos[None, :], s, NEG_INF)
        p = jax.nn.softmax(s, axis=-1).astype(v_all.dtype)
        return jnp.einsum('bhqk,bkhd->bqhd', p, v_all)

    y_b = _over_query_blocks(block, pos, q, c_q).reshape(B, T, FOX_W)
    out = jnp.concatenate([y_a, y_b], axis=-1) @ w_out
    return out, (k, v, logf.astype(h.dtype), h_last.astype(h.dtype), new_buf)


def _diff_mixer(h, pos, dk_past, dv_past, w_in, lam_params, subln_g, w_out, lam_init):
    B, T, _ = h.shape
    q, k, v = jnp.split(h @ w_in, 3, axis=-1)
    q = _rope(q.reshape(B, T, DIFF_HEADS, 2, DIFF_DH), pos) * (DIFF_DH ** -0.5)
    k = _rope(k.reshape(B, T, DIFF_HEADS, 2, DIFF_DH), pos)
    v = v.reshape(B, T, DIFF_HEADS, 2 * DIFF_DH)
    P = dk_past.shape[1]
    k_all = jnp.concatenate([dk_past.astype(k.dtype).reshape(B, P, DIFF_HEADS, 2, DIFF_DH), k], axis=1)
    v_all = jnp.concatenate([dv_past.astype(v.dtype), v], axis=1)
    lp = lam_params.astype(jnp.float32)
    lam = jnp.exp(jnp.sum(lp[0] * lp[1])) - jnp.exp(jnp.sum(lp[2] * lp[3])) + lam_init
    kchunk = jnp.arange(P + T, dtype=jnp.int32) // CHUNK

    def block(qp, qb):
        s = jnp.einsum('bqhcd,bkhcd->bhcqk', qb, k_all).astype(jnp.float32)
        s = jnp.where((qp // CHUNK)[:, None] >= kchunk[None, :], s, NEG_INF)
        p = jax.nn.softmax(s, axis=-1)
        a = (p[:, :, 0] - lam * p[:, :, 1]).astype(v_all.dtype)
        return jnp.einsum('bhqk,bkhd->bqhd', a, v_all)

    o = _over_query_blocks(block, pos, q)
    o = _rmsnorm(o, subln_g) * (1.0 - lam_init)
    out = o.reshape(B, T, DIFF_W) @ w_out
    return out, (k.reshape(B, T, DIFF_HEADS, 2 * DIFF_DH), v)


def _trunk(x, pos, fox_k, fox_v, fox_logf, lru_h, lru_conv, diff_k, diff_v,
           norm_g, w_ffn_in, w_ffn_out, w_in_even, b_fox_f, lru_conv_w, lru_conv_b,
           lru_w_gates, lru_b_gates, lru_lambda, w_out_even, w_in_odd, diff_lambda,
           diff_subln_g, w_out_odd):
    new_even, new_odd = [], []
    for l in range(DEPTH):
        g = norm_g[l]
        x = x + 0.5 * _rmsnorm(_swiglu(_rmsnorm(x, g[0]), w_ffn_in[l, 0], w_ffn_out[l, 0]), g[1])
        hn = _rmsnorm(x, g[2])
        if l % 2 == 0:
            e = l // 2
            m, st = _even_mixer(hn, pos, fox_k[e], fox_v[e], fox_logf[e], lru_h[e], lru_conv[e],
                                w_in_even[e], b_fox_f[e], lru_conv_w[e], lru_conv_b[e],
                                lru_w_gates[e], lru_b_gates[e], lru_lambda[e], w_out_even[e])
            new_even.append(st)
        else:
            o = l // 2
            lam_init = 0.8 - 0.6 * math.exp(-0.3 * l)
            m, st = _diff_mixer(hn, pos, diff_k[o], diff_v[o], w_in_odd[o], diff_lambda[o],
                                diff_subln_g[o], w_out_odd[o], lam_init)
            new_odd.append(st)
        x = x + _rmsnorm(m, g[3])
        x = x + 0.5 * _rmsnorm(_swiglu(_rmsnorm(x, g[4]), w_ffn_in[l, 1], w_ffn_out[l, 1]), g[5])
    even = [jnp.stack(z) for z in zip(*new_even)]
    odd = [jnp.stack(z) for z in zip(*new_odd)]
    return x, even, odd


def setup_inputs(seed: int = 0) -> dict:
    key = jax.random.key(seed)
    ks = jax.random.split(key, 24)
    nrm = jax.random.normal
    f32 = jnp.float32
    u = jax.random.uniform(ks[18], (N_EVEN, LRU_W), f32, 0.9, 0.999)
    s = u ** (1.0 / LRU_C)
    return {
        'x_prompt': nrm(ks[0], (BATCH, SEQ, D_MODEL), f32),
        'x_sample': nrm(ks[1], (DEC_BATCH, DEC_SEQ, D_MODEL), f32),
        'cache_fox_k': nrm(ks[2], (N_EVEN, DEC_BATCH, PAST_LEN, FOX_HEADS, FOX_DH), f32),
        'cache_fox_v': nrm(ks[3], (N_EVEN, DEC_BATCH, PAST_LEN, FOX_HEADS, FOX_DH), f32),
        'cache_fox_logf': jax.nn.log_sigmoid(2.0 + 0.5 * nrm(ks[4], (N_EVEN, DEC_BATCH, PAST_LEN, FOX_HEADS), f32)),
        'state_lru_h': 0.5 * nrm(ks[5], (N_EVEN, DEC_BATCH, LRU_W), f32),
        'state_lru_conv': nrm(ks[6], (N_EVEN, DEC_BATCH, CONV_W - 1, LRU_W), f32),
        'cache_diff_k': nrm(ks[7], (N_ODD, DEC_BATCH, PAST_LEN, DIFF_HEADS, 2 * DIFF_DH), f32),
        'cache_diff_v': nrm(ks[8], (N_ODD, DEC_BATCH, PAST_LEN, DIFF_HEADS, 2 * DIFF_DH), f32),
        'norm_g': 1.0 + 0.05 * nrm(ks[9], (DEPTH, 6, D_MODEL), f32),
        'w_ffn_in': nrm(ks[10], (DEPTH, 2, D_MODEL, 2 * D_FF), f32) * D_MODEL ** -0.5,
        'w_ffn_out': nrm(ks[11], (DEPTH, 2, D_FF, D_MODEL), f32) * D_FF ** -0.5,
        'w_in_even': nrm(ks[12], (N_EVEN, D_MODEL, EVEN_PROJ), f32) * D_MODEL ** -0.5,
        'b_fox_f': 2.0 + 0.5 * nrm(ks[13], (N_EVEN, FOX_HEADS), f32),
        'lru_conv_w': nrm(ks[14], (N_EVEN, CONV_W, LRU_W), f32) * CONV_W ** -0.5,
        'lru_conv_b': 0.01 * nrm(ks[15], (N_EVEN, LRU_W), f32),
        'lru_w_gates': nrm(ks[16], (N_EVEN, LRU_BLOCKS, LRU_BLOCK, 2 * LRU_BLOCK), f32) * LRU_BLOCK ** -0.5,
        'lru_b_gates': 0.01 * nrm(ks[17], (N_EVEN, LRU_BLOCKS, 2 * LRU_BLOCK), f32),
        'lru_lambda': jnp.log(s) - jnp.log1p(-s),
        'w_out_even': nrm(ks[19], (N_EVEN, EVEN_MIX_W, D_MODEL), f32) * EVEN_MIX_W ** -0.5,
        'w_in_odd': nrm(ks[20], (N_ODD, D_MODEL, 3 * DIFF_W), f32) * D_MODEL ** -0.5,
        'diff_lambda': 0.1 * nrm(ks[21], (N_ODD, 4, DIFF_DH), f32),
        'diff_subln_g': 1.0 + 0.05 * nrm(ks[22], (N_ODD, 2 * DIFF_DH), f32),
        'w_out_odd': nrm(ks[23], (N_ODD, DIFF_W, D_MODEL), f32) * DIFF_W ** -0.5,
    }


def reference(x_prompt, x_sample, cache_fox_k, cache_fox_v, cache_fox_logf, state_lru_h,
              state_lru_conv, cache_diff_k, cache_diff_v, norm_g, w_ffn_in, w_ffn_out,
              w_in_even, b_fox_f, lru_conv_w, lru_conv_b, lru_w_gates, lru_b_gates,
              lru_lambda, w_out_even, w_in_odd, diff_lambda, diff_subln_g, w_out_odd):
    weights = (norm_g, w_ffn_in, w_ffn_out, w_in_even, b_fox_f, lru_conv_w, lru_conv_b,
               lru_w_gates, lru_b_gates, lru_lambda, w_out_even, w_in_odd, diff_lambda,
               diff_subln_g, w_out_odd)
    B, S, _ = x_prompt.shape
    dt = x_prompt.dtype
    y_prompt, pe, po = _trunk(
        x_prompt, jnp.arange(S, dtype=jnp.int32),
        jnp.zeros((N_EVEN, B, 0, FOX_HEADS, FOX_DH), dt),
        jnp.zeros((N_EVEN, B, 0, FOX_HEADS, FOX_DH), dt),
        jnp.zeros((N_EVEN, B, 0, FOX_HEADS), dt),
        jnp.zeros((N_EVEN, B, LRU_W), dt),
        jnp.zeros((N_EVEN, B, CONV_W - 1, LRU_W), dt),
        jnp.zeros((N_ODD, B, 0, DIFF_HEADS, 2 * DIFF_DH), dt),
        jnp.zeros((N_ODD, B, 0, DIFF_HEADS, 2 * DIFF_DH), dt),
        *weights)
    past = cache_fox_k.shape[2]
    T = x_sample.shape[1]
    y_sample, se, so = _trunk(
        x_sample, past + jnp.arange(T, dtype=jnp.int32),
        cache_fox_k, cache_fox_v, cache_fox_logf, state_lru_h, state_lru_conv,
        cache_diff_k, cache_diff_v, *weights)
    p_fox_k, p_fox_v, p_fox_logf, p_lru_h, p_lru_conv = pe
    p_diff_k, p_diff_v = po
    s_fox_k, s_fox_v, s_fox_logf, s_lru_h, s_lru_conv = se
    s_diff_k, s_diff_v = so
    return (y_prompt, y_sample,
            p_fox_k, p_fox_v, p_fox_logf, p_lru_h, p_lru_conv, p_diff_k, p_diff_v,
            s_fox_k, s_fox_v, s_fox_logf, s_lru_h, s_lru_conv, s_diff_k, s_diff_v)
```

```python
import functools
import math

import jax
import jax.numpy as jnp
from jax import lax
from jax.experimental import pallas as pl
from jax.experimental.pallas import tpu as pltpu

F32 = jnp.float32
BF16 = jnp.bfloat16

EPS = 1e-6
NEG_INF = -1e30
LRU_C = 8.0
LRU_BLOCKS = 8
CONV_W = 4
FOX_HEADS = 8
DIFF_HEADS = 8
CHUNK = 64
ROT_DIM = 16
ROPE_THETA = 500000.0

LANES = 128
HALF = LANES // 2
FF_CHUNK = 256
VMEM_LIMIT = 56 * 1024 * 1024


def _params(*sem):
    return pltpu.CompilerParams(dimension_semantics=sem, vmem_limit_bytes=VMEM_LIMIT)


def _resident(shape):
    nd = len(shape)
    return pl.BlockSpec(shape, lambda *_: (0,) * nd, pipeline_mode=pl.Buffered(1))


def _rms(x, g):
    return x * lax.rsqrt(jnp.mean(x * x, axis=-1, keepdims=True) + EPS) * g


def _mm(a, b):
    return jnp.dot(a, b, preferred_element_type=F32)


def _mm_nt(a, b):
    return lax.dot_general(a, b, (((1,), (1,)), ((), ())), preferred_element_type=F32)


def _log_sigmoid(x):
    return jnp.minimum(x, 0.0) - jnp.log1p(jnp.exp(-jnp.abs(x)))


def _softplus(x):
    return jnp.maximum(x, 0.0) + jnp.log1p(jnp.exp(-jnp.abs(x)))


def _pick_tile(n, target):
    t = min(n, target)
    while n % t:
        t //= 2
    return t


def _ffn_kernel(x_ref, gpre_ref, gpost_ref, wi_ref, wo_ref, o_ref, act_ref, *, dff):
    x = x_ref[...]
    xn = _rms(x, gpre_ref[...]).astype(BF16)
    for c0 in range(0, dff, FF_CHUNK):
        g = _mm(xn, wi_ref[:, c0:c0 + FF_CHUNK])
        u = _mm(xn, wi_ref[:, dff + c0:dff + c0 + FF_CHUNK])
        act_ref[:, c0:c0 + FF_CHUNK] = ((g * jax.nn.sigmoid(g)) * u).astype(BF16)
    y = _mm(act_ref[...], wo_ref[...])
    o_ref[...] = x + 0.5 * _rms(y, gpost_ref[...])


def _ffn(x2, g_pre, g_post, w_in, w_out, tm):
    n, d = x2.shape
    dff = w_out.shape[0]
    assert dff % FF_CHUNK == 0 and n % tm == 0
    row = pl.BlockSpec((tm, d), lambda i: (i, 0))
    return pl.pallas_call(
        functools.partial(_ffn_kernel, dff=dff),
        grid=(n // tm,),
        in_specs=[row, _resident((1, d)), _resident((1, d)), _resident((d, 2 * dff)), _resident((dff, d))],
        out_specs=row,
        out_shape=jax.ShapeDtypeStruct((n, d), F32),
        scratch_shapes=[pltpu.VMEM((tm, dff), BF16)],
        compiler_params=_params("parallel"),
        name="ffn",
    )(x2, g_pre, g_post, w_in, w_out)


def _inproj_even_kernel(x_ref, g_ref, w_ref, bf_ref, u_ref, gate_ref, q_ref, k_ref, v_ref, kb_ref, vb_ref,
                        logf_ref, *, w):
    hn = _rms(x_ref[...], g_ref[...]).astype(BF16)
    u_ref[...] = _mm(hn, w_ref[:, 0:w])
    gate_ref[...] = _mm(hn, w_ref[:, w:2 * w])
    q_ref[...] = (_mm(hn, w_ref[:, 2 * w:3 * w]) * (HALF ** -0.5)).astype(BF16)
    k = _mm(hn, w_ref[:, 3 * w:4 * w])
    k_ref[...] = k
    kb_ref[...] = k.astype(BF16)
    v = _mm(hn, w_ref[:, 4 * w:5 * w])
    v_ref[...] = v
    vb_ref[...] = v.astype(BF16)
    logf_ref[...] = _log_sigmoid(_mm(hn, w_ref[:, 5 * w:6 * w]) + bf_ref[...])


def _inproj_even(x2, g, w_wide, bf_wide, tm):
    n, d = x2.shape
    w = w_wide.shape[1] // 6
    row = lambda c: pl.BlockSpec((tm, c), lambda i: (i, 0))
    f = jax.ShapeDtypeStruct((n, w), F32)
    b = jax.ShapeDtypeStruct((n, w), BF16)
    return pl.pallas_call(
        functools.partial(_inproj_even_kernel, w=w),
        grid=(n // tm,),
        in_specs=[row(d), _resident((1, d)), _resident((d, 6 * w)), _resident((1, w))],
        out_specs=[row(w)] * 8,
        out_shape=[f, f, b, f, f, b, b, f],
        compiler_params=_params("parallel"),
        name="inproj_even",
    )(x2, g, w_wide, bf_wide)


def _shift_rows(x, d, fill):
    rows = lax.broadcasted_iota(jnp.int32, x.shape, 0)
    return jnp.where(rows >= d, pltpu.roll(x, d, 0), fill)


def _lru_kernel(u_ref, gate_ref, h0_ref, buf0_ref, cw_ref, cb_ref, wr_ref, wi_ref, br_ref, bi_ref, lam_ref,
                ya_ref, hlast_ref, nbuf_ref, ubuf, hcar, *, tt):
    t = pl.program_id(1)
    pad = ubuf.shape[0] - tt

    @pl.when(t == 0)
    def _():
        ubuf[0:pad, :] = buf0_ref[0]
        hcar[...] = h0_ref[0]

    u = u_ref[0]
    ubuf[pad:pad + tt, :] = u
    uc = cb_ref[...]
    for j in range(CONV_W):
        o = pad - (CONV_W - 1) + j
        uc = uc + ubuf[o:o + tt, :] * cw_ref[j:j + 1, :]
    ucb = uc.astype(BF16)
    r = jax.nn.sigmoid(_mm(ucb, wr_ref[...]) + br_ref[...])
    ig = jax.nn.sigmoid(_mm(ucb, wi_ref[...]) + bi_ref[...])
    log_a = (-LRU_C * r) * _softplus(-lam_ref[...])
    a = jnp.exp(log_a)
    bx = jnp.sqrt(-jnp.tanh(log_a) * (a * a + 1.0)) * (ig * uc)
    rows = lax.broadcasted_iota(jnp.int32, bx.shape, 0)
    bx = bx + jnp.where(rows == 0, a * hcar[...], 0.0)
    d = 1
    while d < tt:
        bx = a * _shift_rows(bx, d, 0.0) + bx
        a = a * _shift_rows(a, d, 1.0)
        d *= 2
    hcar[...] = bx[tt - 1:tt, :]
    ya_ref[0] = (bx * jax.nn.gelu(gate_ref[0])).astype(BF16)
    ubuf[0:pad, :] = ubuf[tt:tt + pad, :]

    @pl.when(t == pl.num_programs(1) - 1)
    def _():
        hlast_ref[0] = bx[tt - 1:tt, :]
        nbuf_ref[0] = ubuf[pad - (CONV_W - 1):pad, :]


def _lru(u, gate, h0, buf0, cw, cb, wr, wi, br, bi, lam, tt):
    bsz, T, w = u.shape
    pad = 8
    seq = pl.BlockSpec((1, tt, w), lambda b, t: (b, t, 0))
    per_b = lambda r: pl.BlockSpec((1, r, w), lambda b, t: (b, 0, 0))
    return pl.pallas_call(
        functools.partial(_lru_kernel, tt=tt),
        grid=(bsz, T // tt),
        in_specs=[seq, seq, per_b(1), per_b(pad), _resident((CONV_W, w)), _resident((1, w)), _resident((w, w)),
                  _resident((w, w)), _resident((1, w)), _resident((1, w)), _resident((1, w))],
        out_specs=[seq, per_b(1), per_b(CONV_W - 1)],
        out_shape=[jax.ShapeDtypeStruct((bsz, T, w), BF16), jax.ShapeDtypeStruct((bsz, 1, w), F32),
                   jax.ShapeDtypeStruct((bsz, CONV_W - 1, w), F32)],
        scratch_shapes=[pltpu.VMEM((tt + pad, w), F32), pltpu.VMEM((1, w), F32)],
        compiler_params=_params("parallel", "arbitrary"),
        name="conv_rglru",
    )(u, gate, h0, buf0, cw, cb, wr, wi, br, bi, lam)


def _cumsum_kernel(x_ref, c_ref, ct_ref, car, *, tt):
    @pl.when(pl.program_id(1) == 0)
    def _():
        car[...] = jnp.zeros_like(car)

    x = x_ref[0]
    rows = lax.broadcasted_iota(jnp.int32, x.shape, 0)
    x = x + jnp.where(rows == 0, car[...], 0.0)
    d = 1
    while d < tt:
        x = x + _shift_rows(x, d, 0.0)
        d *= 2
    car[...] = x[tt - 1:tt, :]
    c_ref[0] = x
    for s in range(x.shape[1] // LANES):
        tr = x[:, s * LANES:(s + 1) * LANES].T
        ct_ref[0, s, 0:1, :] = tr[0:1, :]
        ct_ref[0, s, 1:2, :] = tr[HALF:HALF + 1, :]


def _cumsum(x, tt):
    bsz, T, w = x.shape
    ns = w // LANES
    return pl.pallas_call(
        functools.partial(_cumsum_kernel, tt=tt),
        grid=(bsz, T // tt),
        in_specs=[pl.BlockSpec((1, tt, w), lambda b, t: (b, t, 0))],
        out_specs=[pl.BlockSpec((1, tt, w), lambda b, t: (b, t, 0)),
                   pl.BlockSpec((1, ns, 2, tt), lambda b, t: (b, 0, 0, t))],
        out_shape=[jax.ShapeDtypeStruct((bsz, T, w), F32), jax.ShapeDtypeStruct((bsz, ns, 2, T), F32)],
        scratch_shapes=[pltpu.VMEM((1, w), F32)],
        compiler_params=_params("parallel", "arbitrary"),
        name="logf_cumsum",
    )(x)


def _attn_kernel(*refs, tq, tk, nk_total, pos0, fox, lam_init):
    if fox:
        q_ref, k_ref, v_ref, cq_ref, ck_ref, o_ref, m_sc, l_sc, acc_sc = refs
    else:
        q_ref, k_ref, v_ref, lp_ref, sg_ref, o_ref, m_sc, l_sc, acc_sc = refs
    i = pl.program_id(2)
    q0 = pos0 + i * tq
    lane = lax.broadcasted_iota(jnp.int32, (1, LANES), 1)
    lo = lane < HALF
    q = q_ref[0]
    qm = (jnp.where(lo, q, jnp.zeros_like(q)), jnp.where(lo, jnp.zeros_like(q), q))
    if fox:
        cq = cq_ref[0]
        cqm = (cq[:, 0:1], cq[:, HALF:HALF + 1])
        n_full = (q0 + 1) // tk
        n_need = jnp.minimum((q0 + tq - 1) // tk + 1, nk_total)
    else:
        n_full = ((q0 // CHUNK + 1) * CHUNK) // tk
        n_need = jnp.minimum((((q0 + tq - 1) // CHUNK + 1) * CHUNK + tk - 1) // tk, nk_total)
    n_full = jnp.minimum(n_full, n_need)

    m_sc[...] = jnp.full_like(m_sc, NEG_INF)
    l_sc[...] = jnp.zeros_like(l_sc)
    acc_sc[...] = jnp.zeros_like(acc_sc)

    def tile(j, masked):
        k0 = pl.multiple_of(j * tk, tk)
        kt = k_ref[0, pl.ds(k0, tk), :]
        vt = v_ref[0, pl.ds(k0, tk), :]
        if masked:
            qpos = q0 + lax.broadcasted_iota(jnp.int32, (tq, tk), 0)
            kpos = k0 + lax.broadcasted_iota(jnp.int32, (tq, tk), 1)
            vis = (qpos >= kpos) if fox else ((qpos // CHUNK) >= (kpos // CHUNK))
        pv, al = [], []
        for c in range(2):
            s = _mm_nt(qm[c], kt)
            if fox:
                s = s + cqm[c] - ck_ref[0, 0, j, c:c + 1, :]
            if masked:
                s = jnp.where(vis, s, NEG_INF)
            m_prev = m_sc[c]
            m_new = jnp.maximum(m_prev, jnp.max(s, axis=-1, keepdims=True))
            alpha = jnp.exp(m_prev - m_new)
            p = jnp.exp(s - m_new)
            l_sc[c] = alpha * l_sc[c] + jnp.sum(p, axis=-1, keepdims=True)
            m_sc[c] = m_new
            pv.append(_mm(p.astype(BF16), vt))
            al.append(alpha)
        if fox:
            acc_sc[0] = jnp.where(lo, al[0] * acc_sc[0] + pv[0], al[1] * acc_sc[0] + pv[1])
        else:
            acc_sc[0] = al[0] * acc_sc[0] + pv[0]
            acc_sc[1] = al[1] * acc_sc[1] + pv[1]

    def full_body(j, carry):
        tile(j, False)
        return carry

    def masked_body(j, carry):
        tile(j, True)
        return carry

    lax.fori_loop(0, n_full, full_body, 0)
    lax.fori_loop(n_full, n_need, masked_body, 0)

    if fox:
        o_ref[0] = (acc_sc[0] / jnp.where(lo, l_sc[0], l_sc[1])).astype(BF16)
    else:
        lp = lp_ref[...]
        lam = (jnp.exp(jnp.sum(lp[0:1] * lp[1:2], axis=-1, keepdims=True))
               - jnp.exp(jnp.sum(lp[2:3] * lp[3:4], axis=-1, keepdims=True)) + lam_init)
        o = acc_sc[0] / l_sc[0] - lam * (acc_sc[1] / l_sc[1])
        o_ref[0] = (_rms(o, sg_ref[...]) * (1.0 - lam_init)).astype(BF16)


def _attention(q, k, v, extra, *, pos0, fox, lam_init=0.0, tq=256, tk=256):
    bsz, Tq, w = q.shape
    Tk = k.shape[1]
    ns = w // LANES
    tq = _pick_tile(Tq, tq)
    tk = tk if Tk % tk == 0 else Tk
    qspec = pl.BlockSpec((1, tq, LANES), lambda b, s, i: (b, i, s))
    kspec = pl.BlockSpec((1, Tk, LANES), lambda b, s, i: (b, 0, s))
    if fox:
        cq, ckt = extra
        ckt = ckt.reshape(bsz, ns, 2, Tk // tk, tk).transpose(0, 1, 3, 2, 4)
        extra = (cq, ckt)
        especs = [qspec, pl.BlockSpec((1, 1, Tk // tk, 2, tk), lambda b, s, i: (b, s, 0, 0, 0))]
    else:
        lp, sg = extra
        especs = [pl.BlockSpec(lp.shape, lambda b, s, i: (0, 0)), pl.BlockSpec(sg.shape, lambda b, s, i: (0, 0))]
    return pl.pallas_call(
        functools.partial(_attn_kernel, tq=tq, tk=tk, nk_total=Tk // tk, pos0=pos0, fox=fox, lam_init=lam_init),
        grid=(bsz, ns, Tq // tq),
        in_specs=[qspec, kspec, kspec] + especs,
        out_specs=qspec,
        out_shape=jax.ShapeDtypeStruct((bsz, Tq, w), BF16),
        scratch_shapes=[pltpu.VMEM((2, tq, 1), F32), pltpu.VMEM((2, tq, 1), F32),
                        pltpu.VMEM((1 if fox else 2, tq, LANES), F32)],
        compiler_params=_params("parallel", "parallel", "arbitrary"),
        name="fox_attention" if fox else "diff_attention",
    )(q, k, v, *extra)


def _rope_table_kernel(inv_ref, c_ref, s1_ref, s2_ref, *, pos0):
    T = c_ref.shape[0]
    pos = (pos0 + lax.broadcasted_iota(jnp.int32, (T, LANES), 0)).astype(F32)
    ang = pos * inv_ref[...]
    lane = lax.broadcasted_iota(jnp.int32, (T, LANES), 1) % HALF
    sn = jnp.sin(ang)
    c_ref[...] = jnp.cos(ang)
    s1_ref[...] = jnp.where(lane < ROT_DIM // 2, -sn, 0.0)
    s2_ref[...] = jnp.where(lane >= ROT_DIM // 2, sn, 0.0)


def _rope_tables(T, pos0):
    half = ROT_DIM // 2
    inv = ROPE_THETA ** (-jnp.arange(half, dtype=F32) * (2.0 / ROT_DIM))
    lane = jnp.arange(LANES) % HALF
    inv_lane = jnp.where(lane < ROT_DIM, inv[lane % half], 0.0).reshape(1, LANES).astype(F32)
    tab = jax.ShapeDtypeStruct((T, LANES), F32)
    return pl.pallas_call(
        functools.partial(_rope_table_kernel, pos0=pos0),
        out_shape=[tab, tab, tab],
        name="rope_tables",
    )(inv_lane)


def _inproj_odd_kernel(x_ref, g_ref, w_ref, c_ref, s1_ref, s2_ref, q_ref, k_ref, v_ref, kb_ref, vb_ref, *, w):
    hn = _rms(x_ref[...], g_ref[...]).astype(BF16)
    cs, s1, s2 = c_ref[...], s1_ref[...], s2_ref[...]
    half = ROT_DIM // 2

    def rope(t):
        out = []
        for h in range(w // LANES):
            s = t[:, h * LANES:(h + 1) * LANES]
            out.append(s * cs + pltpu.roll(s, LANES - half, 1) * s1 + pltpu.roll(s, half, 1) * s2)
        return jnp.concatenate(out, axis=1)

    q_ref[...] = (rope(_mm(hn, w_ref[:, 0:w])) * (HALF ** -0.5)).astype(BF16)
    k = rope(_mm(hn, w_ref[:, w:2 * w]))
    k_ref[...] = k
    kb_ref[...] = k.astype(BF16)
    v = _mm(hn, w_ref[:, 2 * w:3 * w])
    v_ref[...] = v
    vb_ref[...] = v.astype(BF16)


def _inproj_odd(x2, g, w_in, tabs, tab_blocks, tm):
    n, d = x2.shape
    w = w_in.shape[1] // 3
    row = lambda c: pl.BlockSpec((tm, c), lambda i: (i, 0))
    tab = pl.BlockSpec((tm, LANES), lambda i: (i % tab_blocks, 0))
    f = jax.ShapeDtypeStruct((n, w), F32)
    b = jax.ShapeDtypeStruct((n, w), BF16)
    return pl.pallas_call(
        functools.partial(_inproj_odd_kernel, w=w),
        grid=(n // tm,),
        in_specs=[row(d), _resident((1, d)), _resident((d, 3 * w)), tab, tab, tab],
        out_specs=[row(w)] * 5,
        out_shape=[b, f, f, b, b],
        compiler_params=_params("parallel"),
        name="inproj_odd",
    )(x2, g, w_in, *tabs)


def _outproj_kernel(*refs):
    x_ref, g_ref, w_ref = refs[:3]
    o_ref = refs[-1]
    y = jnp.concatenate([r[...] for r in refs[3:-1]], axis=1)
    o_ref[...] = x_ref[...] + _rms(_mm(y, w_ref[...]), g_ref[...])


def _outproj(x2, g, w_out, ys, tm):
    n, d = x2.shape
    row = lambda c: pl.BlockSpec((tm, c), lambda i: (i, 0))
    return pl.pallas_call(
        _outproj_kernel,
        grid=(n // tm,),
        in_specs=[row(d), _resident((1, d)), _resident(w_out.shape)] + [row(y.shape[1]) for y in ys],
        out_specs=row(d),
        out_shape=jax.ShapeDtypeStruct((n, d), F32),
        compiler_params=_params("parallel"),
        name="outproj",
    )(x2, g, w_out, *ys)


def _trunk(x, pos0, caches, wts, tm):
    bsz, T, d = x.shape
    n = bsz * T
    depth = wts["norm_g"].shape[0]
    w = d // 2
    x2 = x.reshape(n, d)
    tt = _pick_tile(T, 256)
    flat = lambda a: a.reshape(n, a.shape[-1])
    seq = lambda a: a.reshape(bsz, T, a.shape[-1])
    rope = None
    even, odd = [], []
    for l in range(depth):
        g = wts["norm_g"][l].reshape(6, 1, d)
        x2 = _ffn(x2, g[0], g[1], wts["w_ffn_in"][l, 0], wts["w_ffn_out"][l, 0], tm)
        if l % 2 == 0:
            e = l // 2
            u, gate, q, k, v, kb, vb, logf = _inproj_even(x2, g[2], wts["w_in_even"][e], wts["b_fox_f"][e], tm)
            if caches is None:
                h0 = jnp.zeros((bsz, 1, w), F32)
                buf0 = jnp.zeros((bsz, 8, w), F32)
                kb_all, vb_all, logf_all = seq(kb), seq(vb), seq(logf)
            else:
                fk, fv, flogf, lru_h, lru_conv = (c[e] for c in caches[:5])
                h0 = lru_h.reshape(bsz, 1, w)
                buf0 = jnp.pad(lru_conv, ((0, 0), (8 - (CONV_W - 1), 0), (0, 0)))
                past = fk.shape[1]
                kb_all = jnp.concatenate([fk.reshape(bsz, past, w).astype(BF16), seq(kb)], axis=1)
                vb_all = jnp.concatenate([fv.reshape(bsz, past, w).astype(BF16), seq(vb)], axis=1)
                logf_all = jnp.concatenate([jnp.repeat(flogf, HALF, axis=-1), seq(logf)], axis=1)
            ya, h_last, nbuf = _lru(seq(u), seq(gate), h0, buf0, wts["lru_conv_w"][e], wts["lru_conv_b"][e],
                                    wts["lru_wr"][e], wts["lru_wi"][e], wts["lru_br"][e], wts["lru_bi"][e],
                                    wts["lru_lambda"][e], tt)
            Tk = logf_all.shape[1]
            Tp = -(-Tk // LANES) * LANES
            c, ct = _cumsum(jnp.pad(logf_all, ((0, 0), (0, Tp - Tk), (0, 0))), _pick_tile(Tp, 512))
            yb = _attention(seq(q), kb_all, vb_all, (c[:, Tk - T:Tk], ct[..., :Tk]), pos0=pos0, fox=True)
            x2 = _outproj(x2, g[3], wts["w_out_even"][e], [flat(ya), flat(yb)], tm)
            even.append((k.reshape(bsz, T, FOX_HEADS, HALF), v.reshape(bsz, T, FOX_HEADS, HALF),
                         seq(logf).reshape(bsz, T, FOX_HEADS, HALF)[..., 0], h_last.reshape(bsz, w), nbuf))
        else:
            o = l // 2
            lam_init = 0.8 - 0.6 * math.exp(-0.3 * l)
            if rope is None:
                tabs = _rope_tables(T, pos0)
                if T % tm:
                    tabs = [jnp.tile(t, (n // T, 1)) for t in tabs]
                rope = (tabs, tabs[0].shape[0] // tm)
            q, k, v, kb, vb = _inproj_odd(x2, g[2], wts["w_in_odd"][o], rope[0], rope[1], tm)
            if caches is None:
                kb_all, vb_all = seq(kb), seq(vb)
            else:
                dk, dv = (c[o] for c in caches[5:])
                past = dk.shape[1]
                kb_all = jnp.concatenate([dk.reshape(bsz, past, d).astype(BF16), seq(kb)], axis=1)
                vb_all = jnp.concatenate([dv.reshape(bsz, past, d).astype(BF16), seq(vb)], axis=1)
            y = _attention(seq(q), kb_all, vb_all, (wts["diff_lambda"][o], wts["diff_subln_g"][o]),
                           pos0=pos0, fox=False, lam_init=lam_init)
            x2 = _outproj(x2, g[3], wts["w_out_odd"][o], [flat(y)], tm)
            odd.append((k.reshape(bsz, T, DIFF_HEADS, LANES), v.reshape(bsz, T, DIFF_HEADS, LANES)))
        x2 = _ffn(x2, g[4], g[5], wts["w_ffn_in"][l, 1], wts["w_ffn_out"][l, 1], tm)
    return (x2.reshape(bsz, T, d), [jnp.stack(z) for z in zip(*even)], [jnp.stack(z) for z in zip(*odd)])


def _prepare_weights(norm_g, w_ffn_in, w_ffn_out, w_in_even, b_fox_f, lru_conv_w, lru_conv_b, lru_w_gates,
                     lru_b_gates, lru_lambda, w_out_even, w_in_odd, diff_lambda, diff_subln_g, w_out_odd):
    n_even, d, _ = w_in_even.shape
    w = d // 2
    blk = w // LRU_BLOCKS
    wf = jnp.repeat(w_in_even[:, :, 5 * w:], HALF, axis=-1)
    eye = jnp.eye(LRU_BLOCKS, dtype=F32)
    dense = lambda g: jnp.einsum("encg,nm->encmg", g, eye).reshape(n_even, w, w)
    return dict(
        norm_g=norm_g, w_ffn_in=w_ffn_in.astype(BF16), w_ffn_out=w_ffn_out.astype(BF16),
        w_in_even=jnp.concatenate([w_in_even[:, :, :5 * w], wf], axis=-1).astype(BF16),
        b_fox_f=jnp.repeat(b_fox_f, HALF, axis=-1).reshape(n_even, 1, w),
        lru_conv_w=lru_conv_w, lru_conv_b=lru_conv_b.reshape(n_even, 1, w),
        lru_wr=dense(lru_w_gates[..., :blk]).astype(BF16), lru_wi=dense(lru_w_gates[..., blk:]).astype(BF16),
        lru_br=lru_b_gates[..., :blk].reshape(n_even, 1, w), lru_bi=lru_b_gates[..., blk:].reshape(n_even, 1, w),
        lru_lambda=lru_lambda.reshape(n_even, 1, w), w_out_even=w_out_even.astype(BF16),
        w_in_odd=w_in_odd.astype(BF16), diff_lambda=diff_lambda, diff_subln_g=diff_subln_g.reshape(-1, 1, LANES),
        w_out_odd=w_out_odd.astype(BF16))


def kernel(x_prompt, x_sample, cache_fox_k, cache_fox_v, cache_fox_logf, state_lru_h, state_lru_conv, cache_diff_k, cache_diff_v, norm_g, w_ffn_in, w_ffn_out, w_in_even, b_fox_f, lru_conv_w, lru_conv_b, lru_w_gates, lru_b_gates, lru_lambda, w_out_even, w_in_odd, diff_lambda, diff_subln_g, w_out_odd):
    wts = _prepare_weights(norm_g, w_ffn_in, w_ffn_out, w_in_even, b_fox_f, lru_conv_w, lru_conv_b, lru_w_gates,
                           lru_b_gates, lru_lambda, w_out_even, w_in_odd, diff_lambda, diff_subln_g, w_out_odd)
    y_p, pe, po = _trunk(x_prompt, 0, None, wts, _pick_tile(x_prompt.shape[0] * x_prompt.shape[1], 512))
    caches = (cache_fox_k, cache_fox_v, cache_fox_logf, state_lru_h, state_lru_conv, cache_diff_k, cache_diff_v)
    y_s, se, so = _trunk(x_sample, cache_fox_k.shape[2], caches, wts,
                         _pick_tile(x_sample.shape[0] * x_sample.shape[1], 256))
    return (y_p, y_s, *pe, *po, *se, *so)
```

```python
import functools
import math

import jax
import jax.numpy as jnp
from jax import lax
from jax.experimental import pallas as pl
from jax.experimental.pallas import tpu as pltpu

F32 = jnp.float32
BF16 = jnp.bfloat16

EPS = 1e-6
NEG_INF = -1e30
LRU_C = 8.0
LRU_BLOCKS = 8
CONV_W = 4
FOX_HEADS = 8
DIFF_HEADS = 8
CHUNK = 64
ROT_DIM = 16
ROPE_THETA = 500000.0

LANES = 128
HALF = LANES // 2
FF_CHUNK = 256
AHEAD = 2
LOG2E = 1.0 / math.log(2.0)
Q_SCALE = HALF ** -0.5 * LOG2E
VMEM_LIMIT = 56 * 1024 * 1024


def _params(*sem):
    return pltpu.CompilerParams(dimension_semantics=sem, vmem_limit_bytes=VMEM_LIMIT)


def _resident(shape):
    nd = len(shape)
    return pl.BlockSpec(shape, lambda *_: (0,) * nd, pipeline_mode=pl.Buffered(1))


def _rms(x, g):
    return x * lax.rsqrt(jnp.mean(x * x, axis=-1, keepdims=True) + EPS) * g


def _mm(a, b):
    return jnp.dot(a, b, preferred_element_type=F32)


def _mm_nt(a, b):
    return lax.dot_general(a, b, (((1,), (1,)), ((), ())), preferred_element_type=F32)


def _log_sigmoid(x):
    return jnp.minimum(x, 0.0) - jnp.log1p(jnp.exp(-jnp.abs(x)))


def _softplus(x):
    return jnp.maximum(x, 0.0) + jnp.log1p(jnp.exp(-jnp.abs(x)))


def _pick_tile(n, target):
    t = min(n, target)
    while n % t:
        t //= 2
    return t


def _ffn_kernel(x_ref, gpre_ref, gpost_ref, wi_ref, wo_ref, o_ref, act_ref, *, dff):
    x = x_ref[...]
    xn = _rms(x, gpre_ref[...]).astype(BF16)
    for c0 in range(0, dff, FF_CHUNK):
        g = _mm(xn, wi_ref[:, c0:c0 + FF_CHUNK])
        u = _mm(xn, wi_ref[:, dff + c0:dff + c0 + FF_CHUNK])
        act_ref[:, c0:c0 + FF_CHUNK] = ((g * jax.nn.sigmoid(g)) * u).astype(BF16)
    y = _mm(act_ref[...], wo_ref[...])
    o_ref[...] = x + 0.5 * _rms(y, gpost_ref[...])


def _ffn(x2, g_pre, g_post, w_in, w_out, tm):
    n, d = x2.shape
    dff = w_out.shape[0]
    assert dff % FF_CHUNK == 0 and n % tm == 0
    row = pl.BlockSpec((tm, d), lambda i: (i, 0))
    return pl.pallas_call(
        functools.partial(_ffn_kernel, dff=dff),
        grid=(n // tm,),
        in_specs=[row, _resident((1, d)), _resident((1, d)), _resident((d, 2 * dff)), _resident((dff, d))],
        out_specs=row,
        out_shape=jax.ShapeDtypeStruct((n, d), F32),
        scratch_shapes=[pltpu.VMEM((tm, dff), BF16)],
        compiler_params=_params("parallel"),
        name="ffn",
    )(x2, g_pre, g_post, w_in, w_out)


def _inproj_even_kernel(x_ref, g_ref, w_ref, bf_ref, u_ref, gate_ref, q_ref, k_ref, v_ref, kb_ref, vb_ref,
                        logf_ref, *, w):
    hn = _rms(x_ref[...], g_ref[...]).astype(BF16)
    u_ref[...] = _mm(hn, w_ref[:, 0:w])
    gate_ref[...] = _mm(hn, w_ref[:, w:2 * w])
    q_ref[...] = (_mm(hn, w_ref[:, 2 * w:3 * w]) * Q_SCALE).astype(BF16)
    k = _mm(hn, w_ref[:, 3 * w:4 * w])
    k_ref[...] = k
    kb_ref[...] = k.astype(BF16)
    v = _mm(hn, w_ref[:, 4 * w:5 * w])
    v_ref[...] = v
    vb_ref[...] = v.astype(BF16)
    logf_ref[...] = _log_sigmoid(_mm(hn, w_ref[:, 5 * w:6 * w]) + bf_ref[...])


def _inproj_even(x2, g, w_wide, bf_wide, tm):
    n, d = x2.shape
    w = w_wide.shape[1] // 6
    row = lambda c: pl.BlockSpec((tm, c), lambda i: (i, 0))
    f = jax.ShapeDtypeStruct((n, w), F32)
    b = jax.ShapeDtypeStruct((n, w), BF16)
    return pl.pallas_call(
        functools.partial(_inproj_even_kernel, w=w),
        grid=(n // tm,),
        in_specs=[row(d), _resident((1, d)), _resident((d, 6 * w)), _resident((1, w))],
        out_specs=[row(w)] * 8,
        out_shape=[f, f, b, f, f, b, b, f],
        compiler_params=_params("parallel"),
        name="inproj_even",
    )(x2, g, w_wide, bf_wide)


def _shift_rows(x, d, fill):
    rows = lax.broadcasted_iota(jnp.int32, x.shape, 0)
    return jnp.where(rows >= d, pltpu.roll(x, d, 0), fill)


def _lru_kernel(u_ref, gate_ref, h0_ref, buf0_ref, cw_ref, cb_ref, wr_ref, wi_ref, br_ref, bi_ref, lam_ref,
                ya_ref, hlast_ref, nbuf_ref, ubuf, hcar, *, tt):
    t = pl.program_id(1)
    pad = ubuf.shape[0] - tt

    @pl.when(t == 0)
    def _():
        ubuf[0:pad, :] = buf0_ref[0]
        hcar[...] = h0_ref[0]

    u = u_ref[0]
    ubuf[pad:pad + tt, :] = u
    uc = cb_ref[...]
    for j in range(CONV_W):
        o = pad - (CONV_W - 1) + j
        uc = uc + ubuf[o:o + tt, :] * cw_ref[j:j + 1, :]
    ucb = uc.astype(BF16)
    r = jax.nn.sigmoid(_mm(ucb, wr_ref[...]) + br_ref[...])
    ig = jax.nn.sigmoid(_mm(ucb, wi_ref[...]) + bi_ref[...])
    log_a = (-LRU_C * r) * _softplus(-lam_ref[...])
    a = jnp.exp(log_a)
    bx = jnp.sqrt(-jnp.tanh(log_a) * (a * a + 1.0)) * (ig * uc)
    rows = lax.broadcasted_iota(jnp.int32, bx.shape, 0)
    bx = bx + jnp.where(rows == 0, a * hcar[...], 0.0)
    d = 1
    while d < tt:
        bx = a * _shift_rows(bx, d, 0.0) + bx
        a = a * _shift_rows(a, d, 1.0)
        d *= 2
    hcar[...] = bx[tt - 1:tt, :]
    ya_ref[0] = (bx * jax.nn.gelu(gate_ref[0])).astype(BF16)
    ubuf[0:pad, :] = ubuf[tt:tt + pad, :]

    @pl.when(t == pl.num_programs(1) - 1)
    def _():
        hlast_ref[0] = bx[tt - 1:tt, :]
        nbuf_ref[0] = ubuf[pad - (CONV_W - 1):pad, :]


def _lru(u, gate, h0, buf0, cw, cb, wr, wi, br, bi, lam, tt):
    bsz, T, w = u.shape
    pad = 8
    seq = pl.BlockSpec((1, tt, w), lambda b, t: (b, t, 0))
    per_b = lambda r: pl.BlockSpec((1, r, w), lambda b, t: (b, 0, 0))
    return pl.pallas_call(
        functools.partial(_lru_kernel, tt=tt),
        grid=(bsz, T // tt),
        in_specs=[seq, seq, per_b(1), per_b(pad), _resident((CONV_W, w)), _resident((1, w)), _resident((w, w)),
                  _resident((w, w)), _resident((1, w)), _resident((1, w)), _resident((1, w))],
        out_specs=[seq, per_b(1), per_b(CONV_W - 1)],
        out_shape=[jax.ShapeDtypeStruct((bsz, T, w), BF16), jax.ShapeDtypeStruct((bsz, 1, w), F32),
                   jax.ShapeDtypeStruct((bsz, CONV_W - 1, w), F32)],
        scratch_shapes=[pltpu.VMEM((tt + pad, w), F32), pltpu.VMEM((1, w), F32)],
        compiler_params=_params("parallel", "arbitrary"),
        name="conv_rglru",
    )(u, gate, h0, buf0, cw, cb, wr, wi, br, bi, lam)


def _cumsum_kernel(x_ref, ck_ref, cq_ref, car, *, tt):
    @pl.when(pl.program_id(1) == 0)
    def _():
        car[...] = jnp.zeros_like(car)

    x = x_ref[0]
    rows = lax.broadcasted_iota(jnp.int32, x.shape, 0)
    x = x + jnp.where(rows == 0, car[...], 0.0)
    d = 1
    while d < tt:
        x = x + _shift_rows(x, d, 0.0)
        d *= 2
    car[...] = x[tt - 1:tt, :]
    x = x * LOG2E
    lo = lax.broadcasted_iota(jnp.int32, (1, LANES), 1) < HALF
    for s in range(x.shape[1] // LANES):
        xs = x[:, s * LANES:(s + 1) * LANES]
        sw = pltpu.roll(xs, HALF, 1)
        ck_ref[0, :, (2 * s) * LANES:(2 * s + 1) * LANES] = jnp.where(lo, xs, sw)
        ck_ref[0, :, (2 * s + 1) * LANES:(2 * s + 2) * LANES] = jnp.where(lo, sw, xs)
        tr = xs.T
        cq_ref[0, s, 0:1, :] = tr[0:1, :]
        cq_ref[0, s, 1:2, :] = tr[HALF:HALF + 1, :]


def _cumsum(x, tt):
    bsz, T, w = x.shape
    ns = w // LANES
    return pl.pallas_call(
        functools.partial(_cumsum_kernel, tt=tt),
        grid=(bsz, T // tt),
        in_specs=[pl.BlockSpec((1, tt, w), lambda b, t: (b, t, 0))],
        out_specs=[pl.BlockSpec((1, tt, 2 * w), lambda b, t: (b, t, 0)),
                   pl.BlockSpec((1, ns, 2, tt), lambda b, t: (b, 0, 0, t))],
        out_shape=[jax.ShapeDtypeStruct((bsz, T, 2 * w), F32), jax.ShapeDtypeStruct((bsz, ns, 2, T), F32)],
        scratch_shapes=[pltpu.VMEM((1, w), F32)],
        compiler_params=_params("parallel", "arbitrary"),
        name="logf_cumsum",
    )(x)


def _attn_kernel(*refs, tq, tk, nk_total, kv_len, pos0, fox, lam_init):
    if fox:
        q_ref, k_ref, vt_ref, cq_ref, ck_ref, o_ref, m_sc, l_sc, acc_sc, s0_sc = refs
    else:
        q_ref, k_ref, vt_ref, lp_ref, sg_ref, o_ref, m_sc, l_sc, acc_sc, s0_sc = refs
    i = pl.program_id(2)
    q0 = pos0 + i * tq
    shift = CHUNK.bit_length() - 1
    lo = lax.broadcasted_iota(jnp.int32, (1, LANES), 1) < HALF
    spb = q_ref.shape[2] // LANES
    qm = []
    for sl in range(spb):
        q = q_ref[0, :, sl * LANES:(sl + 1) * LANES]
        qm += [jnp.where(lo, q, jnp.zeros_like(q)), jnp.where(lo, jnp.zeros_like(q), q)]
    if fox:
        n_full = (q0 + 1) // tk
        n_need = (q0 + tq - 1) // tk + 1
    else:
        n_full = (((q0 >> shift) + 1) << shift) // tk
        n_need = (((((q0 + tq - 1) >> shift) + 1) << shift) + tk - 1) // tk
    n_need = jnp.minimum(n_need, nk_total)
    n_full = jnp.minimum(jnp.minimum(n_full, kv_len // tk), n_need)

    m_sc[...] = jnp.full_like(m_sc, NEG_INF)
    l_sc[...] = jnp.zeros_like(l_sc)
    acc_sc[...] = jnp.zeros_like(acc_sc)

    n_maps = 2 * spb

    def scores(g, j):
        k0 = j * tk if isinstance(j, int) else pl.multiple_of(j * tk, tk)
        return _mm_nt(k_ref[0, pl.ds(k0, tk), (g // 2) * LANES:(g // 2 + 1) * LANES], qm[g])

    for a in range(AHEAD):
        s0_sc[a] = scores(a % n_maps, a // n_maps)

    def tile(j, masked):
        k0 = pl.multiple_of(j * tk, tk)
        if masked:
            kpos = k0 + lax.broadcasted_iota(jnp.int32, (tk, tq), 0)
            qpos = q0 + lax.broadcasted_iota(jnp.int32, (tk, tq), 1)
            vis = (qpos >= kpos) if fox else ((qpos >> shift) >= (kpos >> shift))
            if kv_len < nk_total * tk:
                vis = vis & (kpos < kv_len)
        pending = [s0_sc[a] for a in range(AHEAD)]
        for g in range(n_maps):
            sl, c = divmod(g, 2)
            dj, g_ahead = divmod(g + AHEAD, n_maps)
            pending.append(scores(g_ahead, jnp.minimum(j + dj, nk_total - 1) if dj else j))
            s = pending.pop(0)
            if fox:
                ck = ck_ref[0, pl.ds(k0, tk), g * LANES:(g + 1) * LANES]
                s = s - jnp.concatenate([ck] * (tq // LANES), axis=1)
            if masked:
                s = jnp.where(vis, s, NEG_INF)
            m_prev = m_sc[g]
            m_tile = jnp.max(s, axis=0, keepdims=True)
            if fox:
                cq = cq_ref[0, sl, c:c + 1, :]
                m_new = jnp.maximum(m_prev, m_tile + cq)
                p = jnp.exp2(s - (m_new - cq))
            else:
                m_new = jnp.maximum(m_prev, m_tile)
                p = jnp.exp2(s - m_new)
            alpha = jnp.exp2(m_prev - m_new)
            l_sc[g] = alpha * l_sc[g] + jnp.sum(p, axis=0, keepdims=True)
            m_sc[g] = m_new
            pt = p.astype(BF16)
            vt = vt_ref[0, sl, j]
            if fox:
                r = slice(c * HALF, (c + 1) * HALF)
                acc_sc[sl, r, :] = alpha * acc_sc[sl, r, :] + _mm(vt[r, :], pt)
            else:
                acc_sc[g] = alpha * acc_sc[g] + _mm(vt, pt)
        for a in range(AHEAD):
            s0_sc[a] = pending[a]

    def full_body(j, carry):
        tile(j, False)
        return carry

    def masked_body(j, carry):
        tile(j, True)
        return carry

    lax.fori_loop(0, n_full, full_body, 0)
    lax.fori_loop(n_full, n_need, masked_body, 0)

    for sl in range(spb):
        a, b = 2 * sl, 2 * sl + 1
        if fox:
            ot = jnp.concatenate([acc_sc[sl, 0:HALF, :] / l_sc[a], acc_sc[sl, HALF:LANES, :] / l_sc[b]], axis=0)
            o = ot.T
        else:
            lp = lp_ref[...]
            lam = (jnp.exp(jnp.sum(lp[0:1] * lp[1:2], axis=-1, keepdims=True))
                   - jnp.exp(jnp.sum(lp[2:3] * lp[3:4], axis=-1, keepdims=True)) + lam_init)
            ot = acc_sc[a] / l_sc[a] - lam * (acc_sc[b] / l_sc[b])
            ot = ot * lax.rsqrt(jnp.mean(ot * ot, axis=0, keepdims=True) + EPS)
            o = (ot.T * sg_ref[...]) * (1.0 - lam_init)
        o_ref[0, :, sl * LANES:(sl + 1) * LANES] = o.astype(BF16)


def _attention(q, k, v, extra, *, pos0, kv_len, fox, lam_init=0.0, tq=256, tk=512, spb=2):
    bsz, Tq, w = q.shape
    Tk = k.shape[1]
    ns = w // LANES
    Tq_pad = -(-Tq // LANES) * LANES
    tq = _pick_tile(Tq_pad, tq)
    tk = tk if Tk % tk == 0 else Tk
    nk = Tk // tk
    q = jnp.pad(q, ((0, 0), (0, Tq_pad - Tq), (0, 0)))
    vt = v.reshape(bsz, nk, tk, ns, LANES).transpose(0, 3, 1, 4, 2)
    assert ns % spb == 0
    sw = spb * LANES
    qspec = pl.BlockSpec((1, tq, sw), lambda b, s, i: (b, i, s))
    kspec = pl.BlockSpec((1, Tk, sw), lambda b, s, i: (b, 0, s))
    vspec = pl.BlockSpec((1, spb, nk, LANES, tk), lambda b, s, i: (b, s, 0, 0, 0))
    if fox:
        cq, ck = extra
        extra = (jnp.pad(cq, ((0, 0), (0, 0), (0, 0), (0, Tq_pad - Tq))), ck)
        especs = [pl.BlockSpec((1, spb, 2, tq), lambda b, s, i: (b, s, 0, i)),
                  pl.BlockSpec((1, Tk, 2 * sw), lambda b, s, i: (b, 0, s))]
    else:
        lp, sg = extra
        especs = [pl.BlockSpec(lp.shape, lambda b, s, i: (0, 0)), pl.BlockSpec(sg.shape, lambda b, s, i: (0, 0))]
    out = pl.pallas_call(
        functools.partial(_attn_kernel, tq=tq, tk=tk, nk_total=nk, kv_len=kv_len, pos0=pos0, fox=fox,
                          lam_init=lam_init),
        grid=(bsz, ns // spb, Tq_pad // tq),
        in_specs=[qspec, kspec, vspec] + especs,
        out_specs=qspec,
        out_shape=jax.ShapeDtypeStruct((bsz, Tq_pad, w), BF16),
        scratch_shapes=[pltpu.VMEM((2 * spb, 1, tq), F32), pltpu.VMEM((2 * spb, 1, tq), F32),
                        pltpu.VMEM((spb if fox else 2 * spb, LANES, tq), F32), pltpu.VMEM((AHEAD, tk, tq), F32)],
        compiler_params=_params("parallel", "parallel", "arbitrary"),
        name="fox_attention" if fox else "diff_attention",
    )(q, k, vt, *extra)
    return out[:, :Tq]


def _rope_table_kernel(inv_ref, c_ref, s1_ref, s2_ref, *, pos0):
    T = c_ref.shape[0]
    pos = (pos0 + lax.broadcasted_iota(jnp.int32, (T, LANES), 0)).astype(F32)
    ang = pos * inv_ref[...]
    lane = lax.broadcasted_iota(jnp.int32, (T, LANES), 1) % HALF
    sn = jnp.sin(ang)
    c_ref[...] = jnp.cos(ang)
    s1_ref[...] = jnp.where(lane < ROT_DIM // 2, -sn, 0.0)
    s2_ref[...] = jnp.where(lane >= ROT_DIM // 2, sn, 0.0)


def _rope_tables(T, pos0):
    half = ROT_DIM // 2
    inv = ROPE_THETA ** (-jnp.arange(half, dtype=F32) * (2.0 / ROT_DIM))
    lane = jnp.arange(LANES) % HALF
    inv_lane = jnp.where(lane < ROT_DIM, inv[lane % half], 0.0).reshape(1, LANES).astype(F32)
    tab = jax.ShapeDtypeStruct((T, LANES), F32)
    return pl.pallas_call(
        functools.partial(_rope_table_kernel, pos0=pos0),
        out_shape=[tab, tab, tab],
        name="rope_tables",
    )(inv_lane)


def _inproj_odd_kernel(x_ref, g_ref, w_ref, c_ref, s1_ref, s2_ref, q_ref, k_ref, v_ref, kb_ref, vb_ref, *, w):
    hn = _rms(x_ref[...], g_ref[...]).astype(BF16)
    cs, s1, s2 = c_ref[...], s1_ref[...], s2_ref[...]
    half = ROT_DIM // 2

    def rope(t):
        out = []
        for h in range(w // LANES):
            s = t[:, h * LANES:(h + 1) * LANES]
            out.append(s * cs + pltpu.roll(s, LANES - half, 1) * s1 + pltpu.roll(s, half, 1) * s2)
        return jnp.concatenate(out, axis=1)

    q_ref[...] = (rope(_mm(hn, w_ref[:, 0:w])) * Q_SCALE).astype(BF16)
    k = rope(_mm(hn, w_ref[:, w:2 * w]))
    k_ref[...] = k
    kb_ref[...] = k.astype(BF16)
    v = _mm(hn, w_ref[:, 2 * w:3 * w])
    v_ref[...] = v
    vb_ref[...] = v.astype(BF16)


def _inproj_odd(x2, g, w_in, tabs, tab_blocks, tm):
    n, d = x2.shape
    w = w_in.shape[1] // 3
    row = lambda c: pl.BlockSpec((tm, c), lambda i: (i, 0))
    tab = pl.BlockSpec((tm, LANES), lambda i: (i % tab_blocks, 0))
    f = jax.ShapeDtypeStruct((n, w), F32)
    b = jax.ShapeDtypeStruct((n, w), BF16)
    return pl.pallas_call(
        functools.partial(_inproj_odd_kernel, w=w),
        grid=(n // tm,),
        in_specs=[row(d), _resident((1, d)), _resident((d, 3 * w)), tab, tab, tab],
        out_specs=[row(w)] * 5,
        out_shape=[b, f, f, b, b],
        compiler_params=_params("parallel"),
        name="inproj_odd",
    )(x2, g, w_in, *tabs)


def _outproj_kernel(*refs):
    x_ref, g_ref, w_ref = refs[:3]
    o_ref = refs[-1]
    y = jnp.concatenate([r[...] for r in refs[3:-1]], axis=1)
    o_ref[...] = x_ref[...] + _rms(_mm(y, w_ref[...]), g_ref[...])


def _outproj(x2, g, w_out, ys, tm):
    n, d = x2.shape
    row = lambda c: pl.BlockSpec((tm, c), lambda i: (i, 0))
    return pl.pallas_call(
        _outproj_kernel,
        grid=(n // tm,),
        in_specs=[row(d), _resident((1, d)), _resident(w_out.shape)] + [row(y.shape[1]) for y in ys],
        out_specs=row(d),
        out_shape=jax.ShapeDtypeStruct((n, d), F32),
        compiler_params=_params("parallel"),
        name="outproj",
    )(x2, g, w_out, *ys)


def _trunk(x, pos0, caches, wts, tm):
    bsz, T, d = x.shape
    n = bsz * T
    depth = wts["norm_g"].shape[0]
    w = d // 2
    x2 = x.reshape(n, d)
    tt = _pick_tile(T, 256)
    flat = lambda a: a.reshape(n, a.shape[-1])
    seq = lambda a: a.reshape(bsz, T, a.shape[-1])
    n_keys = T if caches is None else T + caches[0].shape[2]
    key_pad = -n_keys % LANES
    pad_keys = lambda a: jnp.pad(a, ((0, 0), (0, key_pad), (0, 0)))
    rope = None
    even, odd = [], []
    for l in range(depth):
        g = wts["norm_g"][l].reshape(6, 1, d)
        x2 = _ffn(x2, g[0], g[1], wts["w_ffn_in"][l, 0], wts["w_ffn_out"][l, 0], tm)
        if l % 2 == 0:
            e = l // 2
            u, gate, q, k, v, kb, vb, logf = _inproj_even(x2, g[2], wts["w_in_even"][e], wts["b_fox_f"][e], tm)
            if caches is None:
                h0 = jnp.zeros((bsz, 1, w), F32)
                buf0 = jnp.zeros((bsz, 8, w), F32)
                kb_all, vb_all, logf_all = seq(kb), seq(vb), seq(logf)
            else:
                fk, fv, flogf, lru_h, lru_conv = (c[e] for c in caches[:5])
                h0 = lru_h.reshape(bsz, 1, w)
                buf0 = jnp.pad(lru_conv, ((0, 0), (8 - (CONV_W - 1), 0), (0, 0)))
                past = fk.shape[1]
                kb_all = jnp.concatenate([fk.reshape(bsz, past, w).astype(BF16), seq(kb)], axis=1)
                vb_all = jnp.concatenate([fv.reshape(bsz, past, w).astype(BF16), seq(vb)], axis=1)
                logf_all = jnp.concatenate([jnp.repeat(flogf, HALF, axis=-1), seq(logf)], axis=1)
            ya, h_last, nbuf = _lru(seq(u), seq(gate), h0, buf0, wts["lru_conv_w"][e], wts["lru_conv_b"][e],
                                    wts["lru_wr"][e], wts["lru_wi"][e], wts["lru_br"][e], wts["lru_bi"][e],
                                    wts["lru_lambda"][e], tt)
            Tk = logf_all.shape[1]
            ck, cq = _cumsum(pad_keys(logf_all), _pick_tile(Tk + key_pad, 512))
            yb = _attention(seq(q), pad_keys(kb_all), pad_keys(vb_all), (cq[..., Tk - T:Tk], ck),
                            pos0=pos0, kv_len=Tk, fox=True)
            x2 = _outproj(x2, g[3], wts["w_out_even"][e], [flat(ya), flat(yb)], tm)
            even.append((k.reshape(bsz, T, FOX_HEADS, HALF), v.reshape(bsz, T, FOX_HEADS, HALF),
                         seq(logf).reshape(bsz, T, FOX_HEADS, HALF)[..., 0], h_last.reshape(bsz, w), nbuf))
        else:
            o = l // 2
            lam_init = 0.8 - 0.6 * math.exp(-0.3 * l)
            if rope is None:
                tabs = _rope_tables(T, pos0)
                if T % tm:
                    tabs = [jnp.tile(t, (n // T, 1)) for t in tabs]
                rope = (tabs, tabs[0].shape[0] // tm)
            q, k, v, kb, vb = _inproj_odd(x2, g[2], wts["w_in_odd"][o], rope[0], rope[1], tm)
            if caches is None:
                kb_all, vb_all = seq(kb), seq(vb)
            else:
                dk, dv = (c[o] for c in caches[5:])
                past = dk.shape[1]
                kb_all = jnp.concatenate([dk.reshape(bsz, past, d).astype(BF16), seq(kb)], axis=1)
                vb_all = jnp.concatenate([dv.reshape(bsz, past, d).astype(BF16), seq(vb)], axis=1)
            y = _attention(seq(q), pad_keys(kb_all), pad_keys(vb_all),
                           (wts["diff_lambda"][o], wts["diff_subln_g"][o]),
                           pos0=pos0, kv_len=kb_all.shape[1], fox=False, lam_init=lam_init)
            x2 = _outproj(x2, g[3], wts["w_out_odd"][o], [flat(y)], tm)
            odd.append((k.reshape(bsz, T, DIFF_HEADS, LANES), v.reshape(bsz, T, DIFF_HEADS, LANES)))
        x2 = _ffn(x2, g[4], g[5], wts["w_ffn_in"][l, 1], wts["w_ffn_out"][l, 1], tm)
    return (x2.reshape(bsz, T, d), [jnp.stack(z) for z in zip(*even)], [jnp.stack(z) for z in zip(*odd)])


def _prepare_weights(norm_g, w_ffn_in, w_ffn_out, w_in_even, b_fox_f, lru_conv_w, lru_conv_b, lru_w_gates,
                     lru_b_gates, lru_lambda, w_out_even, w_in_odd, diff_lambda, diff_subln_g, w_out_odd):
    n_even, d, _ = w_in_even.shape
    w = d // 2
    blk = w // LRU_BLOCKS
    wf = jnp.repeat(w_in_even[:, :, 5 * w:], HALF, axis=-1)
    eye = jnp.eye(LRU_BLOCKS, dtype=F32)
    dense = lambda g: jnp.einsum("encg,nm->encmg", g, eye).reshape(n_even, w, w)
    return dict(
        norm_g=norm_g, w_ffn_in=w_ffn_in.astype(BF16), w_ffn_out=w_ffn_out.astype(BF16),
        w_in_even=jnp.concatenate([w_in_even[:, :, :5 * w], wf], axis=-1).astype(BF16),
        b_fox_f=jnp.repeat(b_fox_f, HALF, axis=-1).reshape(n_even, 1, w),
        lru_conv_w=lru_conv_w, lru_conv_b=lru_conv_b.reshape(n_even, 1, w),
        lru_wr=dense(lru_w_gates[..., :blk]).astype(BF16), lru_wi=dense(lru_w_gates[..., blk:]).astype(BF16),
        lru_br=lru_b_gates[..., :blk].reshape(n_even, 1, w), lru_bi=lru_b_gates[..., blk:].reshape(n_even, 1, w),
        lru_lambda=lru_lambda.reshape(n_even, 1, w), w_out_even=w_out_even.astype(BF16),
        w_in_odd=w_in_odd.astype(BF16), diff_lambda=diff_lambda, diff_subln_g=diff_subln_g.reshape(-1, 1, LANES),
        w_out_odd=w_out_odd.astype(BF16))


def kernel(x_prompt, x_sample, cache_fox_k, cache_fox_v, cache_fox_logf, state_lru_h, state_lru_conv, cache_diff_k, cache_diff_v, norm_g, w_ffn_in, w_ffn_out, w_in_even, b_fox_f, lru_conv_w, lru_conv_b, lru_w_gates, lru_b_gates, lru_lambda, w_out_even, w_in_odd, diff_lambda, diff_subln_g, w_out_odd):
    wts = _prepare_weights(norm_g, w_ffn_in, w_ffn_out, w_in_even, b_fox_f, lru_conv_w, lru_conv_b, lru_w_gates,
                           lru_b_gates, lru_lambda, w_out_even, w_in_odd, diff_lambda, diff_subln_g, w_out_odd)
    y_p, pe, po = _trunk(x_prompt, 0, None, wts, _pick_tile(x_prompt.shape[0] * x_prompt.shape[1], 512))
    caches = (cache_fox_k, cache_fox_v, cache_fox_logf, state_lru_h, state_lru_conv, cache_diff_k, cache_diff_v)
    y_s, se, so = _trunk(x_sample, cache_fox_k.shape[2], caches, wts,
                         _pick_tile(x_sample.shape[0] * x_sample.shape[1], 256))
    return (y_p, y_s, *pe, *po, *se, *so)
```

```python
import functools
import math

import jax
import jax.numpy as jnp
from jax import lax
from jax.experimental import pallas as pl
from jax.experimental.pallas import tpu as pltpu

F32 = jnp.float32
BF16 = jnp.bfloat16

EPS = 1e-6
NEG_INF = -1e30
LRU_C = 8.0
LRU_BLOCKS = 8
CONV_W = 4
FOX_HEADS = 8
DIFF_HEADS = 8
CHUNK = 64
ROT_DIM = 16
ROPE_THETA = 500000.0

LANES = 128
HALF = LANES // 2
FF_CHUNK = 256
AHEAD = 2
LOG2E = 1.0 / math.log(2.0)
Q_SCALE = HALF ** -0.5 * LOG2E
VMEM_LIMIT = 56 * 1024 * 1024


def _params(*sem):
    return pltpu.CompilerParams(dimension_semantics=sem, vmem_limit_bytes=VMEM_LIMIT)


def _resident(shape):
    nd = len(shape)
    return pl.BlockSpec(shape, lambda *_: (0,) * nd, pipeline_mode=pl.Buffered(1))


def _rms(x, g):
    return x * lax.rsqrt(jnp.mean(x * x, axis=-1, keepdims=True) + EPS) * g


def _mm(a, b):
    return jnp.dot(a, b, preferred_element_type=F32)


def _mm_nt(a, b):
    return lax.dot_general(a, b, (((1,), (1,)), ((), ())), preferred_element_type=F32)


def _log_sigmoid(x):
    return jnp.minimum(x, 0.0) - jnp.log1p(jnp.exp(-jnp.abs(x)))


def _softplus(x):
    return jnp.maximum(x, 0.0) + jnp.log1p(jnp.exp(-jnp.abs(x)))


def _pick_tile(n, target):
    t = min(n, target)
    while n % t:
        t //= 2
    return t


def _ffn_kernel(x_ref, gpre_ref, gpost_ref, wi_ref, wo_ref, o_ref, act_ref, *, dff):
    x = x_ref[...]
    xn = _rms(x, gpre_ref[...]).astype(BF16)
    for c0 in range(0, dff, FF_CHUNK):
        g = _mm(xn, wi_ref[:, c0:c0 + FF_CHUNK])
        u = _mm(xn, wi_ref[:, dff + c0:dff + c0 + FF_CHUNK])
        act_ref[:, c0:c0 + FF_CHUNK] = ((g * jax.nn.sigmoid(g)) * u).astype(BF16)
    y = _mm(act_ref[...], wo_ref[...])
    o_ref[...] = x + 0.5 * _rms(y, gpost_ref[...])


def _ffn(x2, g_pre, g_post, w_in, w_out, tm):
    n, d = x2.shape
    dff = w_out.shape[0]
    assert dff % FF_CHUNK == 0 and n % tm == 0
    row = pl.BlockSpec((tm, d), lambda i: (i, 0))
    return pl.pallas_call(
        functools.partial(_ffn_kernel, dff=dff),
        grid=(n // tm,),
        in_specs=[row, _resident((1, d)), _resident((1, d)), _resident((d, 2 * dff)), _resident((dff, d))],
        out_specs=row,
        out_shape=jax.ShapeDtypeStruct((n, d), F32),
        scratch_shapes=[pltpu.VMEM((tm, dff), BF16)],
        compiler_params=_params("parallel"),
        name="ffn",
    )(x2, g_pre, g_post, w_in, w_out)


def _inproj_even_kernel(x_ref, g_ref, w_ref, bf_ref, u_ref, gate_ref, q_ref, k_ref, v_ref, kb_ref, vb_ref,
                        logf_ref, *, w):
    hn = _rms(x_ref[...], g_ref[...]).astype(BF16)
    u_ref[...] = _mm(hn, w_ref[:, 0:w])
    gate_ref[...] = _mm(hn, w_ref[:, w:2 * w])
    q_ref[...] = (_mm(hn, w_ref[:, 2 * w:3 * w]) * Q_SCALE).astype(BF16)
    k = _mm(hn, w_ref[:, 3 * w:4 * w])
    k_ref[...] = k
    kb_ref[...] = k.astype(BF16)
    v = _mm(hn, w_ref[:, 4 * w:5 * w])
    v_ref[...] = v
    vb_ref[...] = v.astype(BF16)
    logf_ref[...] = _log_sigmoid(_mm(hn, w_ref[:, 5 * w:6 * w]) + bf_ref[...])


def _inproj_even(x2, g, w_wide, bf_wide, tm):
    n, d = x2.shape
    w = w_wide.shape[1] // 6
    row = lambda c: pl.BlockSpec((tm, c), lambda i: (i, 0))
    f = jax.ShapeDtypeStruct((n, w), F32)
    b = jax.ShapeDtypeStruct((n, w), BF16)
    return pl.pallas_call(
        functools.partial(_inproj_even_kernel, w=w),
        grid=(n // tm,),
        in_specs=[row(d), _resident((1, d)), _resident((d, 6 * w)), _resident((1, w))],
        out_specs=[row(w)] * 8,
        out_shape=[f, f, b, f, f, b, b, f],
        compiler_params=_params("parallel"),
        name="inproj_even",
    )(x2, g, w_wide, bf_wide)


def _shift_rows(x, d, fill):
    rows = lax.broadcasted_iota(jnp.int32, x.shape, 0)
    return jnp.where(rows >= d, pltpu.roll(x, d, 0), fill)


def _lru_kernel(u_ref, gate_ref, h0_ref, buf0_ref, cw_ref, cb_ref, wr_ref, wi_ref, br_ref, bi_ref, lam_ref,
                ya_ref, hlast_ref, nbuf_ref, ubuf, hcar, *, tt):
    t = pl.program_id(1)
    pad = ubuf.shape[0] - tt

    @pl.when(t == 0)
    def _():
        ubuf[0:pad, :] = buf0_ref[0]
        hcar[...] = h0_ref[0]

    u = u_ref[0]
    ubuf[pad:pad + tt, :] = u
    uc = cb_ref[...]
    for j in range(CONV_W):
        o = pad - (CONV_W - 1) + j
        uc = uc + ubuf[o:o + tt, :] * cw_ref[j:j + 1, :]
    ucb = uc.astype(BF16)
    r = jax.nn.sigmoid(_mm(ucb, wr_ref[...]) + br_ref[...])
    ig = jax.nn.sigmoid(_mm(ucb, wi_ref[...]) + bi_ref[...])
    log_a = (-LRU_C * r) * _softplus(-lam_ref[...])
    a = jnp.exp(log_a)
    bx = jnp.sqrt(-jnp.tanh(log_a) * (a * a + 1.0)) * (ig * uc)
    rows = lax.broadcasted_iota(jnp.int32, bx.shape, 0)
    bx = bx + jnp.where(rows == 0, a * hcar[...], 0.0)
    d = 1
    while d < tt:
        bx = a * _shift_rows(bx, d, 0.0) + bx
        a = a * _shift_rows(a, d, 1.0)
        d *= 2
    hcar[...] = bx[tt - 1:tt, :]
    ya_ref[0] = (bx * jax.nn.gelu(gate_ref[0])).astype(BF16)
    ubuf[0:pad, :] = ubuf[tt:tt + pad, :]

    @pl.when(t == pl.num_programs(1) - 1)
    def _():
        hlast_ref[0] = bx[tt - 1:tt, :]
        nbuf_ref[0] = ubuf[pad - (CONV_W - 1):pad, :]


def _lru(u, gate, h0, buf0, cw, cb, wr, wi, br, bi, lam, tt):
    bsz, T, w = u.shape
    pad = 8
    seq = pl.BlockSpec((1, tt, w), lambda b, t: (b, t, 0))
    per_b = lambda r: pl.BlockSpec((1, r, w), lambda b, t: (b, 0, 0))
    return pl.pallas_call(
        functools.partial(_lru_kernel, tt=tt),
        grid=(bsz, T // tt),
        in_specs=[seq, seq, per_b(1), per_b(pad), _resident((CONV_W, w)), _resident((1, w)), _resident((w, w)),
                  _resident((w, w)), _resident((1, w)), _resident((1, w)), _resident((1, w))],
        out_specs=[seq, per_b(1), per_b(CONV_W - 1)],
        out_shape=[jax.ShapeDtypeStruct((bsz, T, w), BF16), jax.ShapeDtypeStruct((bsz, 1, w), F32),
                   jax.ShapeDtypeStruct((bsz, CONV_W - 1, w), F32)],
        scratch_shapes=[pltpu.VMEM((tt + pad, w), F32), pltpu.VMEM((1, w), F32)],
        compiler_params=_params("parallel", "arbitrary"),
        name="conv_rglru",
    )(u, gate, h0, buf0, cw, cb, wr, wi, br, bi, lam)


def _cumsum_kernel(x_ref, ck_ref, cq_ref, car, *, tt):
    @pl.when(pl.program_id(1) == 0)
    def _():
        car[...] = jnp.zeros_like(car)

    x = x_ref[0]
    rows = lax.broadcasted_iota(jnp.int32, x.shape, 0)
    x = x + jnp.where(rows == 0, car[...], 0.0)
    d = 1
    while d < tt:
        x = x + _shift_rows(x, d, 0.0)
        d *= 2
    car[...] = x[tt - 1:tt, :]
    x = x * LOG2E
    lo = lax.broadcasted_iota(jnp.int32, (1, LANES), 1) < HALF
    for s in range(x.shape[1] // LANES):
        xs = x[:, s * LANES:(s + 1) * LANES]
        sw = pltpu.roll(xs, HALF, 1)
        ck_ref[0, :, (2 * s) * LANES:(2 * s + 1) * LANES] = jnp.where(lo, xs, sw)
        ck_ref[0, :, (2 * s + 1) * LANES:(2 * s + 2) * LANES] = jnp.where(lo, sw, xs)
        tr = xs.T
        cq_ref[0, s, 0:1, :] = tr[0:1, :]
        cq_ref[0, s, 1:2, :] = tr[HALF:HALF + 1, :]


def _cumsum(x, tt):
    bsz, T, w = x.shape
    ns = w // LANES
    return pl.pallas_call(
        functools.partial(_cumsum_kernel, tt=tt),
        grid=(bsz, T // tt),
        in_specs=[pl.BlockSpec((1, tt, w), lambda b, t: (b, t, 0))],
        out_specs=[pl.BlockSpec((1, tt, 2 * w), lambda b, t: (b, t, 0)),
                   pl.BlockSpec((1, ns, 2, tt), lambda b, t: (b, 0, 0, t))],
        out_shape=[jax.ShapeDtypeStruct((bsz, T, 2 * w), F32), jax.ShapeDtypeStruct((bsz, ns, 2, T), F32)],
        scratch_shapes=[pltpu.VMEM((1, w), F32)],
        compiler_params=_params("parallel", "arbitrary"),
        name="logf_cumsum",
    )(x)


def _attn_kernel(*refs, tq, tk, nk_total, kv_len, pos0, fox, lam_init):
    if fox:
        q_ref, k_ref, vt_ref, cq_ref, ck_ref, o_ref, m_sc, l_sc, acc_sc, s_sc = refs
    else:
        q_ref, k_ref, vt_ref, lp_ref, sg_ref, o_ref, m_sc, l_sc, acc_sc, s_sc = refs
    i = pl.program_id(2)
    q0 = pos0 + i * tq
    shift = CHUNK.bit_length() - 1
    lo = lax.broadcasted_iota(jnp.int32, (1, LANES), 1) < HALF
    spb = q_ref.shape[2] // LANES
    qm = []
    for sl in range(spb):
        q = q_ref[0, :, sl * LANES:(sl + 1) * LANES]
        qm += [jnp.where(lo, q, jnp.zeros_like(q)), jnp.where(lo, jnp.zeros_like(q), q)]
    if fox:
        n_full = (q0 + 1) // tk
        n_need = (q0 + tq - 1) // tk + 1
    else:
        n_full = (((q0 >> shift) + 1) << shift) // tk
        n_need = (((((q0 + tq - 1) >> shift) + 1) << shift) + tk - 1) // tk
    n_need = jnp.minimum(n_need, nk_total)
    n_full = jnp.minimum(jnp.minimum(n_full, kv_len // tk), n_need)

    m_sc[...] = jnp.full_like(m_sc, NEG_INF)
    l_sc[...] = jnp.zeros_like(l_sc)
    acc_sc[...] = jnp.zeros_like(acc_sc)

    n_maps = 2 * spb

    def scores(g, j):
        k0 = j * tk if isinstance(j, int) else pl.multiple_of(j * tk, tk)
        lanes = slice((g // 2) * LANES, (g // 2 + 1) * LANES)
        hk = tk // 2
        return jnp.concatenate([_mm_nt(k_ref[0, pl.ds(k0 + r, hk), lanes], qm[g]) for r in (0, hk)], axis=0)

    assert AHEAD < n_maps
    for a in range(AHEAD):
        s_sc[a] = scores(a, 0)

    def tile(j, masked):
        k0 = pl.multiple_of(j * tk, tk)
        if masked:
            kpos = k0 + lax.broadcasted_iota(jnp.int32, (tk, tq), 0)
            qpos = q0 + lax.broadcasted_iota(jnp.int32, (tk, tq), 1)
            vis = (qpos >= kpos) if fox else ((qpos >> shift) >= (kpos >> shift))
            if kv_len < nk_total * tk:
                vis = vis & (kpos < kv_len)
        for g in range(n_maps):
            sl, c = divmod(g, 2)
            dj, g_ahead = divmod(g + AHEAD, n_maps)
            s_sc[g_ahead] = scores(g_ahead, jnp.minimum(j + dj, nk_total - 1) if dj else j)
            s = s_sc[g]
            if fox:
                ck = ck_ref[0, pl.ds(k0, tk), g * LANES:(g + 1) * LANES]
                s = s - jnp.concatenate([ck] * (tq // LANES), axis=1)
            if masked:
                s = jnp.where(vis, s, NEG_INF)
            m_prev = m_sc[g]
            m_tile = jnp.max(s, axis=0, keepdims=True)
            if fox:
                cq = cq_ref[0, sl, c:c + 1, :]
                m_new = jnp.maximum(m_prev, m_tile + cq)
                p = jnp.exp2(s - (m_new - cq))
            else:
                m_new = jnp.maximum(m_prev, m_tile)
                p = jnp.exp2(s - m_new)
            alpha = jnp.exp2(m_prev - m_new)
            l_sc[g] = alpha * l_sc[g] + jnp.sum(p, axis=0, keepdims=True)
            m_sc[g] = m_new
            pt = p.astype(BF16)
            vt = vt_ref[0, sl, j]
            if fox:
                r = slice(c * HALF, (c + 1) * HALF)
                acc_sc[sl, r, :] = alpha * acc_sc[sl, r, :] + _mm(vt[r, :], pt)
            else:
                acc_sc[g] = alpha * acc_sc[g] + _mm(vt, pt)

    def full_body(j, carry):
        tile(j, False)
        return carry

    def masked_body(j, carry):
        tile(j, True)
        return carry

    lax.fori_loop(0, n_full, full_body, 0)
    lax.fori_loop(n_full, n_need, masked_body, 0)

    for sl in range(spb):
        a, b = 2 * sl, 2 * sl + 1
        if fox:
            ot = jnp.concatenate([acc_sc[sl, 0:HALF, :] / l_sc[a], acc_sc[sl, HALF:LANES, :] / l_sc[b]], axis=0)
            o = ot.T
        else:
            lp = lp_ref[...]
            lam = (jnp.exp(jnp.sum(lp[0:1] * lp[1:2], axis=-1, keepdims=True))
                   - jnp.exp(jnp.sum(lp[2:3] * lp[3:4], axis=-1, keepdims=True)) + lam_init)
            ot = acc_sc[a] / l_sc[a] - lam * (acc_sc[b] / l_sc[b])
            ot = ot * lax.rsqrt(jnp.mean(ot * ot, axis=0, keepdims=True) + EPS)
            o = (ot.T * sg_ref[...]) * (1.0 - lam_init)
        o_ref[0, :, sl * LANES:(sl + 1) * LANES] = o.astype(BF16)


def _attention(q, k, v, extra, *, pos0, kv_len, fox, lam_init=0.0, tq=256, tk=512, spb=4):
    bsz, Tq, w = q.shape
    Tk = k.shape[1]
    ns = w // LANES
    Tq_pad = -(-Tq // LANES) * LANES
    tq = _pick_tile(Tq_pad, tq)
    tk = tk if Tk % tk == 0 else Tk
    nk = Tk // tk
    q = jnp.pad(q, ((0, 0), (0, Tq_pad - Tq), (0, 0)))
    vt = v.reshape(bsz, nk, tk, ns, LANES).transpose(0, 3, 1, 4, 2)
    assert ns % spb == 0
    sw = spb * LANES
    qspec = pl.BlockSpec((1, tq, sw), lambda b, s, i: (b, i, s))
    kspec = pl.BlockSpec((1, Tk, sw), lambda b, s, i: (b, 0, s))
    vspec = pl.BlockSpec((1, spb, nk, LANES, tk), lambda b, s, i: (b, s, 0, 0, 0))
    if fox:
        cq, ck = extra
        extra = (jnp.pad(cq, ((0, 0), (0, 0), (0, 0), (0, Tq_pad - Tq))), ck)
        especs = [pl.BlockSpec((1, spb, 2, tq), lambda b, s, i: (b, s, 0, i)),
                  pl.BlockSpec((1, Tk, 2 * sw), lambda b, s, i: (b, 0, s), pipeline_mode=pl.Buffered(1))]
    else:
        lp, sg = extra
        especs = [pl.BlockSpec(lp.shape, lambda b, s, i: (0, 0)), pl.BlockSpec(sg.shape, lambda b, s, i: (0, 0))]
    out = pl.pallas_call(
        functools.partial(_attn_kernel, tq=tq, tk=tk, nk_total=nk, kv_len=kv_len, pos0=pos0, fox=fox,
                          lam_init=lam_init),
        grid=(bsz, ns // spb, Tq_pad // tq),
        in_specs=[qspec, kspec, vspec] + especs,
        out_specs=qspec,
        out_shape=jax.ShapeDtypeStruct((bsz, Tq_pad, w), BF16),
        scratch_shapes=[pltpu.VMEM((2 * spb, 1, tq), F32), pltpu.VMEM((2 * spb, 1, tq), F32),
                        pltpu.VMEM((spb if fox else 2 * spb, LANES, tq), F32), pltpu.VMEM((2 * spb, tk, tq), F32)],
        compiler_params=_params("parallel", "parallel", "arbitrary"),
        name="fox_attention" if fox else "diff_attention",
    )(q, k, vt, *extra)
    return out[:, :Tq]


def _rope_table_kernel(inv_ref, c_ref, s1_ref, s2_ref, *, pos0):
    T = c_ref.shape[0]
    pos = (pos0 + lax.broadcasted_iota(jnp.int32, (T, LANES), 0)).astype(F32)
    ang = pos * inv_ref[...]
    lane = lax.broadcasted_iota(jnp.int32, (T, LANES), 1) % HALF
    sn = jnp.sin(ang)
    c_ref[...] = jnp.cos(ang)
    s1_ref[...] = jnp.where(lane < ROT_DIM // 2, -sn, 0.0)
    s2_ref[...] = jnp.where(lane >= ROT_DIM // 2, sn, 0.0)


def _rope_tables(T, pos0):
    half = ROT_DIM // 2
    inv = ROPE_THETA ** (-jnp.arange(half, dtype=F32) * (2.0 / ROT_DIM))
    lane = jnp.arange(LANES) % HALF
    inv_lane = jnp.where(lane < ROT_DIM, inv[lane % half], 0.0).reshape(1, LANES).astype(F32)
    tab = jax.ShapeDtypeStruct((T, LANES), F32)
    return pl.pallas_call(
        functools.partial(_rope_table_kernel, pos0=pos0),
        out_shape=[tab, tab, tab],
        name="rope_tables",
    )(inv_lane)


def _inproj_odd_kernel(x_ref, g_ref, w_ref, c_ref, s1_ref, s2_ref, q_ref, k_ref, v_ref, kb_ref, vb_ref, *, w):
    hn = _rms(x_ref[...], g_ref[...]).astype(BF16)
    cs, s1, s2 = c_ref[...], s1_ref[...], s2_ref[...]
    half = ROT_DIM // 2

    def rope(t):
        out = []
        for h in range(w // LANES):
            s = t[:, h * LANES:(h + 1) * LANES]
            out.append(s * cs + pltpu.roll(s, LANES - half, 1) * s1 + pltpu.roll(s, half, 1) * s2)
        return jnp.concatenate(out, axis=1)

    q_ref[...] = (rope(_mm(hn, w_ref[:, 0:w])) * Q_SCALE).astype(BF16)
    k = rope(_mm(hn, w_ref[:, w:2 * w]))
    k_ref[...] = k
    kb_ref[...] = k.astype(BF16)
    v = _mm(hn, w_ref[:, 2 * w:3 * w])
    v_ref[...] = v
    vb_ref[...] = v.astype(BF16)


def _inproj_odd(x2, g, w_in, tabs, tab_blocks, tm):
    n, d = x2.shape
    w = w_in.shape[1] // 3
    row = lambda c: pl.BlockSpec((tm, c), lambda i: (i, 0))
    tab = pl.BlockSpec((tm, LANES), lambda i: (i % tab_blocks, 0))
    f = jax.ShapeDtypeStruct((n, w), F32)
    b = jax.ShapeDtypeStruct((n, w), BF16)
    return pl.pallas_call(
        functools.partial(_inproj_odd_kernel, w=w),
        grid=(n // tm,),
        in_specs=[row(d), _resident((1, d)), _resident((d, 3 * w)), tab, tab, tab],
        out_specs=[row(w)] * 5,
        out_shape=[b, f, f, b, b],
        compiler_params=_params("parallel"),
        name="inproj_odd",
    )(x2, g, w_in, *tabs)


def _outproj_kernel(*refs):
    x_ref, g_ref, w_ref = refs[:3]
    o_ref = refs[-1]
    y = jnp.concatenate([r[...] for r in refs[3:-1]], axis=1)
    o_ref[...] = x_ref[...] + _rms(_mm(y, w_ref[...]), g_ref[...])


def _outproj(x2, g, w_out, ys, tm):
    n, d = x2.shape
    row = lambda c: pl.BlockSpec((tm, c), lambda i: (i, 0))
    return pl.pallas_call(
        _outproj_kernel,
        grid=(n // tm,),
        in_specs=[row(d), _resident((1, d)), _resident(w_out.shape)] + [row(y.shape[1]) for y in ys],
        out_specs=row(d),
        out_shape=jax.ShapeDtypeStruct((n, d), F32),
        compiler_params=_params("parallel"),
        name="outproj",
    )(x2, g, w_out, *ys)


def _trunk(x, pos0, caches, wts, tm):
    bsz, T, d = x.shape
    n = bsz * T
    depth = wts["norm_g"].shape[0]
    w = d // 2
    x2 = x.reshape(n, d)
    tt = _pick_tile(T, 256)
    flat = lambda a: a.reshape(n, a.shape[-1])
    seq = lambda a: a.reshape(bsz, T, a.shape[-1])
    n_keys = T if caches is None else T + caches[0].shape[2]
    key_pad = -n_keys % LANES
    pad_keys = lambda a: jnp.pad(a, ((0, 0), (0, key_pad), (0, 0)))
    rope = None
    even, odd = [], []
    for l in range(depth):
        g = wts["norm_g"][l].reshape(6, 1, d)
        x2 = _ffn(x2, g[0], g[1], wts["w_ffn_in"][l, 0], wts["w_ffn_out"][l, 0], tm)
        if l % 2 == 0:
            e = l // 2
            u, gate, q, k, v, kb, vb, logf = _inproj_even(x2, g[2], wts["w_in_even"][e], wts["b_fox_f"][e], tm)
            if caches is None:
                h0 = jnp.zeros((bsz, 1, w), F32)
                buf0 = jnp.zeros((bsz, 8, w), F32)
                kb_all, vb_all, logf_all = seq(kb), seq(vb), seq(logf)
            else:
                fk, fv, flogf, lru_h, lru_conv = (c[e] for c in caches[:5])
                h0 = lru_h.reshape(bsz, 1, w)
                buf0 = jnp.pad(lru_conv, ((0, 0), (8 - (CONV_W - 1), 0), (0, 0)))
                past = fk.shape[1]
                kb_all = jnp.concatenate([fk.reshape(bsz, past, w).astype(BF16), seq(kb)], axis=1)
                vb_all = jnp.concatenate([fv.reshape(bsz, past, w).astype(BF16), seq(vb)], axis=1)
                logf_all = jnp.concatenate([jnp.repeat(flogf, HALF, axis=-1), seq(logf)], axis=1)
            ya, h_last, nbuf = _lru(seq(u), seq(gate), h0, buf0, wts["lru_conv_w"][e], wts["lru_conv_b"][e],
                                    wts["lru_wr"][e], wts["lru_wi"][e], wts["lru_br"][e], wts["lru_bi"][e],
                                    wts["lru_lambda"][e], tt)
            Tk = logf_all.shape[1]
            ck, cq = _cumsum(pad_keys(logf_all), _pick_tile(Tk + key_pad, 512))
            yb = _attention(seq(q), pad_keys(kb_all), pad_keys(vb_all), (cq[..., Tk - T:Tk], ck),
                            pos0=pos0, kv_len=Tk, fox=True)
            x2 = _outproj(x2, g[3], wts["w_out_even"][e], [flat(ya), flat(yb)], tm)
            even.append((k.reshape(bsz, T, FOX_HEADS, HALF), v.reshape(bsz, T, FOX_HEADS, HALF),
                         seq(logf).reshape(bsz, T, FOX_HEADS, HALF)[..., 0], h_last.reshape(bsz, w), nbuf))
        else:
            o = l // 2
            lam_init = 0.8 - 0.6 * math.exp(-0.3 * l)
            if rope is None:
                tabs = _rope_tables(T, pos0)
                if T % tm:
                    tabs = [jnp.tile(t, (n // T, 1)) for t in tabs]
                rope = (tabs, tabs[0].shape[0] // tm)
            q, k, v, kb, vb = _inproj_odd(x2, g[2], wts["w_in_odd"][o], rope[0], rope[1], tm)
            if caches is None:
                kb_all, vb_all = seq(kb), seq(vb)
            else:
                dk, dv = (c[o] for c in caches[5:])
                past = dk.shape[1]
                kb_all = jnp.concatenate([dk.reshape(bsz, past, d).astype(BF16), seq(kb)], axis=1)
                vb_all = jnp.concatenate([dv.reshape(bsz, past, d).astype(BF16), seq(vb)], axis=1)
            y = _attention(seq(q), pad_keys(kb_all), pad_keys(vb_all),
                           (wts["diff_lambda"][o], wts["diff_subln_g"][o]),
                           pos0=pos0, kv_len=kb_all.shape[1], fox=False, lam_init=lam_init)
            x2 = _outproj(x2, g[3], wts["w_out_odd"][o], [flat(y)], tm)
            odd.append((k.reshape(bsz, T, DIFF_HEADS, LANES), v.reshape(bsz, T, DIFF_HEADS, LANES)))
        x2 = _ffn(x2, g[4], g[5], wts["w_ffn_in"][l, 1], wts["w_ffn_out"][l, 1], tm)
    return (x2.reshape(bsz, T, d), [jnp.stack(z) for z in zip(*even)], [jnp.stack(z) for z in zip(*odd)])


def _prepare_weights(norm_g, w_ffn_in, w_ffn_out, w_in_even, b_fox_f, lru_conv_w, lru_conv_b, lru_w_gates,
                     lru_b_gates, lru_lambda, w_out_even, w_in_odd, diff_lambda, diff_subln_g, w_out_odd):
    n_even, d, _ = w_in_even.shape
    w = d // 2
    blk = w // LRU_BLOCKS
    wf = jnp.repeat(w_in_even[:, :, 5 * w:], HALF, axis=-1)
    eye = jnp.eye(LRU_BLOCKS, dtype=F32)
    dense = lambda g: jnp.einsum("encg,nm->encmg", g, eye).reshape(n_even, w, w)
    return dict(
        norm_g=norm_g, w_ffn_in=w_ffn_in.astype(BF16), w_ffn_out=w_ffn_out.astype(BF16),
        w_in_even=jnp.concatenate([w_in_even[:, :, :5 * w], wf], axis=-1).astype(BF16),
        b_fox_f=jnp.repeat(b_fox_f, HALF, axis=-1).reshape(n_even, 1, w),
        lru_conv_w=lru_conv_w, lru_conv_b=lru_conv_b.reshape(n_even, 1, w),
        lru_wr=dense(lru_w_gates[..., :blk]).astype(BF16), lru_wi=dense(lru_w_gates[..., blk:]).astype(BF16),
        lru_br=lru_b_gates[..., :blk].reshape(n_even, 1, w), lru_bi=lru_b_gates[..., blk:].reshape(n_even, 1, w),
        lru_lambda=lru_lambda.reshape(n_even, 1, w), w_out_even=w_out_even.astype(BF16),
        w_in_odd=w_in_odd.astype(BF16), diff_lambda=diff_lambda, diff_subln_g=diff_subln_g.reshape(-1, 1, LANES),
        w_out_odd=w_out_odd.astype(BF16))


def kernel(x_prompt, x_sample, cache_fox_k, cache_fox_v, cache_fox_logf, state_lru_h, state_lru_conv, cache_diff_k, cache_diff_v, norm_g, w_ffn_in, w_ffn_out, w_in_even, b_fox_f, lru_conv_w, lru_conv_b, lru_w_gates, lru_b_gates, lru_lambda, w_out_even, w_in_odd, diff_lambda, diff_subln_g, w_out_odd):
    wts = _prepare_weights(norm_g, w_ffn_in, w_ffn_out, w_in_even, b_fox_f, lru_conv_w, lru_conv_b, lru_w_gates,
                           lru_b_gates, lru_lambda, w_out_even, w_in_odd, diff_lambda, diff_subln_g, w_out_odd)
    y_p, pe, po = _trunk(x_prompt, 0, None, wts, _pick_tile(x_prompt.shape[0] * x_prompt.shape[1], 512))
    caches = (cache_fox_k, cache_fox_v, cache_fox_logf, state_lru_h, state_lru_conv, cache_diff_k, cache_diff_v)
    y_s, se, so = _trunk(x_sample, cache_fox_k.shape[2], caches, wts,
                         _pick_tile(x_sample.shape[0] * x_sample.shape[1], 256))
    return (y_p, y_s, *pe, *po, *se, *so)
```

```python
import functools
import math

import jax
import jax.numpy as jnp
from jax import lax
from jax.experimental import pallas as pl
from jax.experimental.pallas import tpu as pltpu

F32 = jnp.float32
BF16 = jnp.bfloat16

EPS = 1e-6
NEG_INF = -1e30
LRU_C = 8.0
LRU_BLOCKS = 8
CONV_W = 4
FOX_HEADS = 8
DIFF_HEADS = 8
CHUNK = 64
ROT_DIM = 16
ROPE_THETA = 500000.0

LANES = 128
HALF = LANES // 2
FF_CHUNK = 256
AHEAD = 2
ATTN_TQ, ATTN_TK = 256, 512
CUMSUM_TILE = 1024
LOG2E = 1.0 / math.log(2.0)
Q_SCALE = HALF ** -0.5 * LOG2E
VMEM_LIMIT = 56 * 1024 * 1024


def _params(*sem):
    return pltpu.CompilerParams(dimension_semantics=sem, vmem_limit_bytes=VMEM_LIMIT)


def _resident(shape):
    nd = len(shape)
    return pl.BlockSpec(shape, lambda *_: (0,) * nd, pipeline_mode=pl.Buffered(1))


def _rms(x, g):
    return x * lax.rsqrt(jnp.mean(x * x, axis=-1, keepdims=True) + EPS) * g


def _mm(a, b):
    return jnp.dot(a, b, preferred_element_type=F32)


def _mm_nt(a, b):
    return lax.dot_general(a, b, (((1,), (1,)), ((), ())), preferred_element_type=F32)


def _log_sigmoid(x):
    return jnp.minimum(x, 0.0) - jnp.log1p(jnp.exp(-jnp.abs(x)))


def _softplus(x):
    return jnp.maximum(x, 0.0) + jnp.log1p(jnp.exp(-jnp.abs(x)))


def _pick_tile(n, target):
    t = min(n, target)
    while n % t:
        t //= 2
    return t


def _ffn_kernel(x_ref, gpre_ref, gpost_ref, wi_ref, wo_ref, *rest, dff):
    *mix, o_ref, act_ref = rest
    x = x_ref[...]
    if mix:
        gmix_ref, wmix_ref, *y_refs = mix
        y = jnp.concatenate([r[...] for r in y_refs], axis=1)
        x = x + _rms(_mm(y, wmix_ref[...]), gmix_ref[...])
    xn = _rms(x, gpre_ref[...]).astype(BF16)
    for c0 in range(0, dff, FF_CHUNK):
        g = _mm(xn, wi_ref[:, c0:c0 + FF_CHUNK])
        u = _mm(xn, wi_ref[:, dff + c0:dff + c0 + FF_CHUNK])
        act_ref[:, c0:c0 + FF_CHUNK] = ((g * jax.nn.sigmoid(g)) * u).astype(BF16)
    y = _mm(act_ref[...], wo_ref[...])
    o_ref[...] = x + 0.5 * _rms(y, gpost_ref[...])


def _ffn(x2, g_pre, g_post, w_in, w_out, tm, mix=None):
    n, d = x2.shape
    dff = w_out.shape[0]
    assert dff % FF_CHUNK == 0 and n % tm == 0
    row = lambda c: pl.BlockSpec((tm, c), lambda i: (i, 0))
    mix_in, mix_specs = [], []
    if mix is not None:
        g_mix, w_mix, ys = mix
        mix_in = [g_mix, w_mix, *ys]
        mix_specs = [_resident((1, d)), _resident(w_mix.shape)] + [row(y.shape[1]) for y in ys]
    return pl.pallas_call(
        functools.partial(_ffn_kernel, dff=dff),
        grid=(n // tm,),
        in_specs=[row(d), _resident((1, d)), _resident((1, d)), _resident((d, 2 * dff)), _resident((dff, d))]
        + mix_specs,
        out_specs=row(d),
        out_shape=jax.ShapeDtypeStruct((n, d), F32),
        scratch_shapes=[pltpu.VMEM((tm, dff), BF16)],
        compiler_params=_params("parallel"),
        name="ffn_mix" if mix is not None else "ffn",
    )(x2, g_pre, g_post, w_in, w_out, *mix_in)


def _emit_kv(k, v, k_ref, v_ref, kb_ref, vx_ref, heads, vt_tiles):
    tm, w = k.shape
    k_ref[0] = k.reshape(tm, heads, w // heads)
    v_ref[0] = v.reshape(tm, heads, w // heads)
    kb_ref[...] = k.astype(BF16)
    if vt_tiles:
        vx_ref[0, :, 0] = v.T.astype(BF16).reshape(w // LANES, LANES, tm)
    else:
        vx_ref[...] = v.astype(BF16)


def _kv_out(n, w, heads, tm, layer, stacks, n_layers, vt_tiles_per_seq):
    dh = w // heads
    stacked = pl.BlockSpec((1, tm, heads, dh), lambda i: (layer, i, 0, 0))
    stacked_shape = jax.ShapeDtypeStruct((n_layers, n, heads, dh), F32)
    row = pl.BlockSpec((tm, w), lambda i: (i, 0))
    if vt_tiles_per_seq:
        nkb = vt_tiles_per_seq
        vx = pl.BlockSpec((1, w // LANES, 1, LANES, tm), lambda i: (i // nkb, 0, i % nkb, 0, 0))
        vx_shape = jax.ShapeDtypeStruct((n // (nkb * tm), w // LANES, nkb, LANES, tm), BF16)
    else:
        vx, vx_shape = row, jax.ShapeDtypeStruct((n, w), BF16)
    specs = [stacked, stacked, row, vx]
    shapes = [stacked_shape, stacked_shape, jax.ShapeDtypeStruct((n, w), BF16), vx_shape]
    extra_in = [] if stacks is None else list(stacks)
    return specs, shapes, extra_in


def _inproj_even_kernel(*refs, w, vt_tiles):
    x_ref, g_ref, w_ref, bf_ref = refs[:4]
    u_ref, gate_ref, q_ref, logf_ref, k_ref, v_ref, kb_ref, vx_ref = refs[-8:]
    hn = _rms(x_ref[...], g_ref[...]).astype(BF16)
    u_ref[...] = _mm(hn, w_ref[:, 0:w])
    gate_ref[...] = _mm(hn, w_ref[:, w:2 * w])
    q_ref[...] = (_mm(hn, w_ref[:, 2 * w:3 * w]) * Q_SCALE).astype(BF16)
    logf_ref[...] = _log_sigmoid(_mm(hn, w_ref[:, 5 * w:6 * w]) + bf_ref[...])
    _emit_kv(_mm(hn, w_ref[:, 3 * w:4 * w]), _mm(hn, w_ref[:, 4 * w:5 * w]), k_ref, v_ref, kb_ref, vx_ref,
             FOX_HEADS, vt_tiles)


def _inproj_even(x2, g, w_wide, bf_wide, tm, layer, stacks, n_layers, vt_tiles_per_seq):
    n, d = x2.shape
    w = w_wide.shape[1] // 6
    row = lambda c: pl.BlockSpec((tm, c), lambda i: (i, 0))
    f = jax.ShapeDtypeStruct((n, w), F32)
    b = jax.ShapeDtypeStruct((n, w), BF16)
    kv_specs, kv_shapes, extra_in = _kv_out(n, w, FOX_HEADS, tm, layer, stacks, n_layers, vt_tiles_per_seq)
    return pl.pallas_call(
        functools.partial(_inproj_even_kernel, w=w, vt_tiles=bool(vt_tiles_per_seq)),
        grid=(n // tm,),
        in_specs=[row(d), _resident((1, d)), _resident((d, 6 * w)), _resident((1, w))]
        + [pl.BlockSpec(memory_space=pl.ANY)] * len(extra_in),
        out_specs=[row(w)] * 4 + kv_specs,
        out_shape=[f, f, b, f] + kv_shapes,
        input_output_aliases={4 + a: 4 + a for a in range(len(extra_in))},
        compiler_params=_params("parallel"),
        name="inproj_even",
    )(x2, g, w_wide, bf_wide, *extra_in)


def _shift_rows(x, d, fill):
    rows = lax.broadcasted_iota(jnp.int32, x.shape, 0)
    return jnp.where(rows >= d, pltpu.roll(x, d, 0), fill)


def _lru_kernel(u_ref, gate_ref, h0_ref, buf0_ref, cw_ref, cb_ref, wr_ref, wi_ref, br_ref, bi_ref, lam_ref,
                ya_ref, hlast_ref, nbuf_ref, ubuf, hcar, *, tt):
    t = pl.program_id(1)
    pad = ubuf.shape[0] - tt

    @pl.when(t == 0)
    def _():
        ubuf[0:pad, :] = buf0_ref[0]
        hcar[...] = h0_ref[0]

    u = u_ref[0]
    ubuf[pad:pad + tt, :] = u
    uc = cb_ref[...]
    for j in range(CONV_W):
        o = pad - (CONV_W - 1) + j
        uc = uc + ubuf[o:o + tt, :] * cw_ref[j:j + 1, :]
    ucb = uc.astype(BF16)
    r = jax.nn.sigmoid(_mm(ucb, wr_ref[...]) + br_ref[...])
    ig = jax.nn.sigmoid(_mm(ucb, wi_ref[...]) + bi_ref[...])
    log_a = (-LRU_C * r) * _softplus(-lam_ref[...])
    a = jnp.exp(log_a)
    bx = jnp.sqrt(-jnp.tanh(log_a) * (a * a + 1.0)) * (ig * uc)
    rows = lax.broadcasted_iota(jnp.int32, bx.shape, 0)
    bx = bx + jnp.where(rows == 0, a * hcar[...], 0.0)
    d = 1
    while d < tt:
        bx = a * _shift_rows(bx, d, 0.0) + bx
        a = a * _shift_rows(a, d, 1.0)
        d *= 2
    hcar[...] = bx[tt - 1:tt, :]
    ya_ref[0] = (bx * jax.nn.gelu(gate_ref[0])).astype(BF16)
    ubuf[0:pad, :] = ubuf[tt:tt + pad, :]

    @pl.when(t == pl.num_programs(1) - 1)
    def _():
        hlast_ref[0] = bx[tt - 1:tt, :]
        nbuf_ref[0] = ubuf[pad - (CONV_W - 1):pad, :]


def _lru(u, gate, h0, buf0, cw, cb, wr, wi, br, bi, lam, tt):
    bsz, T, w = u.shape
    pad = 8
    seq = pl.BlockSpec((1, tt, w), lambda b, t: (b, t, 0))
    per_b = lambda r: pl.BlockSpec((1, r, w), lambda b, t: (b, 0, 0))
    return pl.pallas_call(
        functools.partial(_lru_kernel, tt=tt),
        grid=(bsz, T // tt),
        in_specs=[seq, seq, per_b(1), per_b(pad), _resident((CONV_W, w)), _resident((1, w)), _resident((w, w)),
                  _resident((w, w)), _resident((1, w)), _resident((1, w)), _resident((1, w))],
        out_specs=[seq, per_b(1), per_b(CONV_W - 1)],
        out_shape=[jax.ShapeDtypeStruct((bsz, T, w), BF16), jax.ShapeDtypeStruct((bsz, 1, w), F32),
                   jax.ShapeDtypeStruct((bsz, CONV_W - 1, w), F32)],
        scratch_shapes=[pltpu.VMEM((tt + pad, w), F32), pltpu.VMEM((1, w), F32)],
        compiler_params=_params("parallel", "arbitrary"),
        name="conv_rglru",
    )(u, gate, h0, buf0, cw, cb, wr, wi, br, bi, lam)


def _cumsum_kernel(x_ref, ck_ref, cq_ref, car, *, tt):
    @pl.when(pl.program_id(1) == 0)
    def _():
        car[...] = jnp.zeros_like(car)

    x = x_ref[0]
    rows = lax.broadcasted_iota(jnp.int32, x.shape, 0)
    x = x + jnp.where(rows == 0, car[...], 0.0)
    d = 1
    while d < tt:
        x = x + _shift_rows(x, d, 0.0)
        d *= 2
    car[...] = x[tt - 1:tt, :]
    x = x * LOG2E
    lo = lax.broadcasted_iota(jnp.int32, (1, LANES), 1) < HALF
    for s in range(x.shape[1] // LANES):
        xs = x[:, s * LANES:(s + 1) * LANES]
        sw = pltpu.roll(xs, HALF, 1)
        ck_ref[0, :, (2 * s) * LANES:(2 * s + 1) * LANES] = jnp.where(lo, xs, sw)
        ck_ref[0, :, (2 * s + 1) * LANES:(2 * s + 2) * LANES] = jnp.where(lo, sw, xs)
        tr = xs.T
        cq_ref[0, s, 0:1, :] = tr[0:1, :]
        cq_ref[0, s, 1:2, :] = tr[HALF:HALF + 1, :]


def _cumsum(x, tt):
    bsz, T, w = x.shape
    ns = w // LANES
    return pl.pallas_call(
        functools.partial(_cumsum_kernel, tt=tt),
        grid=(bsz, T // tt),
        in_specs=[pl.BlockSpec((1, tt, w), lambda b, t: (b, t, 0))],
        out_specs=[pl.BlockSpec((1, tt, 2 * w), lambda b, t: (b, t, 0)),
                   pl.BlockSpec((1, ns, 2, tt), lambda b, t: (b, 0, 0, t))],
        out_shape=[jax.ShapeDtypeStruct((bsz, T, 2 * w), F32), jax.ShapeDtypeStruct((bsz, ns, 2, T), F32)],
        scratch_shapes=[pltpu.VMEM((1, w), F32)],
        compiler_params=_params("parallel", "arbitrary"),
        name="logf_cumsum",
    )(x)


def _attn_kernel(*refs, tq, tk, nk_total, kv_len, pos0, fox, lam_init):
    if fox:
        q_ref, k_ref, vt_ref, cq_ref, ck_ref, o_ref, m_sc, l_sc, acc_sc, s_sc = refs
    else:
        q_ref, k_ref, vt_ref, lp_ref, sg_ref, o_ref, m_sc, l_sc, acc_sc, s_sc = refs
    i = pl.program_id(2)
    q0 = pos0 + i * tq
    shift = CHUNK.bit_length() - 1
    lo = lax.broadcasted_iota(jnp.int32, (1, LANES), 1) < HALF
    spb = q_ref.shape[2] // LANES
    qm = []
    for sl in range(spb):
        q = q_ref[0, :, sl * LANES:(sl + 1) * LANES]
        qm += [jnp.where(lo, q, jnp.zeros_like(q)), jnp.where(lo, jnp.zeros_like(q), q)]
    if fox:
        n_full = (q0 + 1) // tk
        n_need = (q0 + tq - 1) // tk + 1
    else:
        n_full = (((q0 >> shift) + 1) << shift) // tk
        n_need = (((((q0 + tq - 1) >> shift) + 1) << shift) + tk - 1) // tk
    n_need = jnp.minimum(n_need, nk_total)
    n_full = jnp.minimum(jnp.minimum(n_full, kv_len // tk), n_need)

    m_sc[...] = jnp.full_like(m_sc, NEG_INF)
    l_sc[...] = jnp.zeros_like(l_sc)
    acc_sc[...] = jnp.zeros_like(acc_sc)

    n_maps = 2 * spb

    def scores(g, j):
        k0 = j * tk if isinstance(j, int) else pl.multiple_of(j * tk, tk)
        lanes = slice((g // 2) * LANES, (g // 2 + 1) * LANES)
        hk = tk // 2
        return jnp.concatenate([_mm_nt(k_ref[0, pl.ds(k0 + r, hk), lanes], qm[g]) for r in (0, hk)], axis=0)

    assert AHEAD < n_maps
    for a in range(AHEAD):
        s_sc[a] = scores(a, 0)

    def tile(j, masked):
        k0 = pl.multiple_of(j * tk, tk)
        if masked:
            kpos = k0 + lax.broadcasted_iota(jnp.int32, (tk, tq), 0)
            qpos = q0 + lax.broadcasted_iota(jnp.int32, (tk, tq), 1)
            vis = (qpos >= kpos) if fox else ((qpos >> shift) >= (kpos >> shift))
            if kv_len < nk_total * tk:
                vis = vis & (kpos < kv_len)
        for g in range(n_maps):
            sl, c = divmod(g, 2)
            dj, g_ahead = divmod(g + AHEAD, n_maps)
            s_sc[g_ahead] = scores(g_ahead, jnp.minimum(j + dj, nk_total - 1) if dj else j)
            s = s_sc[g]
            if fox:
                ck = ck_ref[0, pl.ds(k0, tk), g * LANES:(g + 1) * LANES]
                s = s - jnp.concatenate([ck] * (tq // LANES), axis=1)
            if masked:
                s = jnp.where(vis, s, NEG_INF)
            m_prev = m_sc[g]
            m_tile = jnp.max(s, axis=0, keepdims=True)
            if fox:
                cq = cq_ref[0, sl, c:c + 1, :]
                m_new = jnp.maximum(m_prev, m_tile + cq)
                p = jnp.exp2(s - (m_new - cq))
            else:
                m_new = jnp.maximum(m_prev, m_tile)
                p = jnp.exp2(s - m_new)
            alpha = jnp.exp2(m_prev - m_new)
            l_sc[g] = alpha * l_sc[g] + jnp.sum(p, axis=0, keepdims=True)
            m_sc[g] = m_new
            pt = p.astype(BF16)
            vt = vt_ref[0, sl, j]
            if fox:
                r = slice(c * HALF, (c + 1) * HALF)
                acc_sc[sl, r, :] = alpha * acc_sc[sl, r, :] + _mm(vt[r, :], pt)
            else:
                acc_sc[g] = alpha * acc_sc[g] + _mm(vt, pt)

    def full_body(j, carry):
        tile(j, False)
        return carry

    def masked_body(j, carry):
        tile(j, True)
        return carry

    lax.fori_loop(0, n_full, full_body, 0)
    lax.fori_loop(n_full, n_need, masked_body, 0)

    for sl in range(spb):
        a, b = 2 * sl, 2 * sl + 1
        if fox:
            ot = jnp.concatenate([acc_sc[sl, 0:HALF, :] / l_sc[a], acc_sc[sl, HALF:LANES, :] / l_sc[b]], axis=0)
            o = ot.T
        else:
            lp = lp_ref[...]
            lam = (jnp.exp(jnp.sum(lp[0:1] * lp[1:2], axis=-1, keepdims=True))
                   - jnp.exp(jnp.sum(lp[2:3] * lp[3:4], axis=-1, keepdims=True)) + lam_init)
            ot = acc_sc[a] / l_sc[a] - lam * (acc_sc[b] / l_sc[b])
            ot = ot * lax.rsqrt(jnp.mean(ot * ot, axis=0, keepdims=True) + EPS)
            o = (ot.T * sg_ref[...]) * (1.0 - lam_init)
        o_ref[0, :, sl * LANES:(sl + 1) * LANES] = o.astype(BF16)


def _attention(q, k, v, extra, *, pos0, kv_len, fox, lam_init=0.0, tq=ATTN_TQ, tk=ATTN_TK, spb=4):
    bsz, Tq, w = q.shape
    Tk = k.shape[1]
    ns = w // LANES
    Tq_pad = -(-Tq // LANES) * LANES
    tq = _pick_tile(Tq_pad, tq)
    tk = tk if Tk % tk == 0 else Tk
    nk = Tk // tk
    q = jnp.pad(q, ((0, 0), (0, Tq_pad - Tq), (0, 0)))
    vt = v if v.ndim == 5 else v.reshape(bsz, nk, tk, ns, LANES).transpose(0, 3, 1, 4, 2)
    assert vt.shape == (bsz, ns, nk, LANES, tk)
    assert ns % spb == 0
    sw = spb * LANES
    qspec = pl.BlockSpec((1, tq, sw), lambda b, s, i: (b, i, s))
    kspec = pl.BlockSpec((1, Tk, sw), lambda b, s, i: (b, 0, s))
    vspec = pl.BlockSpec((1, spb, nk, LANES, tk), lambda b, s, i: (b, s, 0, 0, 0))
    if fox:
        cq, ck = extra
        extra = (jnp.pad(cq, ((0, 0), (0, 0), (0, 0), (0, Tq_pad - Tq))), ck)
        especs = [pl.BlockSpec((1, spb, 2, tq), lambda b, s, i: (b, s, 0, i)),
                  pl.BlockSpec((1, Tk, 2 * sw), lambda b, s, i: (b, 0, s), pipeline_mode=pl.Buffered(1))]
    else:
        lp, sg = extra
        especs = [pl.BlockSpec(lp.shape, lambda b, s, i: (0, 0)), pl.BlockSpec(sg.shape, lambda b, s, i: (0, 0))]
    out = pl.pallas_call(
        functools.partial(_attn_kernel, tq=tq, tk=tk, nk_total=nk, kv_len=kv_len, pos0=pos0, fox=fox,
                          lam_init=lam_init),
        grid=(bsz, ns // spb, Tq_pad // tq),
        in_specs=[qspec, kspec, vspec] + especs,
        out_specs=qspec,
        out_shape=jax.ShapeDtypeStruct((bsz, Tq_pad, w), BF16),
        scratch_shapes=[pltpu.VMEM((2 * spb, 1, tq), F32), pltpu.VMEM((2 * spb, 1, tq), F32),
                        pltpu.VMEM((spb if fox else 2 * spb, LANES, tq), F32), pltpu.VMEM((2 * spb, tk, tq), F32)],
        compiler_params=_params("parallel", "parallel", "arbitrary"),
        name="fox_attention" if fox else "diff_attention",
    )(q, k, vt, *extra)
    return out[:, :Tq]


def _rope_table_kernel(inv_ref, c_ref, s1_ref, s2_ref, *, pos0):
    T = c_ref.shape[0]
    pos = (pos0 + lax.broadcasted_iota(jnp.int32, (T, LANES), 0)).astype(F32)
    ang = pos * inv_ref[...]
    lane = lax.broadcasted_iota(jnp.int32, (T, LANES), 1) % HALF
    sn = jnp.sin(ang)
    c_ref[...] = jnp.cos(ang)
    s1_ref[...] = jnp.where(lane < ROT_DIM // 2, -sn, 0.0)
    s2_ref[...] = jnp.where(lane >= ROT_DIM // 2, sn, 0.0)


def _rope_tables(T, pos0):
    half = ROT_DIM // 2
    inv = ROPE_THETA ** (-jnp.arange(half, dtype=F32) * (2.0 / ROT_DIM))
    lane = jnp.arange(LANES) % HALF
    inv_lane = jnp.where(lane < ROT_DIM, inv[lane % half], 0.0).reshape(1, LANES).astype(F32)
    tab = jax.ShapeDtypeStruct((T, LANES), F32)
    return pl.pallas_call(
        functools.partial(_rope_table_kernel, pos0=pos0),
        out_shape=[tab, tab, tab],
        name="rope_tables",
    )(inv_lane)


def _inproj_odd_kernel(*refs, w, vt_tiles):
    x_ref, g_ref, w_ref, c_ref, s1_ref, s2_ref = refs[:6]
    q_ref, k_ref, v_ref, kb_ref, vx_ref = refs[-5:]
    hn = _rms(x_ref[...], g_ref[...]).astype(BF16)
    cs, s1, s2 = c_ref[...], s1_ref[...], s2_ref[...]
    half = ROT_DIM // 2

    def rope(t):
        out = []
        for h in range(w // LANES):
            s = t[:, h * LANES:(h + 1) * LANES]
            out.append(s * cs + pltpu.roll(s, LANES - half, 1) * s1 + pltpu.roll(s, half, 1) * s2)
        return jnp.concatenate(out, axis=1)

    q_ref[...] = (rope(_mm(hn, w_ref[:, 0:w])) * Q_SCALE).astype(BF16)
    _emit_kv(rope(_mm(hn, w_ref[:, w:2 * w])), _mm(hn, w_ref[:, 2 * w:3 * w]), k_ref, v_ref, kb_ref, vx_ref,
             DIFF_HEADS, vt_tiles)


def _inproj_odd(x2, g, w_in, tabs, tab_blocks, tm, layer, stacks, n_layers, vt_tiles_per_seq):
    n, d = x2.shape
    w = w_in.shape[1] // 3
    row = lambda c: pl.BlockSpec((tm, c), lambda i: (i, 0))
    tab = pl.BlockSpec((tm, LANES), lambda i: (i % tab_blocks, 0))
    kv_specs, kv_shapes, extra_in = _kv_out(n, w, DIFF_HEADS, tm, layer, stacks, n_layers, vt_tiles_per_seq)
    return pl.pallas_call(
        functools.partial(_inproj_odd_kernel, w=w, vt_tiles=bool(vt_tiles_per_seq)),
        grid=(n // tm,),
        in_specs=[row(d), _resident((1, d)), _resident((d, 3 * w)), tab, tab, tab]
        + [pl.BlockSpec(memory_space=pl.ANY)] * len(extra_in),
        out_specs=[row(w)] + kv_specs,
        out_shape=[jax.ShapeDtypeStruct((n, w), BF16)] + kv_shapes,
        input_output_aliases={6 + a: 1 + a for a in range(len(extra_in))},
        compiler_params=_params("parallel"),
        name="inproj_odd",
    )(x2, g, w_in, *tabs, *extra_in)


def _trunk(x, pos0, caches, wts, tm):
    bsz, T, d = x.shape
    n = bsz * T
    depth = wts["norm_g"].shape[0]
    w = d // 2
    x2 = x.reshape(n, d)
    tt = _pick_tile(T, 256)
    flat = lambda a: a.reshape(n, a.shape[-1])
    seq = lambda a: a.reshape(bsz, T, a.shape[-1])
    n_keys = T if caches is None else T + caches[0].shape[2]
    key_pad = -n_keys % LANES
    pad_keys = lambda a: jnp.pad(a, ((0, 0), (0, key_pad), (0, 0))) if key_pad else a
    rope = None
    n_even, n_odd = (depth + 1) // 2, depth // 2
    vt_tiles_per_seq = T // tm if caches is None and tm == ATTN_TK and T % tm == 0 else 0
    fox_kv = diff_kv = None
    even, odd = [], []
    for l in range(depth):
        g = wts["norm_g"][l].reshape(6, 1, d)
        x2 = _ffn(x2, g[0], g[1], wts["w_ffn_in"][l, 0], wts["w_ffn_out"][l, 0], tm)
        if l % 2 == 0:
            e = l // 2
            u, gate, q, logf, *fox_kv, kb, vx = _inproj_even(
                x2, g[2], wts["w_in_even"][e], wts["b_fox_f"][e], tm, e, fox_kv, n_even, vt_tiles_per_seq)
            if caches is None:
                h0 = jnp.zeros((bsz, 1, w), F32)
                buf0 = jnp.zeros((bsz, 8, w), F32)
                kb_all, logf_all = seq(kb), seq(logf)
                vb_all = vx if vt_tiles_per_seq else seq(vx)
            else:
                fk, fv, flogf, lru_h, lru_conv = (c[e] for c in caches[:5])
                h0 = lru_h.reshape(bsz, 1, w)
                buf0 = jnp.pad(lru_conv, ((0, 0), (8 - (CONV_W - 1), 0), (0, 0)))
                past = fk.shape[1]
                kb_all = jnp.concatenate([fk.reshape(bsz, past, w).astype(BF16), seq(kb)], axis=1)
                vb_all = jnp.concatenate([fv.reshape(bsz, past, w).astype(BF16), seq(vx)], axis=1)
                logf_all = jnp.concatenate([jnp.repeat(flogf, HALF, axis=-1), seq(logf)], axis=1)
            ya, h_last, nbuf = _lru(seq(u), seq(gate), h0, buf0, wts["lru_conv_w"][e], wts["lru_conv_b"][e],
                                    wts["lru_wr"][e], wts["lru_wi"][e], wts["lru_br"][e], wts["lru_bi"][e],
                                    wts["lru_lambda"][e], tt)
            Tk = logf_all.shape[1]
            Tp = Tk + key_pad
            ck, cq = _cumsum(pad_keys(logf_all), Tp if Tp <= 2 * CUMSUM_TILE else _pick_tile(Tp, CUMSUM_TILE))
            yb = _attention(seq(q), pad_keys(kb_all), pad_keys(vb_all), (cq[..., Tk - T:Tk], ck),
                            pos0=pos0, kv_len=Tk, fox=True)
            mix = (g[3], wts["w_out_even"][e], [flat(ya), flat(yb)])
            even.append((seq(logf).reshape(bsz, T, FOX_HEADS, HALF)[..., 0], h_last.reshape(bsz, w), nbuf))
        else:
            o = l // 2
            lam_init = 0.8 - 0.6 * math.exp(-0.3 * l)
            if rope is None:
                tabs = _rope_tables(T, pos0)
                if T % tm:
                    tabs = [jnp.tile(t, (n // T, 1)) for t in tabs]
                rope = (tabs, tabs[0].shape[0] // tm)
            q, *diff_kv, kb, vx = _inproj_odd(x2, g[2], wts["w_in_odd"][o], rope[0], rope[1], tm, o, diff_kv,
                                              n_odd, vt_tiles_per_seq)
            if caches is None:
                kb_all = seq(kb)
                vb_all = vx if vt_tiles_per_seq else seq(vx)
            else:
                dk, dv = (c[o] for c in caches[5:])
                past = dk.shape[1]
                kb_all = jnp.concatenate([dk.reshape(bsz, past, d).astype(BF16), seq(kb)], axis=1)
                vb_all = jnp.concatenate([dv.reshape(bsz, past, d).astype(BF16), seq(vx)], axis=1)
            y = _attention(seq(q), pad_keys(kb_all), pad_keys(vb_all),
                           (wts["diff_lambda"][o], wts["diff_subln_g"][o]),
                           pos0=pos0, kv_len=kb_all.shape[1], fox=False, lam_init=lam_init)
            mix = (g[3], wts["w_out_odd"][o], [flat(y)])
        x2 = _ffn(x2, g[4], g[5], wts["w_ffn_in"][l, 1], wts["w_ffn_out"][l, 1], tm, mix)
    per_seq = lambda a: a.reshape(a.shape[0], bsz, T, *a.shape[2:])
    even_out = [per_seq(a) for a in fox_kv] + [jnp.stack(z) for z in zip(*even)]
    return x2.reshape(bsz, T, d), even_out, [per_seq(a) for a in diff_kv]


def _prepare_weights(norm_g, w_ffn_in, w_ffn_out, w_in_even, b_fox_f, lru_conv_w, lru_conv_b, lru_w_gates,
                     lru_b_gates, lru_lambda, w_out_even, w_in_odd, diff_lambda, diff_subln_g, w_out_odd):
    n_even, d, _ = w_in_even.shape
    w = d // 2
    blk = w // LRU_BLOCKS
    wf = jnp.repeat(w_in_even[:, :, 5 * w:], HALF, axis=-1)
    eye = jnp.eye(LRU_BLOCKS, dtype=F32)
    dense = lambda g: jnp.einsum("encg,nm->encmg", g, eye).reshape(n_even, w, w)
    return dict(
        norm_g=norm_g, w_ffn_in=w_ffn_in.astype(BF16), w_ffn_out=w_ffn_out.astype(BF16),
        w_in_even=jnp.concatenate([w_in_even[:, :, :5 * w], wf], axis=-1).astype(BF16),
        b_fox_f=jnp.repeat(b_fox_f, HALF, axis=-1).reshape(n_even, 1, w),
        lru_conv_w=lru_conv_w, lru_conv_b=lru_conv_b.reshape(n_even, 1, w),
        lru_wr=dense(lru_w_gates[..., :blk]).astype(BF16), lru_wi=dense(lru_w_gates[..., blk:]).astype(BF16),
        lru_br=lru_b_gates[..., :blk].reshape(n_even, 1, w), lru_bi=lru_b_gates[..., blk:].reshape(n_even, 1, w),
        lru_lambda=lru_lambda.reshape(n_even, 1, w), w_out_even=w_out_even.astype(BF16),
        w_in_odd=w_in_odd.astype(BF16), diff_lambda=diff_lambda, diff_subln_g=diff_subln_g.reshape(-1, 1, LANES),
        w_out_odd=w_out_odd.astype(BF16))


def kernel(x_prompt, x_sample, cache_fox_k, cache_fox_v, cache_fox_logf, state_lru_h, state_lru_conv, cache_diff_k, cache_diff_v, norm_g, w_ffn_in, w_ffn_out, w_in_even, b_fox_f, lru_conv_w, lru_conv_b, lru_w_gates, lru_b_gates, lru_lambda, w_out_even, w_in_odd, diff_lambda, diff_subln_g, w_out_odd):
    wts = _prepare_weights(norm_g, w_ffn_in, w_ffn_out, w_in_even, b_fox_f, lru_conv_w, lru_conv_b, lru_w_gates,
                           lru_b_gates, lru_lambda, w_out_even, w_in_odd, diff_lambda, diff_subln_g, w_out_odd)
    y_p, pe, po = _trunk(x_prompt, 0, None, wts, _pick_tile(x_prompt.shape[0] * x_prompt.shape[1], 512))
    caches = (cache_fox_k, cache_fox_v, cache_fox_logf, state_lru_h, state_lru_conv, cache_diff_k, cache_diff_v)
    y_s, se, so = _trunk(x_sample, cache_fox_k.shape[2], caches, wts,
                         _pick_tile(x_sample.shape[0] * x_sample.shape[1], 256))
    return (y_p, y_s, *pe, *po, *se, *so)
```

```python
import functools
import math

import jax
import jax.numpy as jnp
from jax import lax
from jax.experimental import pallas as pl
from jax.experimental.pallas import tpu as pltpu

F32 = jnp.float32
BF16 = jnp.bfloat16

EPS = 1e-6
NEG_INF = -1e30
LRU_C = 8.0
LRU_BLOCKS = 8
CONV_W = 4
FOX_HEADS = 8
DIFF_HEADS = 8
CHUNK = 64
ROT_DIM = 16
ROPE_THETA = 500000.0

LANES = 128
SUBLANES = 8
HALF = LANES // 2
FF_CHUNK = 256
AHEAD = 2
ATTN_TQ, ATTN_TK = 512, 512
CUMSUM_TILE = 1024
LOG2E = 1.0 / math.log(2.0)
Q_SCALE = HALF ** -0.5 * LOG2E
VMEM_LIMIT = 56 * 1024 * 1024


def _params(*sem):
    return pltpu.CompilerParams(dimension_semantics=sem, vmem_limit_bytes=VMEM_LIMIT)


def _resident(shape, lead=()):
    nd = len(shape)
    return pl.BlockSpec((None,) * len(lead) + tuple(shape), lambda *_: tuple(lead) + (0,) * nd,
                        pipeline_mode=pl.Buffered(1))


def _rms(x, g):
    return x * lax.rsqrt(jnp.mean(x * x, axis=-1, keepdims=True) + EPS) * g


def _mm(a, b):
    return jnp.dot(a, b, preferred_element_type=F32)


def _mm_nt(a, b):
    return lax.dot_general(a, b, (((1,), (1,)), ((), ())), preferred_element_type=F32)


def _log_sigmoid(x):
    return jnp.minimum(x, 0.0) - jnp.log1p(jnp.exp(-jnp.abs(x)))


def _softplus(x):
    return jnp.maximum(x, 0.0) + jnp.log1p(jnp.exp(-jnp.abs(x)))


def _pick_tile(n, target):
    t = min(n, target)
    while n % t:
        t //= 2
    return t


def _ffn_kernel(x_ref, gpre_ref, gpost_ref, wi_ref, wo_ref, *rest, dff):
    *mix, o_ref, act_ref = rest
    x = x_ref[...]
    if mix:
        gmix_ref, wmix_ref, *y_refs = mix
        y = jnp.concatenate([r[...] for r in y_refs], axis=1)
        x = x + _rms(_mm(y, wmix_ref[...]), gmix_ref[...])
    xn = _rms(x, gpre_ref[...]).astype(BF16)
    for c0 in range(0, dff, FF_CHUNK):
        g = _mm(xn, wi_ref[:, c0:c0 + FF_CHUNK])
        u = _mm(xn, wi_ref[:, dff + c0:dff + c0 + FF_CHUNK])
        act_ref[:, c0:c0 + FF_CHUNK] = ((g * jax.nn.sigmoid(g)) * u).astype(BF16)
    y = _mm(act_ref[...], wo_ref[...])
    o_ref[...] = x + 0.5 * _rms(y, gpost_ref[...])


def _ffn(x2, g_pre, g_post, w_in, w_out, which, tm, mix=None):
    n, d = x2.shape
    dff = w_out.shape[-2]
    assert dff % FF_CHUNK == 0 and n % tm == 0
    row = lambda c: pl.BlockSpec((tm, c), lambda i: (i, 0))
    mix_in, mix_specs = [], []
    if mix is not None:
        g_mix, w_mix, ys = mix
        mix_in = [g_mix, w_mix, *ys]
        mix_specs = [_resident((1, d)), _resident(w_mix.shape)] + [row(y.shape[1]) for y in ys]
    return pl.pallas_call(
        functools.partial(_ffn_kernel, dff=dff),
        grid=(n // tm,),
        in_specs=[row(d), _resident((1, d)), _resident((1, d)), _resident((d, 2 * dff), which),
                  _resident((dff, d), which)] + mix_specs,
        out_specs=row(d),
        out_shape=jax.ShapeDtypeStruct((n, d), F32),
        scratch_shapes=[pltpu.VMEM((tm, dff), BF16)],
        compiler_params=_params("parallel"),
        name="ffn_mix" if mix is not None else "ffn",
    )(x2, g_pre, g_post, w_in, w_out, *mix_in)


def _emit_kv(k, v, k_ref, v_ref, kb_ref, vx_ref, heads, vt_tiles):
    tm, w = k.shape
    k_ref[0] = k.reshape(tm, heads, w // heads)
    v_ref[0] = v.reshape(tm, heads, w // heads)
    kb_ref[...] = k.astype(BF16)
    if vt_tiles:
        vx_ref[0, :, 0] = v.T.astype(BF16).reshape(w // LANES, LANES, tm)
    else:
        vx_ref[...] = v.astype(BF16)


def _kv_out(n, w, heads, tm, layer, stacks, n_layers, vt_tiles_per_seq):
    dh = w // heads
    stacked = pl.BlockSpec((1, tm, heads, dh), lambda i: (layer, i, 0, 0))
    stacked_shape = jax.ShapeDtypeStruct((n_layers, n, heads, dh), F32)
    row = pl.BlockSpec((tm, w), lambda i: (i, 0))
    if vt_tiles_per_seq:
        nkb = vt_tiles_per_seq
        vx = pl.BlockSpec((1, w // LANES, 1, LANES, tm), lambda i: (i // nkb, 0, i % nkb, 0, 0))
        vx_shape = jax.ShapeDtypeStruct((n // (nkb * tm), w // LANES, nkb, LANES, tm), BF16)
    else:
        vx, vx_shape = row, jax.ShapeDtypeStruct((n, w), BF16)
    specs = [stacked, stacked, row, vx]
    shapes = [stacked_shape, stacked_shape, jax.ShapeDtypeStruct((n, w), BF16), vx_shape]
    extra_in = [] if stacks is None else list(stacks)
    return specs, shapes, extra_in


def _inproj_even_kernel(*refs, w, vt_tiles):
    x_ref, g_ref, w_ref, bf_ref = refs[:4]
    u_ref, gate_ref, q_ref, logf_ref, logf1_ref, k_ref, v_ref, kb_ref, vx_ref = refs[-9:]
    hn = _rms(x_ref[...], g_ref[...]).astype(BF16)
    u_ref[...] = _mm(hn, w_ref[:, 0:w])
    gate_ref[...] = _mm(hn, w_ref[:, w:2 * w])
    q_ref[...] = (_mm(hn, w_ref[:, 2 * w:3 * w]) * Q_SCALE).astype(BF16)
    logf_ref[...] = _log_sigmoid(_mm(hn, w_ref[:, 5 * w:6 * w]) + bf_ref[:, 0:w])
    logf1_ref[...] = _log_sigmoid(_mm(hn, w_ref[:, 6 * w:6 * w + LANES]) + bf_ref[:, w:w + LANES])
    _emit_kv(_mm(hn, w_ref[:, 3 * w:4 * w]), _mm(hn, w_ref[:, 4 * w:5 * w]), k_ref, v_ref, kb_ref, vx_ref,
             FOX_HEADS, vt_tiles)


def _inproj_even(x2, g, w_wide, bf_wide, tm, layer, stacks, n_layers, vt_tiles_per_seq):
    n, d = x2.shape
    w = (w_wide.shape[1] - LANES) // 6
    row = lambda c: pl.BlockSpec((tm, c), lambda i: (i, 0))
    f = jax.ShapeDtypeStruct((n, w), F32)
    b = jax.ShapeDtypeStruct((n, w), BF16)
    kv_specs, kv_shapes, extra_in = _kv_out(n, w, FOX_HEADS, tm, layer, stacks, n_layers, vt_tiles_per_seq)
    return pl.pallas_call(
        functools.partial(_inproj_even_kernel, w=w, vt_tiles=bool(vt_tiles_per_seq)),
        grid=(n // tm,),
        in_specs=[row(d), _resident((1, d)), _resident(w_wide.shape), _resident(bf_wide.shape)]
        + [pl.BlockSpec(memory_space=pl.ANY)] * len(extra_in),
        out_specs=[row(w)] * 4 + [row(LANES)] + kv_specs,
        out_shape=[f, f, b, f, jax.ShapeDtypeStruct((n, LANES), F32)] + kv_shapes,
        input_output_aliases={4 + a: 5 + a for a in range(len(extra_in))},
        compiler_params=_params("parallel"),
        name="inproj_even",
    )(x2, g, w_wide, bf_wide, *extra_in)


def _scan_rows(a, b, h):
    tt, c = b.shape
    assert tt % SUBLANES == 0
    groups = (tt // SUBLANES, SUBLANES, c)
    b = b.reshape(groups)
    a = None if a is None else a.reshape(groups)
    sub = lax.broadcasted_iota(jnp.int32, groups, 1)
    d = 1
    while d < SUBLANES:
        keep = sub >= d
        if a is None:
            b = b + jnp.where(keep, pltpu.roll(b, d, 1), 0.0)
        else:
            b = a * jnp.where(keep, pltpu.roll(b, d, 1), 0.0) + b
            a = a * jnp.where(keep, pltpu.roll(a, d, 1), 1.0)
        d *= 2
    out = []
    for g in range(groups[0]):
        blk = b[g] + (h if a is None else a[g] * h)
        out.append(blk)
        h = blk[SUBLANES - 1:SUBLANES]
    return jnp.concatenate(out, axis=0)


def _lru_kernel(u_ref, gate_ref, h0_ref, buf0_ref, cw_ref, cb_ref, wr_ref, wi_ref, br_ref, bi_ref, lam_ref,
                ya_ref, hlast_ref, nbuf_ref, ubuf, hcar, *, tt):
    t = pl.program_id(1)
    pad = ubuf.shape[0] - tt

    @pl.when(t == 0)
    def _():
        ubuf[0:pad, :] = buf0_ref[0]
        hcar[...] = h0_ref[0]

    u = u_ref[0]
    ubuf[pad:pad + tt, :] = u
    uc = cb_ref[...]
    for j in range(CONV_W):
        o = pad - (CONV_W - 1) + j
        uc = uc + ubuf[o:o + tt, :] * cw_ref[j:j + 1, :]
    ucb = uc.astype(BF16)
    r = jax.nn.sigmoid(_mm(ucb, wr_ref[...]) + br_ref[...])
    ig = jax.nn.sigmoid(_mm(ucb, wi_ref[...]) + bi_ref[...])
    log_a = (-LRU_C * r) * _softplus(-lam_ref[...])
    a = jnp.exp(log_a)
    bx = jnp.sqrt(-jnp.tanh(log_a) * (a * a + 1.0)) * (ig * uc)
    bx = _scan_rows(a, bx, hcar[...])
    hcar[...] = bx[tt - 1:tt, :]
    ya_ref[0] = (bx * jax.nn.gelu(gate_ref[0])).astype(BF16)
    ubuf[0:pad, :] = ubuf[tt:tt + pad, :]

    @pl.when(t == pl.num_programs(1) - 1)
    def _():
        hlast_ref[0] = bx[tt - 1:tt, :]
        nbuf_ref[0] = ubuf[pad - (CONV_W - 1):pad, :]


def _lru(u, gate, h0, buf0, cw, cb, wr, wi, br, bi, lam, tt):
    bsz, T, w = u.shape
    pad = 8
    seq = pl.BlockSpec((1, tt, w), lambda b, t: (b, t, 0))
    per_b = lambda r: pl.BlockSpec((1, r, w), lambda b, t: (b, 0, 0))
    return pl.pallas_call(
        functools.partial(_lru_kernel, tt=tt),
        grid=(bsz, T // tt),
        in_specs=[seq, seq, per_b(1), per_b(pad), _resident((CONV_W, w)), _resident((1, w)), _resident((w, w)),
                  _resident((w, w)), _resident((1, w)), _resident((1, w)), _resident((1, w))],
        out_specs=[seq, per_b(1), per_b(CONV_W - 1)],
        out_shape=[jax.ShapeDtypeStruct((bsz, T, w), BF16), jax.ShapeDtypeStruct((bsz, 1, w), F32),
                   jax.ShapeDtypeStruct((bsz, CONV_W - 1, w), F32)],
        scratch_shapes=[pltpu.VMEM((tt + pad, w), F32), pltpu.VMEM((1, w), F32)],
        compiler_params=_params("parallel", "arbitrary"),
        name="conv_rglru",
    )(u, gate, h0, buf0, cw, cb, wr, wi, br, bi, lam)


def _cumsum_kernel(x_ref, ck_ref, cq_ref, car, *, tt):
    @pl.when(pl.program_id(1) == 0)
    def _():
        car[...] = jnp.zeros_like(car)

    x = _scan_rows(None, x_ref[0], car[...])
    car[...] = x[tt - 1:tt, :]
    x = x * LOG2E
    lo = lax.broadcasted_iota(jnp.int32, (1, LANES), 1) < HALF
    for s in range(x.shape[1] // LANES):
        xs = x[:, s * LANES:(s + 1) * LANES]
        sw = pltpu.roll(xs, HALF, 1)
        ck_ref[0, :, (2 * s) * LANES:(2 * s + 1) * LANES] = jnp.where(lo, xs, sw)
        ck_ref[0, :, (2 * s + 1) * LANES:(2 * s + 2) * LANES] = jnp.where(lo, sw, xs)
        tr = xs.T
        cq_ref[0, s, 0:1, :] = tr[0:1, :]
        cq_ref[0, s, 1:2, :] = tr[HALF:HALF + 1, :]


def _cumsum(x, tt):
    bsz, T, w = x.shape
    ns = w // LANES
    return pl.pallas_call(
        functools.partial(_cumsum_kernel, tt=tt),
        grid=(bsz, T // tt),
        in_specs=[pl.BlockSpec((1, tt, w), lambda b, t: (b, t, 0))],
        out_specs=[pl.BlockSpec((1, tt, 2 * w), lambda b, t: (b, t, 0)),
                   pl.BlockSpec((1, ns, 2, tt), lambda b, t: (b, 0, 0, t))],
        out_shape=[jax.ShapeDtypeStruct((bsz, T, 2 * w), F32), jax.ShapeDtypeStruct((bsz, ns, 2, T), F32)],
        scratch_shapes=[pltpu.VMEM((1, w), F32)],
        compiler_params=_params("parallel", "arbitrary"),
        name="logf_cumsum",
    )(x)


def _attn_kernel(*refs, tq, tk, nk_total, kv_len, pos0, fox, lam_init):
    if fox:
        q_ref, k_ref, vt_ref, cq_ref, ck_ref, o_ref, m_sc, l_sc, acc_sc, s_sc = refs
    else:
        q_ref, k_ref, vt_ref, lp_ref, sg_ref, o_ref, m_sc, l_sc, acc_sc, s_sc = refs
    i = pl.program_id(2)
    q0 = pos0 + i * tq
    shift = CHUNK.bit_length() - 1
    lo = lax.broadcasted_iota(jnp.int32, (1, LANES), 1) < HALF
    spb = q_ref.shape[2] // LANES
    qm = []
    for sl in range(spb):
        q = q_ref[0, :, sl * LANES:(sl + 1) * LANES]
        qm += [jnp.where(lo, q, jnp.zeros_like(q)), jnp.where(lo, jnp.zeros_like(q), q)]
    if fox:
        n_full = (q0 + 1) // tk
        n_need = (q0 + tq - 1) // tk + 1
    else:
        n_full = (((q0 >> shift) + 1) << shift) // tk
        n_need = (((((q0 + tq - 1) >> shift) + 1) << shift) + tk - 1) // tk
    n_need = jnp.minimum(n_need, nk_total)
    n_full = jnp.minimum(jnp.minimum(n_full, kv_len // tk), n_need)

    m_sc[...] = jnp.full_like(m_sc, NEG_INF)
    l_sc[...] = jnp.zeros_like(l_sc)
    acc_sc[...] = jnp.zeros_like(acc_sc)

    n_maps = 2 * spb

    def scores(g, j):
        k0 = j * tk if isinstance(j, int) else pl.multiple_of(j * tk, tk)
        lanes = slice((g // 2) * LANES, (g // 2 + 1) * LANES)
        hk = tk // 2
        return jnp.concatenate([_mm_nt(k_ref[0, pl.ds(k0 + r, hk), lanes], qm[g]) for r in (0, hk)], axis=0)

    assert AHEAD < n_maps
    for a in range(AHEAD):
        s_sc[a] = scores(a, 0)

    def tile(j, masked):
        k0 = pl.multiple_of(j * tk, tk)
        if masked:
            kpos = k0 + lax.broadcasted_iota(jnp.int32, (tk, tq), 0)
            qpos = q0 + lax.broadcasted_iota(jnp.int32, (tk, tq), 1)
            vis = (qpos >= kpos) if fox else ((qpos >> shift) >= (kpos >> shift))
            if kv_len < nk_total * tk:
                vis = vis & (kpos < kv_len)
        for g in range(n_maps):
            sl, c = divmod(g, 2)
            dj, g_ahead = divmod(g + AHEAD, n_maps)
            s_sc[g_ahead] = scores(g_ahead, jnp.minimum(j + dj, nk_total - 1) if dj else j)
            s = s_sc[g]
            if fox:
                ck = ck_ref[0, pl.ds(k0, tk), g * LANES:(g + 1) * LANES]
                s = s - jnp.concatenate([ck] * (tq // LANES), axis=1)
            if masked:
                s = jnp.where(vis, s, NEG_INF)
            m_prev = m_sc[g]
            m_tile = jnp.max(s, axis=0, keepdims=True)
            if fox:
                cq = cq_ref[0, sl, c:c + 1, :]
                m_new = jnp.maximum(m_prev, m_tile + cq)
                p = jnp.exp2(s - (m_new - cq))
            else:
                m_new = jnp.maximum(m_prev, m_tile)
                p = jnp.exp2(s - m_new)
            alpha = jnp.exp2(m_prev - m_new)
            l_sc[g] = alpha * l_sc[g] + jnp.sum(p, axis=0, keepdims=True)
            m_sc[g] = m_new
            pt = p.astype(BF16)
            vt = vt_ref[0, sl, j]
            if fox:
                r = slice(c * HALF, (c + 1) * HALF)
                acc_sc[sl, r, :] = alpha * acc_sc[sl, r, :] + _mm(vt[r, :], pt)
            else:
                acc_sc[g] = alpha * acc_sc[g] + _mm(vt, pt)

    def full_body(j, carry):
        tile(j, False)
        return carry

    def masked_body(j, carry):
        tile(j, True)
        return carry

    lax.fori_loop(0, n_full, full_body, 0)
    lax.fori_loop(n_full, n_need, masked_body, 0)

    for sl in range(spb):
        a, b = 2 * sl, 2 * sl + 1
        if fox:
            ot = jnp.concatenate([acc_sc[sl, 0:HALF, :] / l_sc[a], acc_sc[sl, HALF:LANES, :] / l_sc[b]], axis=0)
            o = ot.T
        else:
            lp = lp_ref[...]
            lam = (jnp.exp(jnp.sum(lp[0:1] * lp[1:2], axis=-1, keepdims=True))
                   - jnp.exp(jnp.sum(lp[2:3] * lp[3:4], axis=-1, keepdims=True)) + lam_init)
            ot = acc_sc[a] / l_sc[a] - lam * (acc_sc[b] / l_sc[b])
            ot = ot * lax.rsqrt(jnp.mean(ot * ot, axis=0, keepdims=True) + EPS)
            o = (ot.T * sg_ref[...]) * (1.0 - lam_init)
        o_ref[0, :, sl * LANES:(sl + 1) * LANES] = o.astype(BF16)


def _attention(q, k, v, extra, *, pos0, kv_len, fox, lam_init=0.0, tq=ATTN_TQ, tk=ATTN_TK, spb=4):
    bsz, Tq, w = q.shape
    Tk = k.shape[1]
    ns = w // LANES
    Tq_pad = -(-Tq // LANES) * LANES
    tq = _pick_tile(Tq_pad, tq)
    tk = tk if Tk % tk == 0 else Tk
    nk = Tk // tk
    q = jnp.pad(q, ((0, 0), (0, Tq_pad - Tq), (0, 0)))
    vt = v if v.ndim == 5 else v.reshape(bsz, nk, tk, ns, LANES).transpose(0, 3, 1, 4, 2)
    assert vt.shape == (bsz, ns, nk, LANES, tk)
    assert ns % spb == 0
    sw = spb * LANES
    qspec = pl.BlockSpec((1, tq, sw), lambda b, s, i: (b, i, s))
    kspec = pl.BlockSpec((1, Tk, sw), lambda b, s, i: (b, 0, s))
    vspec = pl.BlockSpec((1, spb, nk, LANES, tk), lambda b, s, i: (b, s, 0, 0, 0))
    if fox:
        cq, ck = extra
        extra = (jnp.pad(cq, ((0, 0), (0, 0), (0, 0), (0, Tq_pad - Tq))), ck)
        especs = [pl.BlockSpec((1, spb, 2, tq), lambda b, s, i: (b, s, 0, i)),
                  pl.BlockSpec((1, Tk, 2 * sw), lambda b, s, i: (b, 0, s), pipeline_mode=pl.Buffered(1))]
    else:
        lp, sg = extra
        especs = [pl.BlockSpec(lp.shape, lambda b, s, i: (0, 0)), pl.BlockSpec(sg.shape, lambda b, s, i: (0, 0))]
    out = pl.pallas_call(
        functools.partial(_attn_kernel, tq=tq, tk=tk, nk_total=nk, kv_len=kv_len, pos0=pos0, fox=fox,
                          lam_init=lam_init),
        grid=(bsz, ns // spb, Tq_pad // tq),
        in_specs=[qspec, kspec, vspec] + especs,
        out_specs=qspec,
        out_shape=jax.ShapeDtypeStruct((bsz, Tq_pad, w), BF16),
        scratch_shapes=[pltpu.VMEM((2 * spb, 1, tq), F32), pltpu.VMEM((2 * spb, 1, tq), F32),
                        pltpu.VMEM((spb if fox else 2 * spb, LANES, tq), F32), pltpu.VMEM((2 * spb, tk, tq), F32)],
        compiler_params=_params("parallel", "parallel", "arbitrary"),
        name="fox_attention" if fox else "diff_attention",
    )(q, k, vt, *extra)
    return out[:, :Tq]


def _rope_table_kernel(inv_ref, c_ref, s1_ref, s2_ref, *, pos0):
    T = c_ref.shape[0]
    pos = (pos0 + lax.broadcasted_iota(jnp.int32, (T, LANES), 0)).astype(F32)
    ang = pos * inv_ref[...]
    lane = lax.broadcasted_iota(jnp.int32, (T, LANES), 1) % HALF
    sn = jnp.sin(ang)
    c_ref[...] = jnp.cos(ang)
    s1_ref[...] = jnp.where(lane < ROT_DIM // 2, -sn, 0.0)
    s2_ref[...] = jnp.where(lane >= ROT_DIM // 2, sn, 0.0)


def _rope_tables(T, pos0):
    half = ROT_DIM // 2
    inv = ROPE_THETA ** (-jnp.arange(half, dtype=F32) * (2.0 / ROT_DIM))
    lane = jnp.arange(LANES) % HALF
    inv_lane = jnp.where(lane < ROT_DIM, inv[lane % half], 0.0).reshape(1, LANES).astype(F32)
    tab = jax.ShapeDtypeStruct((T, LANES), F32)
    return pl.pallas_call(
        functools.partial(_rope_table_kernel, pos0=pos0),
        out_shape=[tab, tab, tab],
        name="rope_tables",
    )(inv_lane)


def _inproj_odd_kernel(*refs, w, vt_tiles):
    x_ref, g_ref, w_ref, c_ref, s1_ref, s2_ref = refs[:6]
    q_ref, k_ref, v_ref, kb_ref, vx_ref = refs[-5:]
    hn = _rms(x_ref[...], g_ref[...]).astype(BF16)
    cs, s1, s2 = c_ref[...], s1_ref[...], s2_ref[...]
    half = ROT_DIM // 2

    def rope(t):
        out = []
        for h in range(w // LANES):
            s = t[:, h * LANES:(h + 1) * LANES]
            out.append(s * cs + pltpu.roll(s, LANES - half, 1) * s1 + pltpu.roll(s, half, 1) * s2)
        return jnp.concatenate(out, axis=1)

    q_ref[...] = (rope(_mm(hn, w_ref[:, 0:w])) * Q_SCALE).astype(BF16)
    _emit_kv(rope(_mm(hn, w_ref[:, w:2 * w])), _mm(hn, w_ref[:, 2 * w:3 * w]), k_ref, v_ref, kb_ref, vx_ref,
             DIFF_HEADS, vt_tiles)


def _inproj_odd(x2, g, w_in, tabs, tab_blocks, tm, layer, stacks, n_layers, vt_tiles_per_seq):
    n, d = x2.shape
    w = w_in.shape[1] // 3
    row = lambda c: pl.BlockSpec((tm, c), lambda i: (i, 0))
    tab = pl.BlockSpec((tm, LANES), lambda i: (i % tab_blocks, 0))
    kv_specs, kv_shapes, extra_in = _kv_out(n, w, DIFF_HEADS, tm, layer, stacks, n_layers, vt_tiles_per_seq)
    return pl.pallas_call(
        functools.partial(_inproj_odd_kernel, w=w, vt_tiles=bool(vt_tiles_per_seq)),
        grid=(n // tm,),
        in_specs=[row(d), _resident((1, d)), _resident((d, 3 * w)), tab, tab, tab]
        + [pl.BlockSpec(memory_space=pl.ANY)] * len(extra_in),
        out_specs=[row(w)] + kv_specs,
        out_shape=[jax.ShapeDtypeStruct((n, w), BF16)] + kv_shapes,
        input_output_aliases={6 + a: 1 + a for a in range(len(extra_in))},
        compiler_params=_params("parallel"),
        name="inproj_odd",
    )(x2, g, w_in, *tabs, *extra_in)


def _trunk(x, pos0, caches, wts, tm):
    bsz, T, d = x.shape
    n = bsz * T
    depth = wts["norm_g"].shape[0]
    w = d // 2
    x2 = x.reshape(n, d)
    tt = _pick_tile(T, 256)
    flat = lambda a: a.reshape(n, a.shape[-1])
    seq = lambda a: a.reshape(bsz, T, a.shape[-1])
    n_keys = T if caches is None else T + caches[0].shape[2]
    key_pad = -n_keys % LANES
    pad_keys = lambda a: jnp.pad(a, ((0, 0), (0, key_pad), (0, 0))) if key_pad else a
    rope = None
    n_even, n_odd = (depth + 1) // 2, depth // 2
    vt_tiles_per_seq = T // tm if caches is None and tm == ATTN_TK and T % tm == 0 else 0
    fox_kv = diff_kv = None
    even, odd = [], []
    for l in range(depth):
        g = wts["norm_g"][l].reshape(6, 1, d)
        x2 = _ffn(x2, g[0], g[1], wts["w_ffn_in"], wts["w_ffn_out"], (l, 0), tm)
        if l % 2 == 0:
            e = l // 2
            u, gate, q, logf, logf1, *fox_kv, kb, vx = _inproj_even(
                x2, g[2], wts["w_in_even"][e], wts["b_fox_f"][e], tm, e, fox_kv, n_even, vt_tiles_per_seq)
            if caches is None:
                h0 = jnp.zeros((bsz, 1, w), F32)
                buf0 = jnp.zeros((bsz, 8, w), F32)
                kb_all, logf_all = seq(kb), seq(logf)
                vb_all = vx if vt_tiles_per_seq else seq(vx)
            else:
                fk, fv, flogf, lru_h, lru_conv = (c[e] for c in caches[:5])
                h0 = lru_h.reshape(bsz, 1, w)
                buf0 = jnp.pad(lru_conv, ((0, 0), (8 - (CONV_W - 1), 0), (0, 0)))
                past = fk.shape[1]
                kb_all = jnp.concatenate([fk.reshape(bsz, past, w).astype(BF16), seq(kb)], axis=1)
                vb_all = jnp.concatenate([fv.reshape(bsz, past, w).astype(BF16), seq(vx)], axis=1)
                logf_all = jnp.concatenate([jnp.repeat(flogf, HALF, axis=-1), seq(logf)], axis=1)
            ya, h_last, nbuf = _lru(seq(u), seq(gate), h0, buf0, wts["lru_conv_w"][e], wts["lru_conv_b"][e],
                                    wts["lru_wr"][e], wts["lru_wi"][e], wts["lru_br"][e], wts["lru_bi"][e],
                                    wts["lru_lambda"][e], tt)
            Tk = logf_all.shape[1]
            Tp = Tk + key_pad
            ck, cq = _cumsum(pad_keys(logf_all), Tp if Tp <= 2 * CUMSUM_TILE else _pick_tile(Tp, CUMSUM_TILE))
            yb = _attention(seq(q), pad_keys(kb_all), pad_keys(vb_all), (cq[..., Tk - T:Tk], ck),
                            pos0=pos0, kv_len=Tk, fox=True)
            mix = (g[3], wts["w_out_even"][e], [flat(ya), flat(yb)])
            even.append((seq(logf1)[..., :FOX_HEADS], h_last.reshape(bsz, w), nbuf))
        else:
            o = l // 2
            lam_init = 0.8 - 0.6 * math.exp(-0.3 * l)
            if rope is None:
                tabs = _rope_tables(T, pos0)
                if T % tm:
                    tabs = [jnp.tile(t, (n // T, 1)) for t in tabs]
                rope = (tabs, tabs[0].shape[0] // tm)
            q, *diff_kv, kb, vx = _inproj_odd(x2, g[2], wts["w_in_odd"][o], rope[0], rope[1], tm, o, diff_kv,
                                              n_odd, vt_tiles_per_seq)
            if caches is None:
                kb_all = seq(kb)
                vb_all = vx if vt_tiles_per_seq else seq(vx)
            else:
                dk, dv = (c[o] for c in caches[5:])
                past = dk.shape[1]
                kb_all = jnp.concatenate([dk.reshape(bsz, past, d).astype(BF16), seq(kb)], axis=1)
                vb_all = jnp.concatenate([dv.reshape(bsz, past, d).astype(BF16), seq(vx)], axis=1)
            y = _attention(seq(q), pad_keys(kb_all), pad_keys(vb_all),
                           (wts["diff_lambda"][o], wts["diff_subln_g"][o]),
                           pos0=pos0, kv_len=kb_all.shape[1], fox=False, lam_init=lam_init)
            mix = (g[3], wts["w_out_odd"][o], [flat(y)])
        x2 = _ffn(x2, g[4], g[5], wts["w_ffn_in"], wts["w_ffn_out"], (l, 1), tm, mix)
    per_seq = lambda a: a.reshape(a.shape[0], bsz, T, *a.shape[2:])
    even_out = [per_seq(a) for a in fox_kv] + [jnp.stack(z) for z in zip(*even)]
    return x2.reshape(bsz, T, d), even_out, [per_seq(a) for a in diff_kv]


def _prepare_weights(norm_g, w_ffn_in, w_ffn_out, w_in_even, b_fox_f, lru_conv_w, lru_conv_b, lru_w_gates,
                     lru_b_gates, lru_lambda, w_out_even, w_in_odd, diff_lambda, diff_subln_g, w_out_odd):
    n_even, d, _ = w_in_even.shape
    w = d // 2
    blk = w // LRU_BLOCKS
    wf = jnp.repeat(w_in_even[:, :, 5 * w:], HALF, axis=-1)
    pad_lanes = lambda a: jnp.pad(a, [(0, 0)] * (a.ndim - 1) + [(0, LANES - a.shape[-1])])
    eye = jnp.eye(LRU_BLOCKS, dtype=F32)
    dense = lambda g: jnp.einsum("encg,nm->encmg", g, eye).reshape(n_even, w, w)
    return dict(
        norm_g=norm_g, w_ffn_in=w_ffn_in.astype(BF16), w_ffn_out=w_ffn_out.astype(BF16),
        w_in_even=jnp.concatenate([w_in_even[:, :, :5 * w], wf, pad_lanes(w_in_even[:, :, 5 * w:])],
                                  axis=-1).astype(BF16),
        b_fox_f=jnp.concatenate([jnp.repeat(b_fox_f, HALF, axis=-1), pad_lanes(b_fox_f)],
                                axis=-1).reshape(n_even, 1, w + LANES),
        lru_conv_w=lru_conv_w, lru_conv_b=lru_conv_b.reshape(n_even, 1, w),
        lru_wr=dense(lru_w_gates[..., :blk]).astype(BF16), lru_wi=dense(lru_w_gates[..., blk:]).astype(BF16),
        lru_br=lru_b_gates[..., :blk].reshape(n_even, 1, w), lru_bi=lru_b_gates[..., blk:].reshape(n_even, 1, w),
        lru_lambda=lru_lambda.reshape(n_even, 1, w), w_out_even=w_out_even.astype(BF16),
        w_in_odd=w_in_odd.astype(BF16), diff_lambda=diff_lambda, diff_subln_g=diff_subln_g.reshape(-1, 1, LANES),
        w_out_odd=w_out_odd.astype(BF16))


def kernel(x_prompt, x_sample, cache_fox_k, cache_fox_v, cache_fox_logf, state_lru_h, state_lru_conv, cache_diff_k, cache_diff_v, norm_g, w_ffn_in, w_ffn_out, w_in_even, b_fox_f, lru_conv_w, lru_conv_b, lru_w_gates, lru_b_gates, lru_lambda, w_out_even, w_in_odd, diff_lambda, diff_subln_g, w_out_odd):
    wts = _prepare_weights(norm_g, w_ffn_in, w_ffn_out, w_in_even, b_fox_f, lru_conv_w, lru_conv_b, lru_w_gates,
                           lru_b_gates, lru_lambda, w_out_even, w_in_odd, diff_lambda, diff_subln_g, w_out_odd)
    y_p, pe, po = _trunk(x_prompt, 0, None, wts, _pick_tile(x_prompt.shape[0] * x_prompt.shape[1], 512))
    caches = (cache_fox_k, cache_fox_v, cache_fox_logf, state_lru_h, state_lru_conv, cache_diff_k, cache_diff_v)
    y_s, se, so = _trunk(x_sample, cache_fox_k.shape[2], caches, wts,
                         _pick_tile(x_sample.shape[0] * x_sample.shape[1], 256))
    return (y_p, y_s, *pe, *po, *se, *so)
```

```python
import functools
import math

import jax
import jax.numpy as jnp
from jax import lax
from jax.experimental import pallas as pl
from jax.experimental.pallas import tpu as pltpu

F32 = jnp.float32
BF16 = jnp.bfloat16

EPS = 1e-6
NEG_INF = -1e30
LRU_C = 8.0
LRU_BLOCKS = 8
CONV_W = 4
FOX_HEADS = 8
DIFF_HEADS = 8
CHUNK = 64
ROT_DIM = 16
ROPE_THETA = 500000.0

LANES = 128
SUBLANES = 8
HALF = LANES // 2
FF_CHUNK = 256
AHEAD = 2
ATTN_TQ, ATTN_TK = 512, 512
CUMSUM_TILE = 1024
LOG2E = 1.0 / math.log(2.0)
Q_SCALE = HALF ** -0.5 * LOG2E
VMEM_LIMIT = 56 * 1024 * 1024


def _params(*sem):
    return pltpu.CompilerParams(dimension_semantics=sem, vmem_limit_bytes=VMEM_LIMIT)


def _resident(shape, lead=()):
    nd = len(shape)
    return pl.BlockSpec((None,) * len(lead) + tuple(shape), lambda *_: tuple(lead) + (0,) * nd,
                        pipeline_mode=pl.Buffered(1))


def _rms(x, g):
    return x * lax.rsqrt(jnp.mean(x * x, axis=-1, keepdims=True) + EPS) * g


def _mm(a, b):
    return jnp.dot(a, b, preferred_element_type=F32)


def _mm_nt(a, b):
    return lax.dot_general(a, b, (((1,), (1,)), ((), ())), preferred_element_type=F32)


def _log_sigmoid(x):
    return jnp.minimum(x, 0.0) - jnp.log1p(jnp.exp(-jnp.abs(x)))


def _softplus(x):
    return jnp.maximum(x, 0.0) + jnp.log1p(jnp.exp(-jnp.abs(x)))


def _pick_tile(n, target):
    t = min(n, target)
    while n % t:
        t //= 2
    return t


def _ffn_kernel(x_ref, gpre_ref, gpost_ref, wi_ref, wo_ref, *rest, dff):
    *mix, o_ref, act_ref = rest
    x = x_ref[...]
    if mix:
        gmix_ref, wmix_ref, *y_refs = mix
        y = jnp.concatenate([r[...] for r in y_refs], axis=1)
        x = x + _rms(_mm(y, wmix_ref[...]), gmix_ref[...])
    xn = _rms(x, gpre_ref[...]).astype(BF16)
    for c0 in range(0, dff, FF_CHUNK):
        g = _mm(xn, wi_ref[:, c0:c0 + FF_CHUNK])
        u = _mm(xn, wi_ref[:, dff + c0:dff + c0 + FF_CHUNK])
        act_ref[:, c0:c0 + FF_CHUNK] = ((g * jax.nn.sigmoid(g)) * u).astype(BF16)
    y = _mm(act_ref[...], wo_ref[...])
    o_ref[...] = x + 0.5 * _rms(y, gpost_ref[...])


def _ffn(x2, g_pre, g_post, w_in, w_out, which, tm, mix=None):
    n, d = x2.shape
    dff = w_out.shape[-2]
    assert dff % FF_CHUNK == 0 and n % tm == 0
    row = lambda c: pl.BlockSpec((tm, c), lambda i: (i, 0))
    mix_in, mix_specs = [], []
    if mix is not None:
        g_mix, w_mix, ys = mix
        mix_in = [g_mix, w_mix, *ys]
        mix_specs = [_resident((1, d)), _resident(w_mix.shape)] + [row(y.shape[1]) for y in ys]
    return pl.pallas_call(
        functools.partial(_ffn_kernel, dff=dff),
        grid=(n // tm,),
        in_specs=[row(d), _resident((1, d)), _resident((1, d)), _resident((d, 2 * dff), which),
                  _resident((dff, d), which)] + mix_specs,
        out_specs=row(d),
        out_shape=jax.ShapeDtypeStruct((n, d), F32),
        scratch_shapes=[pltpu.VMEM((tm, dff), BF16)],
        compiler_params=_params("parallel"),
        name="ffn_mix" if mix is not None else "ffn",
    )(x2, g_pre, g_post, w_in, w_out, *mix_in)


def _emit_kv(k, v, k_ref, v_ref, kb_ref, vx_ref, heads, vt_tiles):
    tm, w = k.shape
    k_ref[0] = k.reshape(tm, heads, w // heads)
    v_ref[0] = v.reshape(tm, heads, w // heads)
    kb_ref[...] = k.astype(BF16)
    if vt_tiles:
        vx_ref[0, :, 0] = v.T.astype(BF16).reshape(w // LANES, LANES, tm)
    else:
        vx_ref[...] = v.astype(BF16)


def _kv_out(n, w, heads, tm, layer, stacks, n_layers, vt_tiles_per_seq):
    dh = w // heads
    stacked = pl.BlockSpec((1, tm, heads, dh), lambda i: (layer, i, 0, 0))
    stacked_shape = jax.ShapeDtypeStruct((n_layers, n, heads, dh), F32)
    row = pl.BlockSpec((tm, w), lambda i: (i, 0))
    if vt_tiles_per_seq:
        nkb = vt_tiles_per_seq
        vx = pl.BlockSpec((1, w // LANES, 1, LANES, tm), lambda i: (i // nkb, 0, i % nkb, 0, 0))
        vx_shape = jax.ShapeDtypeStruct((n // (nkb * tm), w // LANES, nkb, LANES, tm), BF16)
    else:
        vx, vx_shape = row, jax.ShapeDtypeStruct((n, w), BF16)
    specs = [stacked, stacked, row, vx]
    shapes = [stacked_shape, stacked_shape, jax.ShapeDtypeStruct((n, w), BF16), vx_shape]
    extra_in = [] if stacks is None else list(stacks)
    return specs, shapes, extra_in


def _inproj_even_kernel(*refs, w, vt_tiles):
    x_ref, g_ref, w_ref, bf_ref = refs[:4]
    u_ref, gate_ref, q_ref, logf_ref, logf1_ref, k_ref, v_ref, kb_ref, vx_ref = refs[-9:]
    hn = _rms(x_ref[...], g_ref[...]).astype(BF16)
    u_ref[...] = _mm(hn, w_ref[:, 0:w])
    gate_ref[...] = _mm(hn, w_ref[:, w:2 * w])
    q_ref[...] = (_mm(hn, w_ref[:, 2 * w:3 * w]) * Q_SCALE).astype(BF16)
    logf_ref[...] = _log_sigmoid(_mm(hn, w_ref[:, 5 * w:6 * w]) + bf_ref[:, 0:w])
    logf1_ref[...] = _log_sigmoid(_mm(hn, w_ref[:, 6 * w:6 * w + LANES]) + bf_ref[:, w:w + LANES])
    _emit_kv(_mm(hn, w_ref[:, 3 * w:4 * w]), _mm(hn, w_ref[:, 4 * w:5 * w]), k_ref, v_ref, kb_ref, vx_ref,
             FOX_HEADS, vt_tiles)


def _inproj_even(x2, g, w_wide, bf_wide, tm, layer, stacks, n_layers, vt_tiles_per_seq):
    n, d = x2.shape
    w = (w_wide.shape[1] - LANES) // 6
    row = lambda c: pl.BlockSpec((tm, c), lambda i: (i, 0))
    f = jax.ShapeDtypeStruct((n, w), F32)
    b = jax.ShapeDtypeStruct((n, w), BF16)
    kv_specs, kv_shapes, extra_in = _kv_out(n, w, FOX_HEADS, tm, layer, stacks, n_layers, vt_tiles_per_seq)
    return pl.pallas_call(
        functools.partial(_inproj_even_kernel, w=w, vt_tiles=bool(vt_tiles_per_seq)),
        grid=(n // tm,),
        in_specs=[row(d), _resident((1, d)), _resident(w_wide.shape), _resident(bf_wide.shape)]
        + [pl.BlockSpec(memory_space=pl.ANY)] * len(extra_in),
        out_specs=[row(w)] * 4 + [row(LANES)] + kv_specs,
        out_shape=[f, f, b, f, jax.ShapeDtypeStruct((n, LANES), F32)] + kv_shapes,
        input_output_aliases={4 + a: 5 + a for a in range(len(extra_in))},
        compiler_params=_params("parallel"),
        name="inproj_even",
    )(x2, g, w_wide, bf_wide, *extra_in)


def _scan_rows(a, b, h):
    tt, c = b.shape
    assert tt % SUBLANES == 0
    groups = (tt // SUBLANES, SUBLANES, c)
    b = b.reshape(groups)
    a = None if a is None else a.reshape(groups)
    sub = lax.broadcasted_iota(jnp.int32, groups, 1)
    d = 1
    while d < SUBLANES:
        keep = sub >= d
        if a is None:
            b = b + jnp.where(keep, pltpu.roll(b, d, 1), 0.0)
        else:
            b = a * jnp.where(keep, pltpu.roll(b, d, 1), 0.0) + b
            a = a * jnp.where(keep, pltpu.roll(a, d, 1), 1.0)
        d *= 2
    out = []
    for g in range(groups[0]):
        blk = b[g] + (h if a is None else a[g] * h)
        out.append(blk)
        h = blk[SUBLANES - 1:SUBLANES]
    return jnp.concatenate(out, axis=0)


def _lru_kernel(u_ref, gate_ref, h0_ref, buf0_ref, cw_ref, cb_ref, wr_ref, wi_ref, br_ref, bi_ref, lam_ref,
                ya_ref, hlast_ref, nbuf_ref, ubuf, hcar, *, tt):
    t = pl.program_id(1)
    pad = ubuf.shape[0] - tt

    @pl.when(t == 0)
    def _():
        ubuf[0:pad, :] = buf0_ref[0]
        hcar[...] = h0_ref[0]

    u = u_ref[0]
    ubuf[pad:pad + tt, :] = u
    uc = cb_ref[...]
    for j in range(CONV_W):
        o = pad - (CONV_W - 1) + j
        uc = uc + ubuf[o:o + tt, :] * cw_ref[j:j + 1, :]
    ucb = uc.astype(BF16)
    r = jax.nn.sigmoid(_mm(ucb, wr_ref[...]) + br_ref[...])
    ig = jax.nn.sigmoid(_mm(ucb, wi_ref[...]) + bi_ref[...])
    log_a = (-LRU_C * r) * _softplus(-lam_ref[...])
    a = jnp.exp(log_a)
    bx = jnp.sqrt(-jnp.tanh(log_a) * (a * a + 1.0)) * (ig * uc)
    bx = _scan_rows(a, bx, hcar[...])
    hcar[...] = bx[tt - 1:tt, :]
    ya_ref[0] = (bx * jax.nn.gelu(gate_ref[0])).astype(BF16)
    ubuf[0:pad, :] = ubuf[tt:tt + pad, :]

    @pl.when(t == pl.num_programs(1) - 1)
    def _():
        hlast_ref[0] = bx[tt - 1:tt, :]
        nbuf_ref[0] = ubuf[pad - (CONV_W - 1):pad, :]


def _lru(u, gate, h0, buf0, cw, cb, wr, wi, br, bi, lam, tt):
    bsz, T, w = u.shape
    pad = 8
    seq = pl.BlockSpec((1, tt, w), lambda b, t: (b, t, 0))
    per_b = lambda r: pl.BlockSpec((1, r, w), lambda b, t: (b, 0, 0))
    return pl.pallas_call(
        functools.partial(_lru_kernel, tt=tt),
        grid=(bsz, T // tt),
        in_specs=[seq, seq, per_b(1), per_b(pad), _resident((CONV_W, w)), _resident((1, w)), _resident((w, w)),
                  _resident((w, w)), _resident((1, w)), _resident((1, w)), _resident((1, w))],
        out_specs=[seq, per_b(1), per_b(CONV_W - 1)],
        out_shape=[jax.ShapeDtypeStruct((bsz, T, w), BF16), jax.ShapeDtypeStruct((bsz, 1, w), F32),
                   jax.ShapeDtypeStruct((bsz, CONV_W - 1, w), F32)],
        scratch_shapes=[pltpu.VMEM((tt + pad, w), F32), pltpu.VMEM((1, w), F32)],
        compiler_params=_params("parallel", "arbitrary"),
        name="conv_rglru",
    )(u, gate, h0, buf0, cw, cb, wr, wi, br, bi, lam)


def _cumsum_kernel(x_ref, ck_ref, cq_ref, car, *, tt):
    @pl.when(pl.program_id(1) == 0)
    def _():
        car[...] = jnp.zeros_like(car)

    x = _scan_rows(None, x_ref[0], car[...])
    car[...] = x[tt - 1:tt, :]
    x = x * LOG2E
    lo = lax.broadcasted_iota(jnp.int32, (1, LANES), 1) < HALF
    for s in range(x.shape[1] // LANES):
        xs = x[:, s * LANES:(s + 1) * LANES]
        sw = pltpu.roll(xs, HALF, 1)
        ck_ref[0, :, (2 * s) * LANES:(2 * s + 1) * LANES] = jnp.where(lo, xs, sw)
        ck_ref[0, :, (2 * s + 1) * LANES:(2 * s + 2) * LANES] = jnp.where(lo, sw, xs)
        tr = xs.T
        cq_ref[0, s, 0:1, :] = tr[0:1, :]
        cq_ref[0, s, 1:2, :] = tr[HALF:HALF + 1, :]


def _cumsum(x, tt):
    bsz, T, w = x.shape
    ns = w // LANES
    return pl.pallas_call(
        functools.partial(_cumsum_kernel, tt=tt),
        grid=(bsz, T // tt),
        in_specs=[pl.BlockSpec((1, tt, w), lambda b, t: (b, t, 0))],
        out_specs=[pl.BlockSpec((1, tt, 2 * w), lambda b, t: (b, t, 0)),
                   pl.BlockSpec((1, ns, 2, tt), lambda b, t: (b, 0, 0, t))],
        out_shape=[jax.ShapeDtypeStruct((bsz, T, 2 * w), F32), jax.ShapeDtypeStruct((bsz, ns, 2, T), F32)],
        scratch_shapes=[pltpu.VMEM((1, w), F32)],
        compiler_params=_params("parallel", "arbitrary"),
        name="logf_cumsum",
    )(x)


def _attn_kernel(*refs, tq, tk, nk_total, kv_len, pos0, fox, lam_init):
    if fox:
        q_ref, k_ref, vt_ref, cq_ref, ck_ref, o_ref, m_sc, l_sc, acc_sc, s_sc = refs
    else:
        q_ref, k_ref, vt_ref, lp_ref, sg_ref, o_ref, m_sc, l_sc, acc_sc, s_sc = refs
    i = pl.program_id(2)
    q0 = pos0 + i * tq
    shift = CHUNK.bit_length() - 1
    lo = lax.broadcasted_iota(jnp.int32, (1, LANES), 1) < HALF
    spb = q_ref.shape[2] // LANES
    qm = []
    for sl in range(spb):
        q = q_ref[0, :, sl * LANES:(sl + 1) * LANES]
        qm += [jnp.where(lo, q, jnp.zeros_like(q)), jnp.where(lo, jnp.zeros_like(q), q)]
    if fox:
        n_full = (q0 + 1) // tk
        n_need = (q0 + tq - 1) // tk + 1
    else:
        n_full = (((q0 >> shift) + 1) << shift) // tk
        n_need = (((((q0 + tq - 1) >> shift) + 1) << shift) + tk - 1) // tk
    n_need = jnp.minimum(n_need, nk_total)
    n_full = jnp.minimum(jnp.minimum(n_full, kv_len // tk), n_need)

    m_sc[...] = jnp.full_like(m_sc, NEG_INF)
    l_sc[...] = jnp.zeros_like(l_sc)
    acc_sc[...] = jnp.zeros_like(acc_sc)

    n_maps = 2 * spb

    def scores(g, j):
        k0 = j * tk if isinstance(j, int) else pl.multiple_of(j * tk, tk)
        lanes = slice((g // 2) * LANES, (g // 2 + 1) * LANES)
        hk = tk // 2
        return jnp.concatenate([_mm_nt(k_ref[0, pl.ds(k0 + r, hk), lanes], qm[g]) for r in (0, hk)], axis=0)

    assert AHEAD < n_maps
    for a in range(AHEAD):
        s_sc[a] = scores(a, 0)

    def tile(j, masked):
        k0 = pl.multiple_of(j * tk, tk)
        if masked:
            kpos = k0 + lax.broadcasted_iota(jnp.int32, (tk, tq), 0)
            qpos = q0 + lax.broadcasted_iota(jnp.int32, (tk, tq), 1)
            vis = (qpos >= kpos) if fox else ((qpos >> shift) >= (kpos >> shift))
            if kv_len < nk_total * tk:
                vis = vis & (kpos < kv_len)
        for g in range(n_maps):
            sl, c = divmod(g, 2)
            dj, g_ahead = divmod(g + AHEAD, n_maps)
            s_sc[g_ahead] = scores(g_ahead, jnp.minimum(j + dj, nk_total - 1) if dj else j)
            s = s_sc[g]
            if fox:
                ck = ck_ref[0, pl.ds(k0, tk), g * LANES:(g + 1) * LANES]
                s = s - jnp.concatenate([ck] * (tq // LANES), axis=1)
            if masked:
                s = jnp.where(vis, s, NEG_INF)
            m_prev = m_sc[g]
            m_tile = jnp.max(s, axis=0, keepdims=True)
            if fox:
                cq = cq_ref[0, sl, c:c + 1, :]
                m_new = jnp.maximum(m_prev, m_tile + cq)
                p = jnp.exp2(s - (m_new - cq))
            else:
                m_new = jnp.maximum(m_prev, m_tile)
                p = jnp.exp2(s - m_new)
            alpha = jnp.exp2(m_prev - m_new)
            l_sc[g] = alpha * l_sc[g] + jnp.sum(p, axis=0, keepdims=True)
            m_sc[g] = m_new
            pt = p.astype(BF16)
            vt = vt_ref[0, sl, j]
            if fox:
                r = slice(c * HALF, (c + 1) * HALF)
                acc_sc[sl, r, :] = alpha * acc_sc[sl, r, :] + _mm(vt[r, :], pt)
            else:
                acc_sc[g] = alpha * acc_sc[g] + _mm(vt, pt)

    def full_body(j, carry):
        tile(j, False)
        return carry

    def masked_body(j, carry):
        tile(j, True)
        return carry

    lax.fori_loop(0, n_full, full_body, 0)
    lax.fori_loop(n_full, n_need, masked_body, 0)

    for sl in range(spb):
        a, b = 2 * sl, 2 * sl + 1
        if fox:
            ot = jnp.concatenate([acc_sc[sl, 0:HALF, :] / l_sc[a], acc_sc[sl, HALF:LANES, :] / l_sc[b]], axis=0)
            o = ot.T
        else:
            lp = lp_ref[...]
            lam = (jnp.exp(jnp.sum(lp[0:1] * lp[1:2], axis=-1, keepdims=True))
                   - jnp.exp(jnp.sum(lp[2:3] * lp[3:4], axis=-1, keepdims=True)) + lam_init)
            ot = acc_sc[a] / l_sc[a] - lam * (acc_sc[b] / l_sc[b])
            ot = ot * lax.rsqrt(jnp.mean(ot * ot, axis=0, keepdims=True) + EPS)
            o = (ot.T * sg_ref[...]) * (1.0 - lam_init)
        o_ref[0, :, sl * LANES:(sl + 1) * LANES] = o.astype(BF16)


def _attention(q, k, v, extra, *, pos0, kv_len, fox, lam_init=0.0, tq=ATTN_TQ, tk=ATTN_TK, spb=4):
    bsz, Tq, w = q.shape
    Tk = k.shape[1]
    ns = w // LANES
    Tq_pad = -(-Tq // LANES) * LANES
    tq = _pick_tile(Tq_pad, tq)
    tk = tk if Tk % tk == 0 else Tk
    nk = Tk // tk
    q = jnp.pad(q, ((0, 0), (0, Tq_pad - Tq), (0, 0)))
    vt = v if v.ndim == 5 else v.reshape(bsz, nk, tk, ns, LANES).transpose(0, 3, 1, 4, 2)
    assert vt.shape == (bsz, ns, nk, LANES, tk)
    assert ns % spb == 0
    sw = spb * LANES
    qspec = pl.BlockSpec((1, tq, sw), lambda b, s, i: (b, i, s))
    kspec = pl.BlockSpec((1, Tk, sw), lambda b, s, i: (b, 0, s))
    vspec = pl.BlockSpec((1, spb, nk, LANES, tk), lambda b, s, i: (b, s, 0, 0, 0))
    if fox:
        cq, ck = extra
        extra = (jnp.pad(cq, ((0, 0), (0, 0), (0, 0), (0, Tq_pad - Tq))), ck)
        especs = [pl.BlockSpec((1, spb, 2, tq), lambda b, s, i: (b, s, 0, i)),
                  pl.BlockSpec((1, Tk, 2 * sw), lambda b, s, i: (b, 0, s), pipeline_mode=pl.Buffered(1))]
    else:
        lp, sg = extra
        especs = [pl.BlockSpec(lp.shape, lambda b, s, i: (0, 0)), pl.BlockSpec(sg.shape, lambda b, s, i: (0, 0))]
    out = pl.pallas_call(
        functools.partial(_attn_kernel, tq=tq, tk=tk, nk_total=nk, kv_len=kv_len, pos0=pos0, fox=fox,
                          lam_init=lam_init),
        grid=(bsz, ns // spb, Tq_pad // tq),
        in_specs=[qspec, kspec, vspec] + especs,
        out_specs=qspec,
        out_shape=jax.ShapeDtypeStruct((bsz, Tq_pad, w), BF16),
        scratch_shapes=[pltpu.VMEM((2 * spb, 1, tq), F32), pltpu.VMEM((2 * spb, 1, tq), F32),
                        pltpu.VMEM((spb if fox else 2 * spb, LANES, tq), F32), pltpu.VMEM((2 * spb, tk, tq), F32)],
        compiler_params=_params("parallel", "parallel", "arbitrary"),
        name="fox_attention" if fox else "diff_attention",
    )(q, k, vt, *extra)
    return out[:, :Tq]


def _rope_table_kernel(inv_ref, c_ref, s1_ref, s2_ref, *, pos0):
    T = c_ref.shape[0]
    pos = (pos0 + lax.broadcasted_iota(jnp.int32, (T, LANES), 0)).astype(F32)
    ang = pos * inv_ref[...]
    lane = lax.broadcasted_iota(jnp.int32, (T, LANES), 1) % HALF
    sn = jnp.sin(ang)
    c_ref[...] = jnp.cos(ang)
    s1_ref[...] = jnp.where(lane < ROT_DIM // 2, -sn, 0.0)
    s2_ref[...] = jnp.where(lane >= ROT_DIM // 2, sn, 0.0)


def _rope_tables(T, pos0):
    half = ROT_DIM // 2
    inv = ROPE_THETA ** (-jnp.arange(half, dtype=F32) * (2.0 / ROT_DIM))
    lane = jnp.arange(LANES) % HALF
    inv_lane = jnp.where(lane < ROT_DIM, inv[lane % half], 0.0).reshape(1, LANES).astype(F32)
    tab = jax.ShapeDtypeStruct((T, LANES), F32)
    return pl.pallas_call(
        functools.partial(_rope_table_kernel, pos0=pos0),
        out_shape=[tab, tab, tab],
        name="rope_tables",
    )(inv_lane)


def _inproj_odd_kernel(*refs, w, vt_tiles):
    x_ref, g_ref, w_ref, c_ref, s1_ref, s2_ref = refs[:6]
    q_ref, k_ref, v_ref, kb_ref, vx_ref = refs[-5:]
    hn = _rms(x_ref[...], g_ref[...]).astype(BF16)
    cs, s1, s2 = c_ref[...], s1_ref[...], s2_ref[...]
    half = ROT_DIM // 2

    def rope(t):
        out = []
        for h in range(w // LANES):
            s = t[:, h * LANES:(h + 1) * LANES]
            out.append(s * cs + pltpu.roll(s, LANES - half, 1) * s1 + pltpu.roll(s, half, 1) * s2)
        return jnp.concatenate(out, axis=1)

    q_ref[...] = (rope(_mm(hn, w_ref[:, 0:w])) * Q_SCALE).astype(BF16)
    _emit_kv(rope(_mm(hn, w_ref[:, w:2 * w])), _mm(hn, w_ref[:, 2 * w:3 * w]), k_ref, v_ref, kb_ref, vx_ref,
             DIFF_HEADS, vt_tiles)


def _inproj_odd(x2, g, w_in, tabs, tab_blocks, tm, layer, stacks, n_layers, vt_tiles_per_seq):
    n, d = x2.shape
    w = w_in.shape[1] // 3
    row = lambda c: pl.BlockSpec((tm, c), lambda i: (i, 0))
    tab = pl.BlockSpec((tm, LANES), lambda i: (i % tab_blocks, 0))
    kv_specs, kv_shapes, extra_in = _kv_out(n, w, DIFF_HEADS, tm, layer, stacks, n_layers, vt_tiles_per_seq)
    return pl.pallas_call(
        functools.partial(_inproj_odd_kernel, w=w, vt_tiles=bool(vt_tiles_per_seq)),
        grid=(n // tm,),
        in_specs=[row(d), _resident((1, d)), _resident((d, 3 * w)), tab, tab, tab]
        + [pl.BlockSpec(memory_space=pl.ANY)] * len(extra_in),
        out_specs=[row(w)] + kv_specs,
        out_shape=[jax.ShapeDtypeStruct((n, w), BF16)] + kv_shapes,
        input_output_aliases={6 + a: 1 + a for a in range(len(extra_in))},
        compiler_params=_params("parallel"),
        name="inproj_odd",
    )(x2, g, w_in, *tabs, *extra_in)


def _history_kv_kernel(ck_ref, cv_ref, kn_ref, vn_ref, k_ref, vt_ref, v_sc):
    past, w = ck_ref.shape[2], k_ref.shape[2]
    new = kn_ref.shape[1]
    tp = k_ref.shape[1]
    k_ref[0, 0:past, :] = ck_ref[0, 0].reshape(past, w).astype(BF16)
    k_ref[0, past:past + new, :] = kn_ref[0]
    k_ref[0, past + new:tp, :] = jnp.zeros((tp - past - new, w), BF16)
    v_sc[0:past, :] = cv_ref[0, 0].reshape(past, w)
    v_sc[past:past + new, :] = vn_ref[0].astype(F32)
    v_sc[past + new:tp, :] = jnp.zeros((tp - past - new, w), F32)
    vt_ref[0, :, 0] = v_sc[...].T.astype(BF16).reshape(w // LANES, LANES, tp)


def _history_kv(cache_k, cache_v, layer, k_new, v_new, tp):
    _, bsz, past, heads, dh = cache_k.shape
    T, w = k_new.shape[1:]
    cspec = pl.BlockSpec((1, 1, past, heads, dh), lambda b: (layer, b, 0, 0, 0))
    nspec = pl.BlockSpec((1, T, w), lambda b: (b, 0, 0))
    return pl.pallas_call(
        _history_kv_kernel,
        grid=(bsz,),
        in_specs=[cspec, cspec, nspec, nspec],
        out_specs=[pl.BlockSpec((1, tp, w), lambda b: (b, 0, 0)),
                   pl.BlockSpec((1, w // LANES, 1, LANES, tp), lambda b: (b, 0, 0, 0, 0))],
        out_shape=[jax.ShapeDtypeStruct((bsz, tp, w), BF16),
                   jax.ShapeDtypeStruct((bsz, w // LANES, 1, LANES, tp), BF16)],
        scratch_shapes=[pltpu.VMEM((tp, w), F32)],
        compiler_params=_params("parallel"),
        name="history_kv",
    )(cache_k, cache_v, k_new, v_new)


def _trunk(x, pos0, caches, wts, tm):
    bsz, T, d = x.shape
    n = bsz * T
    depth = wts["norm_g"].shape[0]
    w = d // 2
    x2 = x.reshape(n, d)
    tt = _pick_tile(T, 256)
    flat = lambda a: a.reshape(n, a.shape[-1])
    seq = lambda a: a.reshape(bsz, T, a.shape[-1])
    n_keys = T if caches is None else T + caches[0].shape[2]
    key_pad = -n_keys % LANES
    pad_keys = lambda a: jnp.pad(a, ((0, 0), (0, key_pad), (0, 0))) if key_pad else a
    rope = None
    n_even, n_odd = (depth + 1) // 2, depth // 2
    vt_tiles_per_seq = T // tm if caches is None and tm == ATTN_TK and T % tm == 0 else 0
    fox_kv = diff_kv = None
    even, odd = [], []
    for l in range(depth):
        g = wts["norm_g"][l].reshape(6, 1, d)
        x2 = _ffn(x2, g[0], g[1], wts["w_ffn_in"], wts["w_ffn_out"], (l, 0), tm)
        if l % 2 == 0:
            e = l // 2
            u, gate, q, logf, logf1, *fox_kv, kb, vx = _inproj_even(
                x2, g[2], wts["w_in_even"][e], wts["b_fox_f"][e], tm, e, fox_kv, n_even, vt_tiles_per_seq)
            if caches is None:
                h0 = jnp.zeros((bsz, 1, w), F32)
                buf0 = jnp.zeros((bsz, 8, w), F32)
                kb_all, logf_all = seq(kb), seq(logf)
                vb_all = vx if vt_tiles_per_seq else seq(vx)
            else:
                flogf, lru_h, lru_conv = (c[e] for c in caches[2:5])
                h0 = lru_h.reshape(bsz, 1, w)
                buf0 = jnp.pad(lru_conv, ((0, 0), (8 - (CONV_W - 1), 0), (0, 0)))
                kb_all, vb_all = _history_kv(caches[0], caches[1], e, seq(kb), seq(vx), n_keys + key_pad)
                logf_all = jnp.concatenate([jnp.repeat(flogf, HALF, axis=-1), seq(logf)], axis=1)
            ya, h_last, nbuf = _lru(seq(u), seq(gate), h0, buf0, wts["lru_conv_w"][e], wts["lru_conv_b"][e],
                                    wts["lru_wr"][e], wts["lru_wi"][e], wts["lru_br"][e], wts["lru_bi"][e],
                                    wts["lru_lambda"][e], tt)
            Tk = logf_all.shape[1]
            Tp = Tk + key_pad
            ck, cq = _cumsum(pad_keys(logf_all), Tp if Tp <= 2 * CUMSUM_TILE else _pick_tile(Tp, CUMSUM_TILE))
            yb = _attention(seq(q), kb_all, vb_all, (cq[..., Tk - T:Tk], ck), pos0=pos0, kv_len=Tk, fox=True)
            mix = (g[3], wts["w_out_even"][e], [flat(ya), flat(yb)])
            even.append((seq(logf1)[..., :FOX_HEADS], h_last.reshape(bsz, w), nbuf))
        else:
            o = l // 2
            lam_init = 0.8 - 0.6 * math.exp(-0.3 * l)
            if rope is None:
                tabs = _rope_tables(T, pos0)
                if T % tm:
                    tabs = [jnp.tile(t, (n // T, 1)) for t in tabs]
                rope = (tabs, tabs[0].shape[0] // tm)
            q, *diff_kv, kb, vx = _inproj_odd(x2, g[2], wts["w_in_odd"][o], rope[0], rope[1], tm, o, diff_kv,
                                              n_odd, vt_tiles_per_seq)
            if caches is None:
                kb_all = seq(kb)
                vb_all = vx if vt_tiles_per_seq else seq(vx)
            else:
                kb_all, vb_all = _history_kv(caches[5], caches[6], o, seq(kb), seq(vx), n_keys + key_pad)
            y = _attention(seq(q), kb_all, vb_all, (wts["diff_lambda"][o], wts["diff_subln_g"][o]),
                           pos0=pos0, kv_len=n_keys, fox=False, lam_init=lam_init)
            mix = (g[3], wts["w_out_odd"][o], [flat(y)])
        x2 = _ffn(x2, g[4], g[5], wts["w_ffn_in"], wts["w_ffn_out"], (l, 1), tm, mix)
    per_seq = lambda a: a.reshape(a.shape[0], bsz, T, *a.shape[2:])
    even_out = [per_seq(a) for a in fox_kv] + [jnp.stack(z) for z in zip(*even)]
    return x2.reshape(bsz, T, d), even_out, [per_seq(a) for a in diff_kv]


def _prepare_weights(norm_g, w_ffn_in, w_ffn_out, w_in_even, b_fox_f, lru_conv_w, lru_conv_b, lru_w_gates,
                     lru_b_gates, lru_lambda, w_out_even, w_in_odd, diff_lambda, diff_subln_g, w_out_odd):
    n_even, d, _ = w_in_even.shape
    w = d // 2
    blk = w // LRU_BLOCKS
    wf = jnp.repeat(w_in_even[:, :, 5 * w:], HALF, axis=-1)
    pad_lanes = lambda a: jnp.pad(a, [(0, 0)] * (a.ndim - 1) + [(0, LANES - a.shape[-1])])
    eye = jnp.eye(LRU_BLOCKS, dtype=F32)
    dense = lambda g: jnp.einsum("encg,nm->encmg", g, eye).reshape(n_even, w, w)
    return dict(
        norm_g=norm_g, w_ffn_in=w_ffn_in.astype(BF16), w_ffn_out=w_ffn_out.astype(BF16),
        w_in_even=jnp.concatenate([w_in_even[:, :, :5 * w], wf, pad_lanes(w_in_even[:, :, 5 * w:])],
                                  axis=-1).astype(BF16),
        b_fox_f=jnp.concatenate([jnp.repeat(b_fox_f, HALF, axis=-1), pad_lanes(b_fox_f)],
                                axis=-1).reshape(n_even, 1, w + LANES),
        lru_conv_w=lru_conv_w, lru_conv_b=lru_conv_b.reshape(n_even, 1, w),
        lru_wr=dense(lru_w_gates[..., :blk]).astype(BF16), lru_wi=dense(lru_w_gates[..., blk:]).astype(BF16),
        lru_br=lru_b_gates[..., :blk].reshape(n_even, 1, w), lru_bi=lru_b_gates[..., blk:].reshape(n_even, 1, w),
        lru_lambda=lru_lambda.reshape(n_even, 1, w), w_out_even=w_out_even.astype(BF16),
        w_in_odd=w_in_odd.astype(BF16), diff_lambda=diff_lambda, diff_subln_g=diff_subln_g.reshape(-1, 1, LANES),
        w_out_odd=w_out_odd.astype(BF16))


def kernel(x_prompt, x_sample, cache_fox_k, cache_fox_v, cache_fox_logf, state_lru_h, state_lru_conv, cache_diff_k, cache_diff_v, norm_g, w_ffn_in, w_ffn_out, w_in_even, b_fox_f, lru_conv_w, lru_conv_b, lru_w_gates, lru_b_gates, lru_lambda, w_out_even, w_in_odd, diff_lambda, diff_subln_g, w_out_odd):
    wts = _prepare_weights(norm_g, w_ffn_in, w_ffn_out, w_in_even, b_fox_f, lru_conv_w, lru_conv_b, lru_w_gates,
                           lru_b_gates, lru_lambda, w_out_even, w_in_odd, diff_lambda, diff_subln_g, w_out_odd)
    y_p, pe, po = _trunk(x_prompt, 0, None, wts, _pick_tile(x_prompt.shape[0] * x_prompt.shape[1], 512))
    caches = (cache_fox_k, cache_fox_v, cache_fox_logf, state_lru_h, state_lru_conv, cache_diff_k, cache_diff_v)
    y_s, se, so = _trunk(x_sample, cache_fox_k.shape[2], caches, wts,
                         _pick_tile(x_sample.shape[0] * x_sample.shape[1], 256))
    return (y_p, y_s, *pe, *po, *se, *so)
```

```python
import functools
import math

import jax
import jax.numpy as jnp
from jax import lax
from jax.experimental import pallas as pl
from jax.experimental.pallas import tpu as pltpu

F32 = jnp.float32
BF16 = jnp.bfloat16

EPS = 1e-6
NEG_INF = -1e30
LRU_C = 8.0
LRU_BLOCKS = 8
CONV_W = 4
FOX_HEADS = 8
DIFF_HEADS = 8
CHUNK = 64
ROT_DIM = 16
ROPE_THETA = 500000.0

LANES = 128
SUBLANES = 8
HALF = LANES // 2
FF_CHUNK = 256
AHEAD = 2
ATTN_TQ, ATTN_TK = 512, 512
CUMSUM_TILE = 1024
LOG2E = 1.0 / math.log(2.0)
Q_SCALE = HALF ** -0.5 * LOG2E
VMEM_LIMIT = 56 * 1024 * 1024


def _params(*sem):
    return pltpu.CompilerParams(dimension_semantics=sem, vmem_limit_bytes=VMEM_LIMIT)


def _resident(shape, lead=()):
    nd = len(shape)
    return pl.BlockSpec((None,) * len(lead) + tuple(shape), lambda *_: tuple(lead) + (0,) * nd,
                        pipeline_mode=pl.Buffered(1))


def _rms(x, g):
    return x * lax.rsqrt(jnp.mean(x * x, axis=-1, keepdims=True) + EPS) * g


def _mm(a, b):
    return jnp.dot(a, b, preferred_element_type=F32)


def _mm_nt(a, b):
    return lax.dot_general(a, b, (((1,), (1,)), ((), ())), preferred_element_type=F32)


def _log_sigmoid(x):
    return jnp.minimum(x, 0.0) - jnp.log1p(jnp.exp(-jnp.abs(x)))


def _softplus(x):
    return jnp.maximum(x, 0.0) + jnp.log1p(jnp.exp(-jnp.abs(x)))


def _pick_tile(n, target):
    t = min(n, target)
    while n % t:
        t //= 2
    return t


def _ffn_kernel(x_ref, gpre_ref, gpost_ref, wi_ref, wo_ref, *rest, dff):
    *mix, o_ref, act_ref = rest
    x = x_ref[...]
    if mix:
        gmix_ref, wmix_ref, *y_refs = mix
        y = jnp.concatenate([r[...] for r in y_refs], axis=1)
        x = x + _rms(_mm(y, wmix_ref[...]), gmix_ref[...])
    xn = _rms(x, gpre_ref[...]).astype(BF16)
    for c0 in range(0, dff, FF_CHUNK):
        g = _mm(xn, wi_ref[:, c0:c0 + FF_CHUNK])
        u = _mm(xn, wi_ref[:, dff + c0:dff + c0 + FF_CHUNK])
        act_ref[:, c0:c0 + FF_CHUNK] = ((g * jax.nn.sigmoid(g)) * u).astype(BF16)
    y = _mm(act_ref[...], wo_ref[...])
    o_ref[...] = x + 0.5 * _rms(y, gpost_ref[...])


def _ffn(x2, g_pre, g_post, w_in, w_out, which, tm, mix=None):
    n, d = x2.shape
    dff = w_out.shape[-2]
    assert dff % FF_CHUNK == 0 and n % tm == 0
    row = lambda c: pl.BlockSpec((tm, c), lambda i: (i, 0))
    mix_in, mix_specs = [], []
    if mix is not None:
        g_mix, w_mix, ys = mix
        mix_in = [g_mix, w_mix, *ys]
        mix_specs = [_resident((1, d)), _resident(w_mix.shape)] + [row(y.shape[1]) for y in ys]
    return pl.pallas_call(
        functools.partial(_ffn_kernel, dff=dff),
        grid=(n // tm,),
        in_specs=[row(d), _resident((1, d)), _resident((1, d)), _resident((d, 2 * dff), which),
                  _resident((dff, d), which)] + mix_specs,
        out_specs=row(d),
        out_shape=jax.ShapeDtypeStruct((n, d), F32),
        scratch_shapes=[pltpu.VMEM((tm, dff), BF16)],
        compiler_params=_params("parallel"),
        name="ffn_mix" if mix is not None else "ffn",
    )(x2, g_pre, g_post, w_in, w_out, *mix_in)


def _emit_kv(k, v, k_ref, v_ref, kb_ref, vx_ref, heads, vt_tiles):
    tm, w = k.shape
    for l in range(k_ref.shape[0]):
        k_ref[l] = k.reshape(tm, heads, w // heads) if l == 0 else jnp.zeros(k_ref.shape[1:], F32)
        v_ref[l] = v.reshape(tm, heads, w // heads) if l == 0 else jnp.zeros(v_ref.shape[1:], F32)
    kb_ref[...] = k.astype(BF16)
    if vt_tiles:
        vx_ref[0, :, 0] = v.T.astype(BF16).reshape(w // LANES, LANES, tm)
    else:
        vx_ref[...] = v.astype(BF16)


def _kv_out(n, w, heads, tm, layer, stacks, n_layers, vt_tiles_per_seq):
    dh = w // heads
    assert (stacks is None) == (layer == 0)
    stacked = pl.BlockSpec((n_layers if stacks is None else 1, tm, heads, dh), lambda i: (layer, i, 0, 0))
    stacked_shape = jax.ShapeDtypeStruct((n_layers, n, heads, dh), F32)
    row = pl.BlockSpec((tm, w), lambda i: (i, 0))
    if vt_tiles_per_seq:
        nkb = vt_tiles_per_seq
        vx = pl.BlockSpec((1, w // LANES, 1, LANES, tm), lambda i: (i // nkb, 0, i % nkb, 0, 0))
        vx_shape = jax.ShapeDtypeStruct((n // (nkb * tm), w // LANES, nkb, LANES, tm), BF16)
    else:
        vx, vx_shape = row, jax.ShapeDtypeStruct((n, w), BF16)
    specs = [stacked, stacked, row, vx]
    shapes = [stacked_shape, stacked_shape, jax.ShapeDtypeStruct((n, w), BF16), vx_shape]
    extra_in = [] if stacks is None else list(stacks)
    return specs, shapes, extra_in


def _inproj_even_kernel(*refs, w, vt_tiles):
    x_ref, g_ref, w_ref, bf_ref = refs[:4]
    u_ref, gate_ref, q_ref, logf_ref, logf1_ref, k_ref, v_ref, kb_ref, vx_ref = refs[-9:]
    hn = _rms(x_ref[...], g_ref[...]).astype(BF16)
    u_ref[...] = _mm(hn, w_ref[:, 0:w])
    gate_ref[...] = _mm(hn, w_ref[:, w:2 * w])
    q_ref[...] = (_mm(hn, w_ref[:, 2 * w:3 * w]) * Q_SCALE).astype(BF16)
    logf_ref[...] = _log_sigmoid(_mm(hn, w_ref[:, 5 * w:6 * w]) + bf_ref[:, 0:w])
    logf1_ref[...] = _log_sigmoid(_mm(hn, w_ref[:, 6 * w:6 * w + LANES]) + bf_ref[:, w:w + LANES])
    _emit_kv(_mm(hn, w_ref[:, 3 * w:4 * w]), _mm(hn, w_ref[:, 4 * w:5 * w]), k_ref, v_ref, kb_ref, vx_ref,
             FOX_HEADS, vt_tiles)


def _inproj_even(x2, g, w_wide, bf_wide, tm, layer, stacks, n_layers, vt_tiles_per_seq):
    n, d = x2.shape
    w = (w_wide.shape[1] - LANES) // 6
    row = lambda c: pl.BlockSpec((tm, c), lambda i: (i, 0))
    f = jax.ShapeDtypeStruct((n, w), F32)
    b = jax.ShapeDtypeStruct((n, w), BF16)
    kv_specs, kv_shapes, extra_in = _kv_out(n, w, FOX_HEADS, tm, layer, stacks, n_layers, vt_tiles_per_seq)
    return pl.pallas_call(
        functools.partial(_inproj_even_kernel, w=w, vt_tiles=bool(vt_tiles_per_seq)),
        grid=(n // tm,),
        in_specs=[row(d), _resident((1, d)), _resident(w_wide.shape), _resident(bf_wide.shape)]
        + [pl.BlockSpec(memory_space=pl.ANY)] * len(extra_in),
        out_specs=[row(w)] * 4 + [row(LANES)] + kv_specs,
        out_shape=[f, f, b, f, jax.ShapeDtypeStruct((n, LANES), F32)] + kv_shapes,
        input_output_aliases={4 + a: 5 + a for a in range(len(extra_in))},
        compiler_params=_params("parallel"),
        name="inproj_even",
    )(x2, g, w_wide, bf_wide, *extra_in)


def _scan_rows(a, b, h):
    tt, c = b.shape
    assert tt % SUBLANES == 0
    groups = (tt // SUBLANES, SUBLANES, c)
    b = b.reshape(groups)
    a = None if a is None else a.reshape(groups)
    sub = lax.broadcasted_iota(jnp.int32, groups, 1)
    d = 1
    while d < SUBLANES:
        keep = sub >= d
        if a is None:
            b = b + jnp.where(keep, pltpu.roll(b, d, 1), 0.0)
        else:
            b = a * jnp.where(keep, pltpu.roll(b, d, 1), 0.0) + b
            a = a * jnp.where(keep, pltpu.roll(a, d, 1), 1.0)
        d *= 2
    out = []
    for g in range(groups[0]):
        blk = b[g] + (h if a is None else a[g] * h)
        out.append(blk)
        h = blk[SUBLANES - 1:SUBLANES]
    return jnp.concatenate(out, axis=0)


def _lru_kernel(u_ref, gate_ref, h0_ref, buf0_ref, cw_ref, cb_ref, wr_ref, wi_ref, br_ref, bi_ref, lam_ref,
                ya_ref, hlast_ref, nbuf_ref, ubuf, hcar, *, tt):
    t = pl.program_id(1)
    pad = ubuf.shape[0] - tt

    @pl.when(t == 0)
    def _():
        ubuf[0:pad, :] = buf0_ref[0]
        hcar[...] = h0_ref[0]

    u = u_ref[0]
    ubuf[pad:pad + tt, :] = u
    uc = cb_ref[...]
    for j in range(CONV_W):
        o = pad - (CONV_W - 1) + j
        uc = uc + ubuf[o:o + tt, :] * cw_ref[j:j + 1, :]
    ucb = uc.astype(BF16)
    r = jax.nn.sigmoid(_mm(ucb, wr_ref[...]) + br_ref[...])
    ig = jax.nn.sigmoid(_mm(ucb, wi_ref[...]) + bi_ref[...])
    log_a = (-LRU_C * r) * _softplus(-lam_ref[...])
    a = jnp.exp(log_a)
    bx = jnp.sqrt(-jnp.tanh(log_a) * (a * a + 1.0)) * (ig * uc)
    bx = _scan_rows(a, bx, hcar[...])
    hcar[...] = bx[tt - 1:tt, :]
    ya_ref[0] = (bx * jax.nn.gelu(gate_ref[0])).astype(BF16)
    ubuf[0:pad, :] = ubuf[tt:tt + pad, :]

    @pl.when(t == pl.num_programs(1) - 1)
    def _():
        hlast_ref[0] = bx[tt - 1:tt, :]
        nbuf_ref[0] = ubuf[pad - (CONV_W - 1):pad, :]


def _lru(u, gate, h0, buf0, cw, cb, wr, wi, br, bi, lam, tt):
    bsz, T, w = u.shape
    pad = 8
    seq = pl.BlockSpec((1, tt, w), lambda b, t: (b, t, 0))
    per_b = lambda r: pl.BlockSpec((1, r, w), lambda b, t: (b, 0, 0))
    return pl.pallas_call(
        functools.partial(_lru_kernel, tt=tt),
        grid=(bsz, T // tt),
        in_specs=[seq, seq, per_b(1), per_b(pad), _resident((CONV_W, w)), _resident((1, w)), _resident((w, w)),
                  _resident((w, w)), _resident((1, w)), _resident((1, w)), _resident((1, w))],
        out_specs=[seq, per_b(1), per_b(CONV_W - 1)],
        out_shape=[jax.ShapeDtypeStruct((bsz, T, w), BF16), jax.ShapeDtypeStruct((bsz, 1, w), F32),
                   jax.ShapeDtypeStruct((bsz, CONV_W - 1, w), F32)],
        scratch_shapes=[pltpu.VMEM((tt + pad, w), F32), pltpu.VMEM((1, w), F32)],
        compiler_params=_params("parallel", "arbitrary"),
        name="conv_rglru",
    )(u, gate, h0, buf0, cw, cb, wr, wi, br, bi, lam)


def _cumsum_kernel(x_ref, ck_ref, cq_ref, car, *, tt):
    @pl.when(pl.program_id(1) == 0)
    def _():
        car[...] = jnp.zeros_like(car)

    x = _scan_rows(None, x_ref[0], car[...])
    car[...] = x[tt - 1:tt, :]
    x = x * LOG2E
    lo = lax.broadcasted_iota(jnp.int32, (1, LANES), 1) < HALF
    for s in range(x.shape[1] // LANES):
        xs = x[:, s * LANES:(s + 1) * LANES]
        sw = pltpu.roll(xs, HALF, 1)
        ck_ref[0, :, (2 * s) * LANES:(2 * s + 1) * LANES] = jnp.where(lo, xs, sw)
        ck_ref[0, :, (2 * s + 1) * LANES:(2 * s + 2) * LANES] = jnp.where(lo, sw, xs)
        tr = xs.T
        cq_ref[0, s, 0:1, :] = tr[0:1, :]
        cq_ref[0, s, 1:2, :] = tr[HALF:HALF + 1, :]


def _cumsum(x, tt):
    bsz, T, w = x.shape
    ns = w // LANES
    return pl.pallas_call(
        functools.partial(_cumsum_kernel, tt=tt),
        grid=(bsz, T // tt),
        in_specs=[pl.BlockSpec((1, tt, w), lambda b, t: (b, t, 0))],
        out_specs=[pl.BlockSpec((1, tt, 2 * w), lambda b, t: (b, t, 0)),
                   pl.BlockSpec((1, ns, 2, tt), lambda b, t: (b, 0, 0, t))],
        out_shape=[jax.ShapeDtypeStruct((bsz, T, 2 * w), F32), jax.ShapeDtypeStruct((bsz, ns, 2, T), F32)],
        scratch_shapes=[pltpu.VMEM((1, w), F32)],
        compiler_params=_params("parallel", "arbitrary"),
        name="logf_cumsum",
    )(x)


def _attn_kernel(*refs, tq, tk, nk_total, kv_len, pos0, fox, lam_init):
    if fox:
        q_ref, k_ref, vt_ref, cq_ref, ck_ref, o_ref, m_sc, l_sc, acc_sc, s_sc = refs
    else:
        q_ref, k_ref, vt_ref, lp_ref, sg_ref, o_ref, m_sc, l_sc, acc_sc, s_sc = refs
    i = pl.program_id(2)
    q0 = pos0 + i * tq
    shift = CHUNK.bit_length() - 1
    lo = lax.broadcasted_iota(jnp.int32, (1, LANES), 1) < HALF
    spb = q_ref.shape[2] // LANES
    qm = []
    for sl in range(spb):
        q = q_ref[0, :, sl * LANES:(sl + 1) * LANES]
        qm += [jnp.where(lo, q, jnp.zeros_like(q)), jnp.where(lo, jnp.zeros_like(q), q)]
    if fox:
        n_full = (q0 + 1) // tk
        n_need = (q0 + tq - 1) // tk + 1
    else:
        n_full = (((q0 >> shift) + 1) << shift) // tk
        n_need = (((((q0 + tq - 1) >> shift) + 1) << shift) + tk - 1) // tk
    n_need = jnp.minimum(n_need, nk_total)
    n_full = jnp.minimum(jnp.minimum(n_full, kv_len // tk), n_need)

    m_sc[...] = jnp.full_like(m_sc, NEG_INF)
    l_sc[...] = jnp.zeros_like(l_sc)
    acc_sc[...] = jnp.zeros_like(acc_sc)

    n_maps = 2 * spb

    def scores(g, j):
        k0 = j * tk if isinstance(j, int) else pl.multiple_of(j * tk, tk)
        lanes = slice((g // 2) * LANES, (g // 2 + 1) * LANES)
        hk = tk // 2
        return jnp.concatenate([_mm_nt(k_ref[0, pl.ds(k0 + r, hk), lanes], qm[g]) for r in (0, hk)], axis=0)

    assert AHEAD < n_maps
    for a in range(AHEAD):
        s_sc[a] = scores(a, 0)

    hq, hk = tq // 2, tk // 2
    diag = (tq == tk and pos0 % tk == 0 and kv_len == nk_total * tk and hq % LANES == 0 and hk % CHUNK == 0)

    def right_half(full, extra, op):
        return jnp.concatenate([full[:, :hq], op(full[:, hq:], extra)], axis=1)

    def tile(j, masked):
        k0 = pl.multiple_of(j * tk, tk)
        split = masked and diag
        if split:
            kr = lax.broadcasted_iota(jnp.int32, (hk, hq), 0)
            qr = lax.broadcasted_iota(jnp.int32, (hk, hq), 1)
            tri = (qr >= kr) if fox else ((qr >> shift) >= (kr >> shift))
        elif masked:
            kpos = k0 + lax.broadcasted_iota(jnp.int32, (tk, tq), 0)
            qpos = q0 + lax.broadcasted_iota(jnp.int32, (tk, tq), 1)
            vis = (qpos >= kpos) if fox else ((qpos >> shift) >= (kpos >> shift))
            if kv_len < nk_total * tk:
                vis = vis & (kpos < kv_len)
        for g in range(n_maps):
            sl, c = divmod(g, 2)
            dj, g_ahead = divmod(g + AHEAD, n_maps)
            s_sc[g_ahead] = scores(g_ahead, jnp.minimum(j + dj, nk_total - 1) if dj else j)
            ck = ck_ref[0, pl.ds(k0, tk), g * LANES:(g + 1) * LANES] if fox else None

            def biased(rows, cols):
                blk = s_sc[g, rows, cols]
                return blk - jnp.concatenate([ck[rows]] * (blk.shape[1] // LANES), axis=1) if fox else blk

            if split:
                s = biased(slice(0, hk), slice(0, tq))
                s = jnp.concatenate([jnp.where(tri, s[:, :hq], NEG_INF), s[:, hq:]], axis=1)
                s_late = jnp.where(tri, biased(slice(hk, tk), slice(hq, tq)), NEG_INF)
            else:
                s = biased(slice(0, tk), slice(0, tq))
                if masked:
                    s = jnp.where(vis, s, NEG_INF)
            m_prev = m_sc[g]
            m_tile = jnp.max(s, axis=0, keepdims=True)
            if split:
                m_tile = right_half(m_tile, jnp.max(s_late, axis=0, keepdims=True), jnp.maximum)
            if fox:
                cq = cq_ref[0, sl, c:c + 1, :]
                m_new = jnp.maximum(m_prev, m_tile + cq)
                m_sub = m_new - cq
            else:
                m_new = m_sub = jnp.maximum(m_prev, m_tile)
            p = jnp.exp2(s - m_sub)
            p_sum = jnp.sum(p, axis=0, keepdims=True)
            alpha = jnp.exp2(m_prev - m_new)
            vt = vt_ref[0, sl, j]
            r = slice(c * HALF, (c + 1) * HALF) if fox else slice(0, LANES)
            if split:
                p_late = jnp.exp2(s_late - m_sub[:, hq:])
                p_sum = right_half(p_sum, jnp.sum(p_late, axis=0, keepdims=True), jnp.add)
                pv = right_half(_mm(vt[r, :hk], p.astype(BF16)), _mm(vt[r, hk:], p_late.astype(BF16)), jnp.add)
            else:
                pv = _mm(vt[r, :], p.astype(BF16))
            l_sc[g] = alpha * l_sc[g] + p_sum
            m_sc[g] = m_new
            if fox:
                acc_sc[sl, r, :] = alpha * acc_sc[sl, r, :] + pv
            else:
                acc_sc[g] = alpha * acc_sc[g] + pv

    def full_body(j, carry):
        tile(j, False)
        return carry

    def masked_body(j, carry):
        tile(j, True)
        return carry

    lax.fori_loop(0, n_full, full_body, 0)
    lax.fori_loop(n_full, n_need, masked_body, 0)

    for sl in range(spb):
        a, b = 2 * sl, 2 * sl + 1
        if fox:
            ot = jnp.concatenate([acc_sc[sl, 0:HALF, :] / l_sc[a], acc_sc[sl, HALF:LANES, :] / l_sc[b]], axis=0)
            o = ot.T
        else:
            lp = lp_ref[...]
            lam = (jnp.exp(jnp.sum(lp[0:1] * lp[1:2], axis=-1, keepdims=True))
                   - jnp.exp(jnp.sum(lp[2:3] * lp[3:4], axis=-1, keepdims=True)) + lam_init)
            ot = acc_sc[a] / l_sc[a] - lam * (acc_sc[b] / l_sc[b])
            ot = ot * lax.rsqrt(jnp.mean(ot * ot, axis=0, keepdims=True) + EPS)
            o = (ot.T * sg_ref[...]) * (1.0 - lam_init)
        o_ref[0, :, sl * LANES:(sl + 1) * LANES] = o.astype(BF16)


def _attention(q, k, v, extra, *, pos0, kv_len, fox, lam_init=0.0, tq=ATTN_TQ, tk=ATTN_TK, spb=4):
    bsz, Tq, w = q.shape
    Tk = k.shape[1]
    ns = w // LANES
    Tq_pad = -(-Tq // LANES) * LANES
    tq = _pick_tile(Tq_pad, tq)
    tk = tk if Tk % tk == 0 else Tk
    nk = Tk // tk
    q = jnp.pad(q, ((0, 0), (0, Tq_pad - Tq), (0, 0)))
    vt = v if v.ndim == 5 else v.reshape(bsz, nk, tk, ns, LANES).transpose(0, 3, 1, 4, 2)
    assert vt.shape == (bsz, ns, nk, LANES, tk)
    assert ns % spb == 0
    sw = spb * LANES
    qspec = pl.BlockSpec((1, tq, sw), lambda b, s, i: (b, i, s))
    kspec = pl.BlockSpec((1, Tk, sw), lambda b, s, i: (b, 0, s))
    vspec = pl.BlockSpec((1, spb, nk, LANES, tk), lambda b, s, i: (b, s, 0, 0, 0))
    if fox:
        cq, ck = extra
        extra = (jnp.pad(cq, ((0, 0), (0, 0), (0, 0), (0, Tq_pad - Tq))), ck)
        especs = [pl.BlockSpec((1, spb, 2, tq), lambda b, s, i: (b, s, 0, i)),
                  pl.BlockSpec((1, Tk, 2 * sw), lambda b, s, i: (b, 0, s), pipeline_mode=pl.Buffered(1))]
    else:
        lp, sg = extra
        especs = [pl.BlockSpec(lp.shape, lambda b, s, i: (0, 0)), pl.BlockSpec(sg.shape, lambda b, s, i: (0, 0))]
    out = pl.pallas_call(
        functools.partial(_attn_kernel, tq=tq, tk=tk, nk_total=nk, kv_len=kv_len, pos0=pos0, fox=fox,
                          lam_init=lam_init),
        grid=(bsz, ns // spb, Tq_pad // tq),
        in_specs=[qspec, kspec, vspec] + especs,
        out_specs=qspec,
        out_shape=jax.ShapeDtypeStruct((bsz, Tq_pad, w), BF16),
        scratch_shapes=[pltpu.VMEM((2 * spb, 1, tq), F32), pltpu.VMEM((2 * spb, 1, tq), F32),
                        pltpu.VMEM((spb if fox else 2 * spb, LANES, tq), F32), pltpu.VMEM((2 * spb, tk, tq), F32)],
        compiler_params=_params("parallel", "parallel", "arbitrary"),
        name="fox_attention" if fox else "diff_attention",
    )(q, k, vt, *extra)
    return out[:, :Tq]


def _rope_table_kernel(inv_ref, c_ref, s1_ref, s2_ref, *, pos0):
    T = c_ref.shape[0]
    pos = (pos0 + lax.broadcasted_iota(jnp.int32, (T, LANES), 0)).astype(F32)
    ang = pos * inv_ref[...]
    lane = lax.broadcasted_iota(jnp.int32, (T, LANES), 1) % HALF
    sn = jnp.sin(ang)
    c_ref[...] = jnp.cos(ang)
    s1_ref[...] = jnp.where(lane < ROT_DIM // 2, -sn, 0.0)
    s2_ref[...] = jnp.where(lane >= ROT_DIM // 2, sn, 0.0)


def _rope_tables(T, pos0):
    half = ROT_DIM // 2
    inv = ROPE_THETA ** (-jnp.arange(half, dtype=F32) * (2.0 / ROT_DIM))
    lane = jnp.arange(LANES) % HALF
    inv_lane = jnp.where(lane < ROT_DIM, inv[lane % half], 0.0).reshape(1, LANES).astype(F32)
    tab = jax.ShapeDtypeStruct((T, LANES), F32)
    return pl.pallas_call(
        functools.partial(_rope_table_kernel, pos0=pos0),
        out_shape=[tab, tab, tab],
        name="rope_tables",
    )(inv_lane)


def _inproj_odd_kernel(*refs, w, vt_tiles):
    x_ref, g_ref, w_ref, c_ref, s1_ref, s2_ref = refs[:6]
    q_ref, k_ref, v_ref, kb_ref, vx_ref = refs[-5:]
    hn = _rms(x_ref[...], g_ref[...]).astype(BF16)
    cs, s1, s2 = c_ref[...], s1_ref[...], s2_ref[...]
    half = ROT_DIM // 2

    def rope(t):
        out = []
        for h in range(w // LANES):
            s = t[:, h * LANES:(h + 1) * LANES]
            out.append(s * cs + pltpu.roll(s, LANES - half, 1) * s1 + pltpu.roll(s, half, 1) * s2)
        return jnp.concatenate(out, axis=1)

    q_ref[...] = (rope(_mm(hn, w_ref[:, 0:w])) * Q_SCALE).astype(BF16)
    _emit_kv(rope(_mm(hn, w_ref[:, w:2 * w])), _mm(hn, w_ref[:, 2 * w:3 * w]), k_ref, v_ref, kb_ref, vx_ref,
             DIFF_HEADS, vt_tiles)


def _inproj_odd(x2, g, w_in, tabs, tab_blocks, tm, layer, stacks, n_layers, vt_tiles_per_seq):
    n, d = x2.shape
    w = w_in.shape[1] // 3
    row = lambda c: pl.BlockSpec((tm, c), lambda i: (i, 0))
    tab = pl.BlockSpec((tm, LANES), lambda i: (i % tab_blocks, 0))
    kv_specs, kv_shapes, extra_in = _kv_out(n, w, DIFF_HEADS, tm, layer, stacks, n_layers, vt_tiles_per_seq)
    return pl.pallas_call(
        functools.partial(_inproj_odd_kernel, w=w, vt_tiles=bool(vt_tiles_per_seq)),
        grid=(n // tm,),
        in_specs=[row(d), _resident((1, d)), _resident((d, 3 * w)), tab, tab, tab]
        + [pl.BlockSpec(memory_space=pl.ANY)] * len(extra_in),
        out_specs=[row(w)] + kv_specs,
        out_shape=[jax.ShapeDtypeStruct((n, w), BF16)] + kv_shapes,
        input_output_aliases={6 + a: 1 + a for a in range(len(extra_in))},
        compiler_params=_params("parallel"),
        name="inproj_odd",
    )(x2, g, w_in, *tabs, *extra_in)


def _history_kv_kernel(ck_ref, cv_ref, kn_ref, vn_ref, k_ref, vt_ref, v_sc):
    past, w = ck_ref.shape[2], k_ref.shape[2]
    new = kn_ref.shape[1]
    tp = k_ref.shape[1]
    k_ref[0, 0:past, :] = ck_ref[0, 0].reshape(past, w).astype(BF16)
    k_ref[0, past:past + new, :] = kn_ref[0]
    k_ref[0, past + new:tp, :] = jnp.zeros((tp - past - new, w), BF16)
    v_sc[0:past, :] = cv_ref[0, 0].reshape(past, w)
    v_sc[past:past + new, :] = vn_ref[0].astype(F32)
    v_sc[past + new:tp, :] = jnp.zeros((tp - past - new, w), F32)
    vt_ref[0, :, 0] = v_sc[...].T.astype(BF16).reshape(w // LANES, LANES, tp)


def _history_kv(cache_k, cache_v, layer, k_new, v_new, tp):
    _, bsz, past, heads, dh = cache_k.shape
    T, w = k_new.shape[1:]
    cspec = pl.BlockSpec((1, 1, past, heads, dh), lambda b: (layer, b, 0, 0, 0))
    nspec = pl.BlockSpec((1, T, w), lambda b: (b, 0, 0))
    return pl.pallas_call(
        _history_kv_kernel,
        grid=(bsz,),
        in_specs=[cspec, cspec, nspec, nspec],
        out_specs=[pl.BlockSpec((1, tp, w), lambda b: (b, 0, 0)),
                   pl.BlockSpec((1, w // LANES, 1, LANES, tp), lambda b: (b, 0, 0, 0, 0))],
        out_shape=[jax.ShapeDtypeStruct((bsz, tp, w), BF16),
                   jax.ShapeDtypeStruct((bsz, w // LANES, 1, LANES, tp), BF16)],
        scratch_shapes=[pltpu.VMEM((tp, w), F32)],
        compiler_params=_params("parallel"),
        name="history_kv",
    )(cache_k, cache_v, k_new, v_new)


def _trunk(x, pos0, caches, wts, tm):
    bsz, T, d = x.shape
    n = bsz * T
    depth = wts["norm_g"].shape[0]
    w = d // 2
    x2 = x.reshape(n, d)
    tt = _pick_tile(T, 256)
    flat = lambda a: a.reshape(n, a.shape[-1])
    seq = lambda a: a.reshape(bsz, T, a.shape[-1])
    n_keys = T if caches is None else T + caches[0].shape[2]
    key_pad = -n_keys % LANES
    pad_keys = lambda a: jnp.pad(a, ((0, 0), (0, key_pad), (0, 0))) if key_pad else a
    rope = None
    n_even, n_odd = (depth + 1) // 2, depth // 2
    vt_tiles_per_seq = T // tm if caches is None and tm == ATTN_TK and T % tm == 0 else 0
    fox_kv = diff_kv = None
    even, odd = [], []
    for l in range(depth):
        g = wts["norm_g"][l].reshape(6, 1, d)
        x2 = _ffn(x2, g[0], g[1], wts["w_ffn_in"], wts["w_ffn_out"], (l, 0), tm)
        if l % 2 == 0:
            e = l // 2
            u, gate, q, logf, logf1, *fox_kv, kb, vx = _inproj_even(
                x2, g[2], wts["w_in_even"][e], wts["b_fox_f"][e], tm, e, fox_kv, n_even, vt_tiles_per_seq)
            if caches is None:
                h0 = jnp.zeros((bsz, 1, w), F32)
                buf0 = jnp.zeros((bsz, 8, w), F32)
                kb_all, logf_all = seq(kb), seq(logf)
                vb_all = vx if vt_tiles_per_seq else seq(vx)
            else:
                flogf, lru_h, lru_conv = (c[e] for c in caches[2:5])
                h0 = lru_h.reshape(bsz, 1, w)
                buf0 = jnp.pad(lru_conv, ((0, 0), (8 - (CONV_W - 1), 0), (0, 0)))
                kb_all, vb_all = _history_kv(caches[0], caches[1], e, seq(kb), seq(vx), n_keys + key_pad)
                logf_all = jnp.concatenate([jnp.repeat(flogf, HALF, axis=-1), seq(logf)], axis=1)
            ya, h_last, nbuf = _lru(seq(u), seq(gate), h0, buf0, wts["lru_conv_w"][e], wts["lru_conv_b"][e],
                                    wts["lru_wr"][e], wts["lru_wi"][e], wts["lru_br"][e], wts["lru_bi"][e],
                                    wts["lru_lambda"][e], tt)
            Tk = logf_all.shape[1]
            Tp = Tk + key_pad
            ck, cq = _cumsum(pad_keys(logf_all), Tp if Tp <= 2 * CUMSUM_TILE else _pick_tile(Tp, CUMSUM_TILE))
            yb = _attention(seq(q), kb_all, vb_all, (cq[..., Tk - T:Tk], ck), pos0=pos0, kv_len=Tk, fox=True)
            mix = (g[3], wts["w_out_even"][e], [flat(ya), flat(yb)])
            even.append((seq(logf1)[..., :FOX_HEADS], h_last.reshape(bsz, w), nbuf))
        else:
            o = l // 2
            lam_init = 0.8 - 0.6 * math.exp(-0.3 * l)
            if rope is None:
                tabs = _rope_tables(T, pos0)
                if T % tm:
                    tabs = [jnp.tile(t, (n // T, 1)) for t in tabs]
                rope = (tabs, tabs[0].shape[0] // tm)
            q, *diff_kv, kb, vx = _inproj_odd(x2, g[2], wts["w_in_odd"][o], rope[0], rope[1], tm, o, diff_kv,
                                              n_odd, vt_tiles_per_seq)
            if caches is None:
                kb_all = seq(kb)
                vb_all = vx if vt_tiles_per_seq else seq(vx)
            else:
                kb_all, vb_all = _history_kv(caches[5], caches[6], o, seq(kb), seq(vx), n_keys + key_pad)
            y = _attention(seq(q), kb_all, vb_all, (wts["diff_lambda"][o], wts["diff_subln_g"][o]),
                           pos0=pos0, kv_len=n_keys, fox=False, lam_init=lam_init)
            mix = (g[3], wts["w_out_odd"][o], [flat(y)])
        x2 = _ffn(x2, g[4], g[5], wts["w_ffn_in"], wts["w_ffn_out"], (l, 1), tm, mix)
    per_seq = lambda a: a.reshape(a.shape[0], bsz, T, *a.shape[2:])
    even_out = [per_seq(a) for a in fox_kv] + [jnp.stack(z) for z in zip(*even)]
    return x2.reshape(bsz, T, d), even_out, [per_seq(a) for a in diff_kv]


def _prepare_weights(norm_g, w_ffn_in, w_ffn_out, w_in_even, b_fox_f, lru_conv_w, lru_conv_b, lru_w_gates,
                     lru_b_gates, lru_lambda, w_out_even, w_in_odd, diff_lambda, diff_subln_g, w_out_odd):
    n_even, d, _ = w_in_even.shape
    w = d // 2
    blk = w // LRU_BLOCKS
    wf = jnp.repeat(w_in_even[:, :, 5 * w:], HALF, axis=-1)
    pad_lanes = lambda a: jnp.pad(a, [(0, 0)] * (a.ndim - 1) + [(0, LANES - a.shape[-1])])
    eye = jnp.eye(LRU_BLOCKS, dtype=F32)
    dense = lambda g: jnp.einsum("encg,nm->encmg", g, eye).reshape(n_even, w, w)
    return dict(
        norm_g=norm_g, w_ffn_in=w_ffn_in.astype(BF16), w_ffn_out=w_ffn_out.astype(BF16),
        w_in_even=jnp.concatenate([w_in_even[:, :, :5 * w], wf, pad_lanes(w_in_even[:, :, 5 * w:])],
                                  axis=-1).astype(BF16),
        b_fox_f=jnp.concatenate([jnp.repeat(b_fox_f, HALF, axis=-1), pad_lanes(b_fox_f)],
                                axis=-1).reshape(n_even, 1, w + LANES),
        lru_conv_w=lru_conv_w, lru_conv_b=lru_conv_b.reshape(n_even, 1, w),
        lru_wr=dense(lru_w_gates[..., :blk]).astype(BF16), lru_wi=dense(lru_w_gates[..., blk:]).astype(BF16),
        lru_br=lru_b_gates[..., :blk].reshape(n_even, 1, w), lru_bi=lru_b_gates[..., blk:].reshape(n_even, 1, w),
        lru_lambda=lru_lambda.reshape(n_even, 1, w), w_out_even=w_out_even.astype(BF16),
        w_in_odd=w_in_odd.astype(BF16), diff_lambda=diff_lambda, diff_subln_g=diff_subln_g.reshape(-1, 1, LANES),
        w_out_odd=w_out_odd.astype(BF16))


def kernel(x_prompt, x_sample, cache_fox_k, cache_fox_v, cache_fox_logf, state_lru_h, state_lru_conv, cache_diff_k, cache_diff_v, norm_g, w_ffn_in, w_ffn_out, w_in_even, b_fox_f, lru_conv_w, lru_conv_b, lru_w_gates, lru_b_gates, lru_lambda, w_out_even, w_in_odd, diff_lambda, diff_subln_g, w_out_odd):
    wts = _prepare_weights(norm_g, w_ffn_in, w_ffn_out, w_in_even, b_fox_f, lru_conv_w, lru_conv_b, lru_w_gates,
                           lru_b_gates, lru_lambda, w_out_even, w_in_odd, diff_lambda, diff_subln_g, w_out_odd)
    y_p, pe, po = _trunk(x_prompt, 0, None, wts, _pick_tile(x_prompt.shape[0] * x_prompt.shape[1], 512))
    caches = (cache_fox_k, cache_fox_v, cache_fox_logf, state_lru_h, state_lru_conv, cache_diff_k, cache_diff_v)
    y_s, se, so = _trunk(x_sample, cache_fox_k.shape[2], caches, wts,
                         _pick_tile(x_sample.shape[0] * x_sample.shape[1], 256))
    return (y_p, y_s, *pe, *po, *se, *so)
```

```python
import functools
import math

import jax
import jax.numpy as jnp
from jax import lax
from jax.experimental import pallas as pl
from jax.experimental.pallas import tpu as pltpu

F32 = jnp.float32
BF16 = jnp.bfloat16

EPS = 1e-6
NEG_INF = -1e30
LRU_C = 8.0
LRU_BLOCKS = 8
CONV_W = 4
FOX_HEADS = 8
DIFF_HEADS = 8
CHUNK = 64
ROT_DIM = 16
ROPE_THETA = 500000.0

LANES = 128
SUBLANES = 8
HALF = LANES // 2
FF_CHUNK = 256
AHEAD = 2
ATTN_TQ, ATTN_TK = 512, 512
CUMSUM_TILE = 1024
LOG2E = 1.0 / math.log(2.0)
Q_SCALE = HALF ** -0.5 * LOG2E
VMEM_LIMIT = 56 * 1024 * 1024


def _params(*sem):
    return pltpu.CompilerParams(dimension_semantics=sem, vmem_limit_bytes=VMEM_LIMIT)


def _resident(shape, lead=()):
    nd = len(shape)
    return pl.BlockSpec((None,) * len(lead) + tuple(shape), lambda *_: tuple(lead) + (0,) * nd,
                        pipeline_mode=pl.Buffered(1))


def _rms(x, g):
    return x * lax.rsqrt(jnp.mean(x * x, axis=-1, keepdims=True) + EPS) * g


def _mm(a, b):
    return jnp.dot(a, b, preferred_element_type=F32)


def _mm_nt(a, b):
    return lax.dot_general(a, b, (((1,), (1,)), ((), ())), preferred_element_type=F32)


def _log_sigmoid(x):
    return jnp.minimum(x, 0.0) - jnp.log1p(jnp.exp(-jnp.abs(x)))


def _softplus(x):
    return jnp.maximum(x, 0.0) + jnp.log1p(jnp.exp(-jnp.abs(x)))


def _pick_tile(n, target):
    t = min(n, target)
    while n % t:
        t //= 2
    return t


def _ffn_kernel(x_ref, gpre_ref, gpost_ref, wi_ref, wo_ref, *rest, dff):
    *mix, o_ref, act_ref = rest
    x = x_ref[...]
    if mix:
        gmix_ref, wmix_ref, *y_refs = mix
        y = jnp.concatenate([r[...] for r in y_refs], axis=1)
        x = x + _rms(_mm(y, wmix_ref[...]), gmix_ref[...])
    xn = _rms(x, gpre_ref[...]).astype(BF16)
    for c0 in range(0, dff, FF_CHUNK):
        g = _mm(xn, wi_ref[:, c0:c0 + FF_CHUNK])
        u = _mm(xn, wi_ref[:, dff + c0:dff + c0 + FF_CHUNK])
        act_ref[:, c0:c0 + FF_CHUNK] = ((g * jax.nn.sigmoid(g)) * u).astype(BF16)
    y = _mm(act_ref[...], wo_ref[...])
    o_ref[...] = x + 0.5 * _rms(y, gpost_ref[...])


def _ffn(x2, g_pre, g_post, w_in, w_out, which, tm, mix=None):
    n, d = x2.shape
    dff = w_out.shape[-2]
    assert dff % FF_CHUNK == 0 and n % tm == 0
    row = lambda c: pl.BlockSpec((tm, c), lambda i: (i, 0))
    mix_in, mix_specs = [], []
    if mix is not None:
        g_mix, w_mix, ys = mix
        mix_in = [g_mix, w_mix, *ys]
        mix_specs = [_resident((1, d)), _resident(w_mix.shape)] + [row(y.shape[1]) for y in ys]
    return pl.pallas_call(
        functools.partial(_ffn_kernel, dff=dff),
        grid=(n // tm,),
        in_specs=[row(d), _resident((1, d)), _resident((1, d)), _resident((d, 2 * dff), which),
                  _resident((dff, d), which)] + mix_specs,
        out_specs=row(d),
        out_shape=jax.ShapeDtypeStruct((n, d), F32),
        scratch_shapes=[pltpu.VMEM((tm, dff), BF16)],
        compiler_params=_params("parallel"),
        name="ffn_mix" if mix is not None else "ffn",
    )(x2, g_pre, g_post, w_in, w_out, *mix_in)


def _emit_kv(k, v, k_ref, v_ref, kb_ref, vx_ref, heads, vt_tiles):
    tm, w = k.shape
    for l in range(k_ref.shape[0]):
        k_ref[l] = k.reshape(tm, heads, w // heads) if l == 0 else jnp.zeros(k_ref.shape[1:], F32)
        v_ref[l] = v.reshape(tm, heads, w // heads) if l == 0 else jnp.zeros(v_ref.shape[1:], F32)
    kb_ref[...] = k.astype(BF16)
    if vt_tiles:
        vx_ref[0, :, 0] = v.T.astype(BF16).reshape(w // LANES, LANES, tm)
    else:
        vx_ref[...] = v.astype(BF16)


def _kv_out(n, w, heads, tm, layer, stacks, n_layers, vt_tiles_per_seq):
    dh = w // heads
    assert (stacks is None) == (layer == 0)
    stacked = pl.BlockSpec((n_layers if stacks is None else 1, tm, heads, dh), lambda i: (layer, i, 0, 0))
    stacked_shape = jax.ShapeDtypeStruct((n_layers, n, heads, dh), F32)
    row = pl.BlockSpec((tm, w), lambda i: (i, 0))
    if vt_tiles_per_seq:
        nkb = vt_tiles_per_seq
        vx = pl.BlockSpec((1, w // LANES, 1, LANES, tm), lambda i: (i // nkb, 0, i % nkb, 0, 0))
        vx_shape = jax.ShapeDtypeStruct((n // (nkb * tm), w // LANES, nkb, LANES, tm), BF16)
    else:
        vx, vx_shape = row, jax.ShapeDtypeStruct((n, w), BF16)
    specs = [stacked, stacked, row, vx]
    shapes = [stacked_shape, stacked_shape, jax.ShapeDtypeStruct((n, w), BF16), vx_shape]
    extra_in = [] if stacks is None else list(stacks)
    return specs, shapes, extra_in


def _inproj_even_kernel(*refs, w, vt_tiles):
    x_ref, g_ref, w_ref, bf_ref = refs[:4]
    u_ref, gate_ref, q_ref, logf_ref, logf1_ref, k_ref, v_ref, kb_ref, vx_ref = refs[-9:]
    hn = _rms(x_ref[...], g_ref[...]).astype(BF16)
    u_ref[...] = _mm(hn, w_ref[:, 0:w])
    gate_ref[...] = _mm(hn, w_ref[:, w:2 * w])
    q_ref[...] = (_mm(hn, w_ref[:, 2 * w:3 * w]) * Q_SCALE).astype(BF16)
    logf_ref[...] = _log_sigmoid(_mm(hn, w_ref[:, 5 * w:6 * w]) + bf_ref[:, 0:w])
    logf1_ref[...] = _log_sigmoid(_mm(hn, w_ref[:, 6 * w:6 * w + LANES]) + bf_ref[:, w:w + LANES])
    _emit_kv(_mm(hn, w_ref[:, 3 * w:4 * w]), _mm(hn, w_ref[:, 4 * w:5 * w]), k_ref, v_ref, kb_ref, vx_ref,
             FOX_HEADS, vt_tiles)


def _inproj_even(x2, g, w_wide, bf_wide, tm, layer, stacks, n_layers, vt_tiles_per_seq):
    n, d = x2.shape
    w = (w_wide.shape[1] - LANES) // 6
    row = lambda c: pl.BlockSpec((tm, c), lambda i: (i, 0))
    f = jax.ShapeDtypeStruct((n, w), F32)
    b = jax.ShapeDtypeStruct((n, w), BF16)
    kv_specs, kv_shapes, extra_in = _kv_out(n, w, FOX_HEADS, tm, layer, stacks, n_layers, vt_tiles_per_seq)
    return pl.pallas_call(
        functools.partial(_inproj_even_kernel, w=w, vt_tiles=bool(vt_tiles_per_seq)),
        grid=(n // tm,),
        in_specs=[row(d), _resident((1, d)), _resident(w_wide.shape), _resident(bf_wide.shape)]
        + [pl.BlockSpec(memory_space=pl.ANY)] * len(extra_in),
        out_specs=[row(w)] * 4 + [row(LANES)] + kv_specs,
        out_shape=[f, f, b, f, jax.ShapeDtypeStruct((n, LANES), F32)] + kv_shapes,
        input_output_aliases={4 + a: 5 + a for a in range(len(extra_in))},
        compiler_params=_params("parallel"),
        name="inproj_even",
    )(x2, g, w_wide, bf_wide, *extra_in)


def _scan_rows(a, b, h):
    tt, c = b.shape
    assert tt % SUBLANES == 0
    groups = (tt // SUBLANES, SUBLANES, c)
    b = b.reshape(groups)
    a = None if a is None else a.reshape(groups)
    sub = lax.broadcasted_iota(jnp.int32, groups, 1)
    d = 1
    while d < SUBLANES:
        keep = sub >= d
        if a is None:
            b = b + jnp.where(keep, pltpu.roll(b, d, 1), 0.0)
        else:
            b = a * jnp.where(keep, pltpu.roll(b, d, 1), 0.0) + b
            a = a * jnp.where(keep, pltpu.roll(a, d, 1), 1.0)
        d *= 2
    out = []
    for g in range(groups[0]):
        blk = b[g] + (h if a is None else a[g] * h)
        out.append(blk)
        h = blk[SUBLANES - 1:SUBLANES]
    return jnp.concatenate(out, axis=0)


def _lru_kernel(u_ref, gate_ref, h0_ref, buf0_ref, cw_ref, cb_ref, wr_ref, wi_ref, br_ref, bi_ref, lam_ref,
                ya_ref, hlast_ref, nbuf_ref, ubuf, hcar, *, tt):
    t = pl.program_id(1)
    pad = ubuf.shape[0] - tt

    @pl.when(t == 0)
    def _():
        ubuf[0:pad, :] = buf0_ref[0]
        hcar[...] = h0_ref[0]

    u = u_ref[0]
    ubuf[pad:pad + tt, :] = u
    uc = cb_ref[...]
    for j in range(CONV_W):
        o = pad - (CONV_W - 1) + j
        uc = uc + ubuf[o:o + tt, :] * cw_ref[j:j + 1, :]
    ucb = uc.astype(BF16)
    r = jax.nn.sigmoid(_mm(ucb, wr_ref[...]) + br_ref[...])
    ig = jax.nn.sigmoid(_mm(ucb, wi_ref[...]) + bi_ref[...])
    log_a = (-LRU_C * r) * _softplus(-lam_ref[...])
    a = jnp.exp(log_a)
    bx = jnp.sqrt(-jnp.tanh(log_a) * (a * a + 1.0)) * (ig * uc)
    bx = _scan_rows(a, bx, hcar[...])
    hcar[...] = bx[tt - 1:tt, :]
    ya_ref[0] = (bx * jax.nn.gelu(gate_ref[0])).astype(BF16)
    ubuf[0:pad, :] = ubuf[tt:tt + pad, :]

    @pl.when(t == pl.num_programs(1) - 1)
    def _():
        hlast_ref[0] = bx[tt - 1:tt, :]
        nbuf_ref[0] = ubuf[pad - (CONV_W - 1):pad, :]


def _lru(u, gate, h0, buf0, cw, cb, wr, wi, br, bi, lam, tt):
    bsz, T, w = u.shape
    pad = 8
    seq = pl.BlockSpec((1, tt, w), lambda b, t: (b, t, 0))
    per_b = lambda r: pl.BlockSpec((1, r, w), lambda b, t: (b, 0, 0))
    return pl.pallas_call(
        functools.partial(_lru_kernel, tt=tt),
        grid=(bsz, T // tt),
        in_specs=[seq, seq, per_b(1), per_b(pad), _resident((CONV_W, w)), _resident((1, w)), _resident((w, w)),
                  _resident((w, w)), _resident((1, w)), _resident((1, w)), _resident((1, w))],
        out_specs=[seq, per_b(1), per_b(CONV_W - 1)],
        out_shape=[jax.ShapeDtypeStruct((bsz, T, w), BF16), jax.ShapeDtypeStruct((bsz, 1, w), F32),
                   jax.ShapeDtypeStruct((bsz, CONV_W - 1, w), F32)],
        scratch_shapes=[pltpu.VMEM((tt + pad, w), F32), pltpu.VMEM((1, w), F32)],
        compiler_params=_params("parallel", "arbitrary"),
        name="conv_rglru",
    )(u, gate, h0, buf0, cw, cb, wr, wi, br, bi, lam)


def _cumsum_kernel(x_ref, ck_ref, cq_ref, car, *, tt):
    @pl.when(pl.program_id(1) == 0)
    def _():
        car[...] = jnp.zeros_like(car)

    x = _scan_rows(None, x_ref[0], car[...])
    car[...] = x[tt - 1:tt, :]
    x = x * LOG2E
    lo = lax.broadcasted_iota(jnp.int32, (1, LANES), 1) < HALF
    for s in range(x.shape[1] // LANES):
        xs = x[:, s * LANES:(s + 1) * LANES]
        sw = pltpu.roll(xs, HALF, 1)
        ck_ref[0, :, (2 * s) * LANES:(2 * s + 1) * LANES] = jnp.where(lo, xs, sw)
        ck_ref[0, :, (2 * s + 1) * LANES:(2 * s + 2) * LANES] = jnp.where(lo, sw, xs)
        tr = xs.T
        cq_ref[0, s, 0:1, :] = tr[0:1, :]
        cq_ref[0, s, 1:2, :] = tr[HALF:HALF + 1, :]


def _cumsum(x, tt):
    bsz, T, w = x.shape
    ns = w // LANES
    return pl.pallas_call(
        functools.partial(_cumsum_kernel, tt=tt),
        grid=(bsz, T // tt),
        in_specs=[pl.BlockSpec((1, tt, w), lambda b, t: (b, t, 0))],
        out_specs=[pl.BlockSpec((1, tt, 2 * w), lambda b, t: (b, t, 0)),
                   pl.BlockSpec((1, ns, 2, tt), lambda b, t: (b, 0, 0, t))],
        out_shape=[jax.ShapeDtypeStruct((bsz, T, 2 * w), F32), jax.ShapeDtypeStruct((bsz, ns, 2, T), F32)],
        scratch_shapes=[pltpu.VMEM((1, w), F32)],
        compiler_params=_params("parallel", "arbitrary"),
        name="logf_cumsum",
    )(x)


def _attn_kernel(*refs, tq, tk, nk_total, kv_len, pos0, fox, lam_init):
    if fox:
        q_ref, k_ref, vt_ref, cq_ref, ck_ref, o_ref, m_sc, l_sc, acc_sc, s_sc = refs
    else:
        q_ref, k_ref, vt_ref, lp_ref, sg_ref, o_ref, m_sc, l_sc, acc_sc, s_sc = refs
    i = pl.program_id(2)
    q0 = pos0 + i * tq
    shift = CHUNK.bit_length() - 1
    lo = lax.broadcasted_iota(jnp.int32, (1, LANES), 1) < HALF
    spb = q_ref.shape[2] // LANES
    qm = []
    for sl in range(spb):
        q = q_ref[0, :, sl * LANES:(sl + 1) * LANES]
        qm += [jnp.where(lo, q, jnp.zeros_like(q)), jnp.where(lo, jnp.zeros_like(q), q)]
    if fox:
        n_full = (q0 + 1) // tk
        n_need = (q0 + tq - 1) // tk + 1
    else:
        n_full = (((q0 >> shift) + 1) << shift) // tk
        n_need = (((((q0 + tq - 1) >> shift) + 1) << shift) + tk - 1) // tk
    n_need = jnp.minimum(n_need, nk_total)
    n_full = jnp.minimum(jnp.minimum(n_full, kv_len // tk), n_need)

    m_sc[...] = jnp.full_like(m_sc, NEG_INF)
    l_sc[...] = jnp.zeros_like(l_sc)
    acc_sc[...] = jnp.zeros_like(acc_sc)

    n_maps = 2 * spb

    def scores(g, j):
        k0 = j * tk if isinstance(j, int) else pl.multiple_of(j * tk, tk)
        lanes = slice((g // 2) * LANES, (g // 2 + 1) * LANES)
        hk = tk // 2
        return jnp.concatenate([_mm_nt(k_ref[0, pl.ds(k0 + r, hk), lanes], qm[g]) for r in (0, hk)], axis=0)

    assert AHEAD < n_maps
    for a in range(AHEAD):
        s_sc[a] = scores(a, 0)

    hq, hk = tq // 2, tk // 2
    diag = (tq == tk and pos0 % tk == 0 and kv_len == nk_total * tk and hq % LANES == 0 and hk % CHUNK == 0)

    def right_half(full, extra, op):
        return jnp.concatenate([full[:, :hq], op(full[:, hq:], extra)], axis=1)

    def tile(j, masked):
        k0 = pl.multiple_of(j * tk, tk)
        split = masked and diag
        if split:
            kr = lax.broadcasted_iota(jnp.int32, (hk, hq), 0)
            qr = lax.broadcasted_iota(jnp.int32, (hk, hq), 1)
            tri = (qr >= kr) if fox else ((qr >> shift) >= (kr >> shift))
        elif masked:
            kpos = k0 + lax.broadcasted_iota(jnp.int32, (tk, tq), 0)
            qpos = q0 + lax.broadcasted_iota(jnp.int32, (tk, tq), 1)
            vis = (qpos >= kpos) if fox else ((qpos >> shift) >= (kpos >> shift))
            if kv_len < nk_total * tk:
                vis = vis & (kpos < kv_len)
        for g in range(n_maps):
            sl, c = divmod(g, 2)
            dj, g_ahead = divmod(g + AHEAD, n_maps)
            s_sc[g_ahead] = scores(g_ahead, jnp.minimum(j + dj, nk_total - 1) if dj else j)
            ck = ck_ref[0, pl.ds(k0, tk), g * LANES:(g + 1) * LANES] if fox else None

            def biased(rows, cols):
                blk = s_sc[g, rows, cols]
                return blk - jnp.concatenate([ck[rows]] * (blk.shape[1] // LANES), axis=1) if fox else blk

            if split:
                s = biased(slice(0, hk), slice(0, tq))
                s = jnp.concatenate([jnp.where(tri, s[:, :hq], NEG_INF), s[:, hq:]], axis=1)
                s_late = jnp.where(tri, biased(slice(hk, tk), slice(hq, tq)), NEG_INF)
            else:
                s = biased(slice(0, tk), slice(0, tq))
                if masked:
                    s = jnp.where(vis, s, NEG_INF)
            m_prev = m_sc[g]
            m_tile = jnp.max(s, axis=0, keepdims=True)
            if split:
                m_tile = right_half(m_tile, jnp.max(s_late, axis=0, keepdims=True), jnp.maximum)
            if fox:
                cq = cq_ref[0, sl, c:c + 1, :]
                m_new = jnp.maximum(m_prev, m_tile + cq)
                m_sub = m_new - cq
            else:
                m_new = m_sub = jnp.maximum(m_prev, m_tile)
            p = jnp.exp2(s - m_sub)
            p_sum = jnp.sum(p, axis=0, keepdims=True)
            alpha = jnp.exp2(m_prev - m_new)
            vt = vt_ref[0, sl, j]
            r = slice(c * HALF, (c + 1) * HALF) if fox else slice(0, LANES)
            if split:
                p_late = jnp.exp2(s_late - m_sub[:, hq:])
                p_sum = right_half(p_sum, jnp.sum(p_late, axis=0, keepdims=True), jnp.add)
                pv = right_half(_mm(vt[r, :hk], p.astype(BF16)), _mm(vt[r, hk:], p_late.astype(BF16)), jnp.add)
            else:
                pv = _mm(vt[r, :], p.astype(BF16))
            l_sc[g] = alpha * l_sc[g] + p_sum
            m_sc[g] = m_new
            if fox:
                acc_sc[sl, r, :] = alpha * acc_sc[sl, r, :] + pv
            else:
                acc_sc[g] = alpha * acc_sc[g] + pv

    def full_body(j, carry):
        tile(j, False)
        return carry

    def masked_body(j, carry):
        tile(j, True)
        return carry

    lax.fori_loop(0, n_full, full_body, 0)
    lax.fori_loop(n_full, n_need, masked_body, 0)

    for sl in range(spb):
        a, b = 2 * sl, 2 * sl + 1
        if fox:
            ot = jnp.concatenate([acc_sc[sl, 0:HALF, :] / l_sc[a], acc_sc[sl, HALF:LANES, :] / l_sc[b]], axis=0)
            o = ot.T
        else:
            lp = lp_ref[...]
            lam = (jnp.exp(jnp.sum(lp[0:1] * lp[1:2], axis=-1, keepdims=True))
                   - jnp.exp(jnp.sum(lp[2:3] * lp[3:4], axis=-1, keepdims=True)) + lam_init)
            ot = acc_sc[a] / l_sc[a] - lam * (acc_sc[b] / l_sc[b])
            ot = ot * lax.rsqrt(jnp.mean(ot * ot, axis=0, keepdims=True) + EPS)
            o = (ot.T * sg_ref[...]) * (1.0 - lam_init)
        o_ref[0, :, sl * LANES:(sl + 1) * LANES] = o.astype(BF16)


def _attention(q, k, v, extra, *, pos0, kv_len, fox, lam_init=0.0, tq=ATTN_TQ, tk=ATTN_TK, spb=4):
    bsz, Tq, w = q.shape
    Tk = k.shape[1]
    ns = w // LANES
    Tq_pad = -(-Tq // LANES) * LANES
    tq = _pick_tile(Tq_pad, tq)
    tk = tk if Tk % tk == 0 else Tk
    nk = Tk // tk
    q = jnp.pad(q, ((0, 0), (0, Tq_pad - Tq), (0, 0)))
    vt = v if v.ndim == 5 else v.reshape(bsz, nk, tk, ns, LANES).transpose(0, 3, 1, 4, 2)
    assert vt.shape == (bsz, ns, nk, LANES, tk)
    assert ns % spb == 0
    sw = spb * LANES
    qspec = pl.BlockSpec((1, tq, sw), lambda b, s, i: (b, i, s))
    kspec = pl.BlockSpec((1, Tk, sw), lambda b, s, i: (b, 0, s))
    vspec = pl.BlockSpec((1, spb, nk, LANES, tk), lambda b, s, i: (b, s, 0, 0, 0))
    if fox:
        cq, ck = extra
        extra = (jnp.pad(cq, ((0, 0), (0, 0), (0, 0), (0, Tq_pad - Tq))), ck)
        especs = [pl.BlockSpec((1, spb, 2, tq), lambda b, s, i: (b, s, 0, i)),
                  pl.BlockSpec((1, Tk, 2 * sw), lambda b, s, i: (b, 0, s), pipeline_mode=pl.Buffered(1))]
    else:
        lp, sg = extra
        especs = [pl.BlockSpec(lp.shape, lambda b, s, i: (0, 0)), pl.BlockSpec(sg.shape, lambda b, s, i: (0, 0))]
    out = pl.pallas_call(
        functools.partial(_attn_kernel, tq=tq, tk=tk, nk_total=nk, kv_len=kv_len, pos0=pos0, fox=fox,
                          lam_init=lam_init),
        grid=(bsz, ns // spb, Tq_pad // tq),
        in_specs=[qspec, kspec, vspec] + especs,
        out_specs=qspec,
        out_shape=jax.ShapeDtypeStruct((bsz, Tq_pad, w), BF16),
        scratch_shapes=[pltpu.VMEM((2 * spb, 1, tq), F32), pltpu.VMEM((2 * spb, 1, tq), F32),
                        pltpu.VMEM((spb if fox else 2 * spb, LANES, tq), F32), pltpu.VMEM((2 * spb, tk, tq), F32)],
        compiler_params=_params("parallel", "parallel", "arbitrary"),
        name="fox_attention" if fox else "diff_attention",
    )(q, k, vt, *extra)
    return out[:, :Tq]


def _rope_table_kernel(inv_ref, c_ref, s1_ref, s2_ref, *, pos0):
    T = c_ref.shape[0]
    pos = (pos0 + lax.broadcasted_iota(jnp.int32, (T, LANES), 0)).astype(F32)
    ang = pos * inv_ref[...]
    lane = lax.broadcasted_iota(jnp.int32, (T, LANES), 1) % HALF
    sn = jnp.sin(ang)
    c_ref[...] = jnp.cos(ang)
    s1_ref[...] = jnp.where(lane < ROT_DIM // 2, -sn, 0.0)
    s2_ref[...] = jnp.where(lane >= ROT_DIM // 2, sn, 0.0)


def _rope_tables(T, pos0):
    half = ROT_DIM // 2
    inv = ROPE_THETA ** (-jnp.arange(half, dtype=F32) * (2.0 / ROT_DIM))
    lane = jnp.arange(LANES) % HALF
    inv_lane = jnp.where(lane < ROT_DIM, inv[lane % half], 0.0).reshape(1, LANES).astype(F32)
    tab = jax.ShapeDtypeStruct((T, LANES), F32)
    return pl.pallas_call(
        functools.partial(_rope_table_kernel, pos0=pos0),
        out_shape=[tab, tab, tab],
        name="rope_tables",
    )(inv_lane)


def _inproj_odd_kernel(*refs, w, vt_tiles):
    x_ref, g_ref, w_ref, c_ref, s1_ref, s2_ref = refs[:6]
    q_ref, k_ref, v_ref, kb_ref, vx_ref = refs[-5:]
    hn = _rms(x_ref[...], g_ref[...]).astype(BF16)
    cs, s1, s2 = c_ref[...], s1_ref[...], s2_ref[...]
    half = ROT_DIM // 2

    def rope(t):
        out = []
        for h in range(w // LANES):
            s = t[:, h * LANES:(h + 1) * LANES]
            out.append(s * cs + pltpu.roll(s, LANES - half, 1) * s1 + pltpu.roll(s, half, 1) * s2)
        return jnp.concatenate(out, axis=1)

    q_ref[...] = (rope(_mm(hn, w_ref[:, 0:w])) * Q_SCALE).astype(BF16)
    _emit_kv(rope(_mm(hn, w_ref[:, w:2 * w])), _mm(hn, w_ref[:, 2 * w:3 * w]), k_ref, v_ref, kb_ref, vx_ref,
             DIFF_HEADS, vt_tiles)


def _inproj_odd(x2, g, w_in, tabs, tab_blocks, tm, layer, stacks, n_layers, vt_tiles_per_seq):
    n, d = x2.shape
    w = w_in.shape[1] // 3
    row = lambda c: pl.BlockSpec((tm, c), lambda i: (i, 0))
    tab = pl.BlockSpec((tm, LANES), lambda i: (i % tab_blocks, 0))
    kv_specs, kv_shapes, extra_in = _kv_out(n, w, DIFF_HEADS, tm, layer, stacks, n_layers, vt_tiles_per_seq)
    return pl.pallas_call(
        functools.partial(_inproj_odd_kernel, w=w, vt_tiles=bool(vt_tiles_per_seq)),
        grid=(n // tm,),
        in_specs=[row(d), _resident((1, d)), _resident((d, 3 * w)), tab, tab, tab]
        + [pl.BlockSpec(memory_space=pl.ANY)] * len(extra_in),
        out_specs=[row(w)] + kv_specs,
        out_shape=[jax.ShapeDtypeStruct((n, w), BF16)] + kv_shapes,
        input_output_aliases={6 + a: 1 + a for a in range(len(extra_in))},
        compiler_params=_params("parallel"),
        name="inproj_odd",
    )(x2, g, w_in, *tabs, *extra_in)


def _history_kv_kernel(ck_ref, cv_ref, kn_ref, vn_ref, k_ref, vt_ref, v_sc, *, time_minor):
    w, tp = k_ref.shape[2], k_ref.shape[1]
    new = kn_ref.shape[1]
    past = ck_ref.shape[4] if time_minor else ck_ref.shape[2]
    slots = (w // LANES, LANES)
    k_ref[0, past:past + new, :] = kn_ref[0]
    k_ref[0, past + new:tp, :] = jnp.zeros((tp - past - new, w), BF16)
    if time_minor:
        k_ref[0, 0:past, :] = ck_ref[0, 0].reshape(w, past).T.astype(BF16)
        vt_ref[0, :, 0, :, 0:past] = cv_ref[0, 0].reshape(*slots, past).astype(BF16)
        v_sc[0:new, :] = vn_ref[0].astype(F32)
        v_sc[new:tp - past, :] = jnp.zeros((tp - past - new, w), F32)
        vt_ref[0, :, 0, :, past:tp] = v_sc[...].T.astype(BF16).reshape(*slots, tp - past)
    else:
        k_ref[0, 0:past, :] = ck_ref[0, 0].reshape(past, w).astype(BF16)
        v_sc[0:past, :] = cv_ref[0, 0].reshape(past, w)
        v_sc[past:past + new, :] = vn_ref[0].astype(F32)
        v_sc[past + new:tp, :] = jnp.zeros((tp - past - new, w), F32)
        vt_ref[0, :, 0] = v_sc[...].T.astype(BF16).reshape(*slots, tp)


def _history_kv(cache_k, cache_v, layer, k_new, v_new, tp):
    _, bsz, past, heads, dh = cache_k.shape
    T, w = k_new.shape[1:]
    time_minor = dh < LANES and past % LANES == 0 and (tp - past) % LANES == 0
    if time_minor:
        cache_k, cache_v = (jnp.transpose(c, (0, 1, 3, 4, 2)) for c in (cache_k, cache_v))
    cspec = pl.BlockSpec((1, 1) + cache_k.shape[2:], lambda b: (layer, b, 0, 0, 0))
    nspec = pl.BlockSpec((1, T, w), lambda b: (b, 0, 0))
    return pl.pallas_call(
        functools.partial(_history_kv_kernel, time_minor=time_minor),
        grid=(bsz,),
        in_specs=[cspec, cspec, nspec, nspec],
        out_specs=[pl.BlockSpec((1, tp, w), lambda b: (b, 0, 0)),
                   pl.BlockSpec((1, w // LANES, 1, LANES, tp), lambda b: (b, 0, 0, 0, 0))],
        out_shape=[jax.ShapeDtypeStruct((bsz, tp, w), BF16),
                   jax.ShapeDtypeStruct((bsz, w // LANES, 1, LANES, tp), BF16)],
        scratch_shapes=[pltpu.VMEM((tp - past if time_minor else tp, w), F32)],
        compiler_params=_params("parallel"),
        name="history_kv",
    )(cache_k, cache_v, k_new, v_new)


def _trunk(x, pos0, caches, wts, tm):
    bsz, T, d = x.shape
    n = bsz * T
    depth = wts["norm_g"].shape[0]
    w = d // 2
    x2 = x.reshape(n, d)
    tt = _pick_tile(T, 256)
    flat = lambda a: a.reshape(n, a.shape[-1])
    seq = lambda a: a.reshape(bsz, T, a.shape[-1])
    n_keys = T if caches is None else T + caches[0].shape[2]
    key_pad = -n_keys % LANES
    pad_keys = lambda a: jnp.pad(a, ((0, 0), (0, key_pad), (0, 0))) if key_pad else a
    rope = None
    n_even, n_odd = (depth + 1) // 2, depth // 2
    vt_tiles_per_seq = T // tm if caches is None and tm == ATTN_TK and T % tm == 0 else 0
    fox_kv = diff_kv = None
    even, odd = [], []
    for l in range(depth):
        g = wts["norm_g"][l].reshape(6, 1, d)
        x2 = _ffn(x2, g[0], g[1], wts["w_ffn_in"], wts["w_ffn_out"], (l, 0), tm)
        if l % 2 == 0:
            e = l // 2
            u, gate, q, logf, logf1, *fox_kv, kb, vx = _inproj_even(
                x2, g[2], wts["w_in_even"][e], wts["b_fox_f"][e], tm, e, fox_kv, n_even, vt_tiles_per_seq)
            if caches is None:
                h0 = jnp.zeros((bsz, 1, w), F32)
                buf0 = jnp.zeros((bsz, 8, w), F32)
                kb_all, logf_all = seq(kb), seq(logf)
                vb_all = vx if vt_tiles_per_seq else seq(vx)
            else:
                flogf, lru_h, lru_conv = (c[e] for c in caches[2:5])
                h0 = lru_h.reshape(bsz, 1, w)
                buf0 = jnp.pad(lru_conv, ((0, 0), (8 - (CONV_W - 1), 0), (0, 0)))
                kb_all, vb_all = _history_kv(caches[0], caches[1], e, seq(kb), seq(vx), n_keys + key_pad)
                logf_all = jnp.concatenate([jnp.repeat(flogf, HALF, axis=-1), seq(logf)], axis=1)
            ya, h_last, nbuf = _lru(seq(u), seq(gate), h0, buf0, wts["lru_conv_w"][e], wts["lru_conv_b"][e],
                                    wts["lru_wr"][e], wts["lru_wi"][e], wts["lru_br"][e], wts["lru_bi"][e],
                                    wts["lru_lambda"][e], tt)
            Tk = logf_all.shape[1]
            Tp = Tk + key_pad
            ck, cq = _cumsum(pad_keys(logf_all), Tp if Tp <= 2 * CUMSUM_TILE else _pick_tile(Tp, CUMSUM_TILE))
            yb = _attention(seq(q), kb_all, vb_all, (cq[..., Tk - T:Tk], ck), pos0=pos0, kv_len=Tk, fox=True)
            mix = (g[3], wts["w_out_even"][e], [flat(ya), flat(yb)])
            even.append((seq(logf1)[..., :FOX_HEADS], h_last.reshape(bsz, w), nbuf))
        else:
            o = l // 2
            lam_init = 0.8 - 0.6 * math.exp(-0.3 * l)
            if rope is None:
                tabs = _rope_tables(T, pos0)
                if T % tm:
                    tabs = [jnp.tile(t, (n // T, 1)) for t in tabs]
                rope = (tabs, tabs[0].shape[0] // tm)
            q, *diff_kv, kb, vx = _inproj_odd(x2, g[2], wts["w_in_odd"][o], rope[0], rope[1], tm, o, diff_kv,
                                              n_odd, vt_tiles_per_seq)
            if caches is None:
                kb_all = seq(kb)
                vb_all = vx if vt_tiles_per_seq else seq(vx)
            else:
                kb_all, vb_all = _history_kv(caches[5], caches[6], o, seq(kb), seq(vx), n_keys + key_pad)
            y = _attention(seq(q), kb_all, vb_all, (wts["diff_lambda"][o], wts["diff_subln_g"][o]),
                           pos0=pos0, kv_len=n_keys, fox=False, lam_init=lam_init)
            mix = (g[3], wts["w_out_odd"][o], [flat(y)])
        x2 = _ffn(x2, g[4], g[5], wts["w_ffn_in"], wts["w_ffn_out"], (l, 1), tm, mix)
    per_seq = lambda a: a.reshape(a.shape[0], bsz, T, *a.shape[2:])
    even_out = [per_seq(a) for a in fox_kv] + [jnp.stack(z) for z in zip(*even)]
    return x2.reshape(bsz, T, d), even_out, [per_seq(a) for a in diff_kv]


def _prepare_weights(norm_g, w_ffn_in, w_ffn_out, w_in_even, b_fox_f, lru_conv_w, lru_conv_b, lru_w_gates,
                     lru_b_gates, lru_lambda, w_out_even, w_in_odd, diff_lambda, diff_subln_g, w_out_odd):
    n_even, d, _ = w_in_even.shape
    w = d // 2
    blk = w // LRU_BLOCKS
    wf = jnp.repeat(w_in_even[:, :, 5 * w:], HALF, axis=-1)
    pad_lanes = lambda a: jnp.pad(a, [(0, 0)] * (a.ndim - 1) + [(0, LANES - a.shape[-1])])
    eye = jnp.eye(LRU_BLOCKS, dtype=F32)
    dense = lambda g: jnp.einsum("encg,nm->encmg", g, eye).reshape(n_even, w, w)
    return dict(
        norm_g=norm_g, w_ffn_in=w_ffn_in.astype(BF16), w_ffn_out=w_ffn_out.astype(BF16),
        w_in_even=jnp.concatenate([w_in_even[:, :, :5 * w], wf, pad_lanes(w_in_even[:, :, 5 * w:])],
                                  axis=-1).astype(BF16),
        b_fox_f=jnp.concatenate([jnp.repeat(b_fox_f, HALF, axis=-1), pad_lanes(b_fox_f)],
                                axis=-1).reshape(n_even, 1, w + LANES),
        lru_conv_w=lru_conv_w, lru_conv_b=lru_conv_b.reshape(n_even, 1, w),
        lru_wr=dense(lru_w_gates[..., :blk]).astype(BF16), lru_wi=dense(lru_w_gates[..., blk:]).astype(BF16),
        lru_br=lru_b_gates[..., :blk].reshape(n_even, 1, w), lru_bi=lru_b_gates[..., blk:].reshape(n_even, 1, w),
        lru_lambda=lru_lambda.reshape(n_even, 1, w), w_out_even=w_out_even.astype(BF16),
        w_in_odd=w_in_odd.astype(BF16), diff_lambda=diff_lambda, diff_subln_g=diff_subln_g.reshape(-1, 1, LANES),
        w_out_odd=w_out_odd.astype(BF16))


def kernel(x_prompt, x_sample, cache_fox_k, cache_fox_v, cache_fox_logf, state_lru_h, state_lru_conv, cache_diff_k, cache_diff_v, norm_g, w_ffn_in, w_ffn_out, w_in_even, b_fox_f, lru_conv_w, lru_conv_b, lru_w_gates, lru_b_gates, lru_lambda, w_out_even, w_in_odd, diff_lambda, diff_subln_g, w_out_odd):
    wts = _prepare_weights(norm_g, w_ffn_in, w_ffn_out, w_in_even, b_fox_f, lru_conv_w, lru_conv_b, lru_w_gates,
                           lru_b_gates, lru_lambda, w_out_even, w_in_odd, diff_lambda, diff_subln_g, w_out_odd)
    y_p, pe, po = _trunk(x_prompt, 0, None, wts, _pick_tile(x_prompt.shape[0] * x_prompt.shape[1], 512))
    caches = (cache_fox_k, cache_fox_v, cache_fox_logf, state_lru_h, state_lru_conv, cache_diff_k, cache_diff_v)
    y_s, se, so = _trunk(x_sample, cache_fox_k.shape[2], caches, wts,
                         _pick_tile(x_sample.shape[0] * x_sample.shape[1], 256))
    return (y_p, y_s, *pe, *po, *se, *so)
```

```python
import functools
import math

import jax
import jax.numpy as jnp
from jax import lax
from jax.experimental import pallas as pl
from jax.experimental.pallas import tpu as pltpu

F32 = jnp.float32
BF16 = jnp.bfloat16

EPS = 1e-6
NEG_INF = -1e30
LRU_C = 8.0
LRU_BLOCKS = 8
CONV_W = 4
FOX_HEADS = 8
DIFF_HEADS = 8
CHUNK = 64
ROT_DIM = 16
ROPE_THETA = 500000.0

LANES = 128
SUBLANES = 8
HALF = LANES // 2
FF_CHUNK = 256
AHEAD = 2
ATTN_TQ, ATTN_TK = 512, 512
CUMSUM_TILE = 1024
LOG2E = 1.0 / math.log(2.0)
Q_SCALE = HALF ** -0.5 * LOG2E
VMEM_LIMIT = 56 * 1024 * 1024


def _params(*sem):
    return pltpu.CompilerParams(dimension_semantics=sem, vmem_limit_bytes=VMEM_LIMIT)


def _resident(shape, lead=()):
    nd = len(shape)
    return pl.BlockSpec((None,) * len(lead) + tuple(shape), lambda *_: tuple(lead) + (0,) * nd,
                        pipeline_mode=pl.Buffered(1))


def _rms(x, g):
    return x * lax.rsqrt(jnp.mean(x * x, axis=-1, keepdims=True) + EPS) * g


def _mm(a, b):
    return jnp.dot(a, b, preferred_element_type=F32)


def _mm_nt(a, b):
    return lax.dot_general(a, b, (((1,), (1,)), ((), ())), preferred_element_type=F32)


def _log_sigmoid(x):
    return jnp.minimum(x, 0.0) - jnp.log1p(jnp.exp(-jnp.abs(x)))


def _softplus(x):
    return jnp.maximum(x, 0.0) + jnp.log1p(jnp.exp(-jnp.abs(x)))


def _pick_tile(n, target):
    t = min(n, target)
    while n % t:
        t //= 2
    return t


def _ffn_kernel(x_ref, gpre_ref, gpost_ref, wi_ref, wo_ref, *rest, dff):
    *mix, o_ref, act_ref = rest
    x = x_ref[...]
    if mix:
        gmix_ref, wmix_ref, *y_refs = mix
        y = jnp.concatenate([r[...] for r in y_refs], axis=1)
        x = x + _rms(_mm(y, wmix_ref[...]), gmix_ref[...])
    xn = _rms(x, gpre_ref[...]).astype(BF16)
    for c0 in range(0, dff, FF_CHUNK):
        g = _mm(xn, wi_ref[:, c0:c0 + FF_CHUNK])
        u = _mm(xn, wi_ref[:, dff + c0:dff + c0 + FF_CHUNK])
        act_ref[:, c0:c0 + FF_CHUNK] = ((g * jax.nn.sigmoid(g)) * u).astype(BF16)
    y = _mm(act_ref[...], wo_ref[...])
    o_ref[...] = x + 0.5 * _rms(y, gpost_ref[...])


def _ffn(x2, g_pre, g_post, w_in, w_out, which, tm, mix=None):
    n, d = x2.shape
    dff = w_out.shape[-2]
    assert dff % FF_CHUNK == 0 and n % tm == 0
    row = lambda c: pl.BlockSpec((tm, c), lambda i: (i, 0))
    mix_in, mix_specs = [], []
    if mix is not None:
        g_mix, w_mix, ys = mix
        mix_in = [g_mix, w_mix, *ys]
        mix_specs = [_resident((1, d)), _resident(w_mix.shape)] + [row(y.shape[1]) for y in ys]
    return pl.pallas_call(
        functools.partial(_ffn_kernel, dff=dff),
        grid=(n // tm,),
        in_specs=[row(d), _resident((1, d)), _resident((1, d)), _resident((d, 2 * dff), which),
                  _resident((dff, d), which)] + mix_specs,
        out_specs=row(d),
        out_shape=jax.ShapeDtypeStruct((n, d), F32),
        scratch_shapes=[pltpu.VMEM((tm, dff), BF16)],
        compiler_params=_params("parallel"),
        name="ffn_mix" if mix is not None else "ffn",
    )(x2, g_pre, g_post, w_in, w_out, *mix_in)


def _emit_kv(k, v, k_ref, v_ref, kb_ref, vx_ref, heads, vt_tiles):
    tm, w = k.shape
    vt = v.T if vt_tiles else None
    if k_ref.ndim == 5:
        k_out, v_out = (t.reshape(1, heads, w // heads, tm) for t in (k.T, vt))
    else:
        k_out, v_out = (t.reshape(tm, heads, w // heads) for t in (k, v))
    for l in range(k_ref.shape[0]):
        k_ref[l] = k_out if l == 0 else jnp.zeros_like(k_out)
        v_ref[l] = v_out if l == 0 else jnp.zeros_like(v_out)
    kb_ref[...] = k.astype(BF16)
    if vt_tiles:
        vx_ref[0, :, 0] = vt.astype(BF16).reshape(w // LANES, LANES, tm)
    else:
        vx_ref[...] = v.astype(BF16)


def _kv_out(n, w, heads, tm, layer, stacks, n_layers, vt_tiles_per_seq):
    dh = w // heads
    assert (stacks is None) == (layer == 0)
    own_layers = n_layers if stacks is None else 1
    nkb = vt_tiles_per_seq
    if nkb and dh < LANES:
        stacked = pl.BlockSpec((own_layers, 1, heads, dh, tm), lambda i: (layer, i // nkb, 0, 0, i % nkb))
        stacked_shape = jax.ShapeDtypeStruct((n_layers, n // (nkb * tm), heads, dh, nkb * tm), F32)
    else:
        stacked = pl.BlockSpec((own_layers, tm, heads, dh), lambda i: (layer, i, 0, 0))
        stacked_shape = jax.ShapeDtypeStruct((n_layers, n, heads, dh), F32)
    row = pl.BlockSpec((tm, w), lambda i: (i, 0))
    if vt_tiles_per_seq:
        vx = pl.BlockSpec((1, w // LANES, 1, LANES, tm), lambda i: (i // nkb, 0, i % nkb, 0, 0))
        vx_shape = jax.ShapeDtypeStruct((n // (nkb * tm), w // LANES, nkb, LANES, tm), BF16)
    else:
        vx, vx_shape = row, jax.ShapeDtypeStruct((n, w), BF16)
    specs = [stacked, stacked, row, vx]
    shapes = [stacked_shape, stacked_shape, jax.ShapeDtypeStruct((n, w), BF16), vx_shape]
    extra_in = [] if stacks is None else list(stacks)
    return specs, shapes, extra_in


def _inproj_even_kernel(*refs, w, vt_tiles):
    x_ref, g_ref, w_ref, bf_ref = refs[:4]
    u_ref, gate_ref, q_ref, logf_ref, logf1_ref, k_ref, v_ref, kb_ref, vx_ref = refs[-9:]
    hn = _rms(x_ref[...], g_ref[...]).astype(BF16)
    u_ref[...] = _mm(hn, w_ref[:, 0:w])
    gate_ref[...] = _mm(hn, w_ref[:, w:2 * w])
    q_ref[...] = (_mm(hn, w_ref[:, 2 * w:3 * w]) * Q_SCALE).astype(BF16)
    logf_ref[...] = _log_sigmoid(_mm(hn, w_ref[:, 5 * w:6 * w]) + bf_ref[:, 0:w])
    logf1_ref[...] = _log_sigmoid(_mm(hn, w_ref[:, 6 * w:6 * w + LANES]) + bf_ref[:, w:w + LANES])
    _emit_kv(_mm(hn, w_ref[:, 3 * w:4 * w]), _mm(hn, w_ref[:, 4 * w:5 * w]), k_ref, v_ref, kb_ref, vx_ref,
             FOX_HEADS, vt_tiles)


def _inproj_even(x2, g, w_wide, bf_wide, tm, layer, stacks, n_layers, vt_tiles_per_seq):
    n, d = x2.shape
    w = (w_wide.shape[1] - LANES) // 6
    row = lambda c: pl.BlockSpec((tm, c), lambda i: (i, 0))
    f = jax.ShapeDtypeStruct((n, w), F32)
    b = jax.ShapeDtypeStruct((n, w), BF16)
    kv_specs, kv_shapes, extra_in = _kv_out(n, w, FOX_HEADS, tm, layer, stacks, n_layers, vt_tiles_per_seq)
    return pl.pallas_call(
        functools.partial(_inproj_even_kernel, w=w, vt_tiles=bool(vt_tiles_per_seq)),
        grid=(n // tm,),
        in_specs=[row(d), _resident((1, d)), _resident(w_wide.shape), _resident(bf_wide.shape)]
        + [pl.BlockSpec(memory_space=pl.ANY)] * len(extra_in),
        out_specs=[row(w)] * 4 + [row(LANES)] + kv_specs,
        out_shape=[f, f, b, f, jax.ShapeDtypeStruct((n, LANES), F32)] + kv_shapes,
        input_output_aliases={4 + a: 5 + a for a in range(len(extra_in))},
        compiler_params=_params("parallel"),
        name="inproj_even",
    )(x2, g, w_wide, bf_wide, *extra_in)


def _scan_rows(a, b, h):
    tt, c = b.shape
    assert tt % SUBLANES == 0
    groups = (tt // SUBLANES, SUBLANES, c)
    b = b.reshape(groups)
    a = None if a is None else a.reshape(groups)
    sub = lax.broadcasted_iota(jnp.int32, groups, 1)
    d = 1
    while d < SUBLANES:
        keep = sub >= d
        if a is None:
            b = b + jnp.where(keep, pltpu.roll(b, d, 1), 0.0)
        else:
            b = a * jnp.where(keep, pltpu.roll(b, d, 1), 0.0) + b
            a = a * jnp.where(keep, pltpu.roll(a, d, 1), 1.0)
        d *= 2
    out = []
    for g in range(groups[0]):
        blk = b[g] + (h if a is None else a[g] * h)
        out.append(blk)
        h = blk[SUBLANES - 1:SUBLANES]
    return jnp.concatenate(out, axis=0)


def _lru_kernel(u_ref, gate_ref, h0_ref, buf0_ref, cw_ref, cb_ref, wr_ref, wi_ref, br_ref, bi_ref, lam_ref,
                ya_ref, hlast_ref, nbuf_ref, ubuf, hcar, *, tt):
    t = pl.program_id(1)
    pad = ubuf.shape[0] - tt

    @pl.when(t == 0)
    def _():
        ubuf[0:pad, :] = buf0_ref[0]
        hcar[...] = h0_ref[0]

    u = u_ref[0]
    ubuf[pad:pad + tt, :] = u
    uc = cb_ref[...]
    for j in range(CONV_W):
        o = pad - (CONV_W - 1) + j
        uc = uc + ubuf[o:o + tt, :] * cw_ref[j:j + 1, :]
    ucb = uc.astype(BF16)
    r = jax.nn.sigmoid(_mm(ucb, wr_ref[...]) + br_ref[...])
    ig = jax.nn.sigmoid(_mm(ucb, wi_ref[...]) + bi_ref[...])
    log_a = (-LRU_C * r) * _softplus(-lam_ref[...])
    a = jnp.exp(log_a)
    bx = jnp.sqrt(-jnp.tanh(log_a) * (a * a + 1.0)) * (ig * uc)
    bx = _scan_rows(a, bx, hcar[...])
    hcar[...] = bx[tt - 1:tt, :]
    ya_ref[0] = (bx * jax.nn.gelu(gate_ref[0])).astype(BF16)
    ubuf[0:pad, :] = ubuf[tt:tt + pad, :]

    @pl.when(t == pl.num_programs(1) - 1)
    def _():
        hlast_ref[0] = bx[tt - 1:tt, :]
        nbuf_ref[0] = ubuf[pad - (CONV_W - 1):pad, :]


def _lru(u, gate, h0, buf0, cw, cb, wr, wi, br, bi, lam, tt):
    bsz, T, w = u.shape
    pad = 8
    seq = pl.BlockSpec((1, tt, w), lambda b, t: (b, t, 0))
    per_b = lambda r: pl.BlockSpec((1, r, w), lambda b, t: (b, 0, 0))
    return pl.pallas_call(
        functools.partial(_lru_kernel, tt=tt),
        grid=(bsz, T // tt),
        in_specs=[seq, seq, per_b(1), per_b(pad), _resident((CONV_W, w)), _resident((1, w)), _resident((w, w)),
                  _resident((w, w)), _resident((1, w)), _resident((1, w)), _resident((1, w))],
        out_specs=[seq, per_b(1), per_b(CONV_W - 1)],
        out_shape=[jax.ShapeDtypeStruct((bsz, T, w), BF16), jax.ShapeDtypeStruct((bsz, 1, w), F32),
                   jax.ShapeDtypeStruct((bsz, CONV_W - 1, w), F32)],
        scratch_shapes=[pltpu.VMEM((tt + pad, w), F32), pltpu.VMEM((1, w), F32)],
        compiler_params=_params("parallel", "arbitrary"),
        name="conv_rglru",
    )(u, gate, h0, buf0, cw, cb, wr, wi, br, bi, lam)


def _cumsum_kernel(x_ref, ck_ref, cq_ref, car, *, tt):
    @pl.when(pl.program_id(1) == 0)
    def _():
        car[...] = jnp.zeros_like(car)

    x = _scan_rows(None, x_ref[0], car[...])
    car[...] = x[tt - 1:tt, :]
    x = x * LOG2E
    lo = lax.broadcasted_iota(jnp.int32, (1, LANES), 1) < HALF
    for s in range(x.shape[1] // LANES):
        xs = x[:, s * LANES:(s + 1) * LANES]
        sw = pltpu.roll(xs, HALF, 1)
        ck_ref[0, :, (2 * s) * LANES:(2 * s + 1) * LANES] = jnp.where(lo, xs, sw)
        ck_ref[0, :, (2 * s + 1) * LANES:(2 * s + 2) * LANES] = jnp.where(lo, sw, xs)
        tr = xs.T
        cq_ref[0, s, 0:1, :] = tr[0:1, :]
        cq_ref[0, s, 1:2, :] = tr[HALF:HALF + 1, :]


def _cumsum(x, tt):
    bsz, T, w = x.shape
    ns = w // LANES
    return pl.pallas_call(
        functools.partial(_cumsum_kernel, tt=tt),
        grid=(bsz, T // tt),
        in_specs=[pl.BlockSpec((1, tt, w), lambda b, t: (b, t, 0))],
        out_specs=[pl.BlockSpec((1, tt, 2 * w), lambda b, t: (b, t, 0)),
                   pl.BlockSpec((1, ns, 2, tt), lambda b, t: (b, 0, 0, t))],
        out_shape=[jax.ShapeDtypeStruct((bsz, T, 2 * w), F32), jax.ShapeDtypeStruct((bsz, ns, 2, T), F32)],
        scratch_shapes=[pltpu.VMEM((1, w), F32)],
        compiler_params=_params("parallel", "arbitrary"),
        name="logf_cumsum",
    )(x)


def _attn_kernel(*refs, tq, tk, nk_total, kv_len, pos0, fox, lam_init):
    if fox:
        q_ref, k_ref, vt_ref, cq_ref, ck_ref, o_ref, m_sc, l_sc, acc_sc, s_sc = refs
    else:
        q_ref, k_ref, vt_ref, lp_ref, sg_ref, o_ref, m_sc, l_sc, acc_sc, s_sc = refs
    i = pl.program_id(2)
    q0 = pos0 + i * tq
    shift = CHUNK.bit_length() - 1
    lo = lax.broadcasted_iota(jnp.int32, (1, LANES), 1) < HALF
    spb = q_ref.shape[2] // LANES
    qm = []
    for sl in range(spb):
        q = q_ref[0, :, sl * LANES:(sl + 1) * LANES]
        qm += [jnp.where(lo, q, jnp.zeros_like(q)), jnp.where(lo, jnp.zeros_like(q), q)]
    if fox:
        n_full = (q0 + 1) // tk
        n_need = (q0 + tq - 1) // tk + 1
    else:
        n_full = (((q0 >> shift) + 1) << shift) // tk
        n_need = (((((q0 + tq - 1) >> shift) + 1) << shift) + tk - 1) // tk
    n_need = jnp.minimum(n_need, nk_total)
    n_full = jnp.minimum(jnp.minimum(n_full, kv_len // tk), n_need)

    m_sc[...] = jnp.full_like(m_sc, NEG_INF)
    l_sc[...] = jnp.zeros_like(l_sc)
    acc_sc[...] = jnp.zeros_like(acc_sc)

    n_maps = 2 * spb

    def scores(g, j):
        k0 = j * tk if isinstance(j, int) else pl.multiple_of(j * tk, tk)
        lanes = slice((g // 2) * LANES, (g // 2 + 1) * LANES)
        hk = tk // 2
        return jnp.concatenate([_mm_nt(k_ref[0, pl.ds(k0 + r, hk), lanes], qm[g]) for r in (0, hk)], axis=0)

    assert AHEAD < n_maps
    for a in range(AHEAD):
        s_sc[a] = scores(a, 0)

    hq, hk = tq // 2, tk // 2
    diag = (tq == tk and pos0 % tk == 0 and kv_len == nk_total * tk and hq % LANES == 0 and hk % CHUNK == 0)

    def right_half(full, extra, op):
        return jnp.concatenate([full[:, :hq], op(full[:, hq:], extra)], axis=1)

    def tile(j, masked):
        k0 = pl.multiple_of(j * tk, tk)
        split = masked and diag
        if split:
            kr = lax.broadcasted_iota(jnp.int32, (hk, hq), 0)
            qr = lax.broadcasted_iota(jnp.int32, (hk, hq), 1)
            tri = (qr >= kr) if fox else ((qr >> shift) >= (kr >> shift))
        elif masked:
            kpos = k0 + lax.broadcasted_iota(jnp.int32, (tk, tq), 0)
            qpos = q0 + lax.broadcasted_iota(jnp.int32, (tk, tq), 1)
            vis = (qpos >= kpos) if fox else ((qpos >> shift) >= (kpos >> shift))
            if kv_len < nk_total * tk:
                vis = vis & (kpos < kv_len)
        for g in range(n_maps):
            sl, c = divmod(g, 2)
            dj, g_ahead = divmod(g + AHEAD, n_maps)
            s_sc[g_ahead] = scores(g_ahead, jnp.minimum(j + dj, nk_total - 1) if dj else j)
            ck = ck_ref[0, pl.ds(k0, tk), g * LANES:(g + 1) * LANES] if fox else None

            def biased(rows, cols):
                blk = s_sc[g, rows, cols]
                return blk - jnp.concatenate([ck[rows]] * (blk.shape[1] // LANES), axis=1) if fox else blk

            if split:
                s = biased(slice(0, hk), slice(0, tq))
                s = jnp.concatenate([jnp.where(tri, s[:, :hq], NEG_INF), s[:, hq:]], axis=1)
                s_late = jnp.where(tri, biased(slice(hk, tk), slice(hq, tq)), NEG_INF)
            else:
                s = biased(slice(0, tk), slice(0, tq))
                if masked:
                    s = jnp.where(vis, s, NEG_INF)
            m_prev = m_sc[g]
            m_tile = jnp.max(s, axis=0, keepdims=True)
            if split:
                m_tile = right_half(m_tile, jnp.max(s_late, axis=0, keepdims=True), jnp.maximum)
            if fox:
                cq = cq_ref[0, sl, c:c + 1, :]
                m_new = jnp.maximum(m_prev, m_tile + cq)
                m_sub = m_new - cq
            else:
                m_new = m_sub = jnp.maximum(m_prev, m_tile)
            p = jnp.exp2(s - m_sub)
            p_sum = jnp.sum(p, axis=0, keepdims=True)
            alpha = jnp.exp2(m_prev - m_new)
            vt = vt_ref[0, sl, j]
            r = slice(c * HALF, (c + 1) * HALF) if fox else slice(0, LANES)
            if split:
                p_late = jnp.exp2(s_late - m_sub[:, hq:])
                p_sum = right_half(p_sum, jnp.sum(p_late, axis=0, keepdims=True), jnp.add)
                pv = right_half(_mm(vt[r, :hk], p.astype(BF16)), _mm(vt[r, hk:], p_late.astype(BF16)), jnp.add)
            else:
                pv = _mm(vt[r, :], p.astype(BF16))
            l_sc[g] = alpha * l_sc[g] + p_sum
            m_sc[g] = m_new
            if fox:
                acc_sc[sl, r, :] = alpha * acc_sc[sl, r, :] + pv
            else:
                acc_sc[g] = alpha * acc_sc[g] + pv

    def full_body(j, carry):
        tile(j, False)
        return carry

    def masked_body(j, carry):
        tile(j, True)
        return carry

    lax.fori_loop(0, n_full, full_body, 0)
    lax.fori_loop(n_full, n_need, masked_body, 0)

    for sl in range(spb):
        a, b = 2 * sl, 2 * sl + 1
        if fox:
            ot = jnp.concatenate([acc_sc[sl, 0:HALF, :] / l_sc[a], acc_sc[sl, HALF:LANES, :] / l_sc[b]], axis=0)
            o = ot.T
        else:
            lp = lp_ref[...]
            lam = (jnp.exp(jnp.sum(lp[0:1] * lp[1:2], axis=-1, keepdims=True))
                   - jnp.exp(jnp.sum(lp[2:3] * lp[3:4], axis=-1, keepdims=True)) + lam_init)
            ot = acc_sc[a] / l_sc[a] - lam * (acc_sc[b] / l_sc[b])
            ot = ot * lax.rsqrt(jnp.mean(ot * ot, axis=0, keepdims=True) + EPS)
            o = (ot.T * sg_ref[...]) * (1.0 - lam_init)
        o_ref[0, :, sl * LANES:(sl + 1) * LANES] = o.astype(BF16)


def _attention(q, k, v, extra, *, pos0, kv_len, fox, lam_init=0.0, tq=ATTN_TQ, tk=ATTN_TK, spb=4):
    bsz, Tq, w = q.shape
    Tk = k.shape[1]
    ns = w // LANES
    Tq_pad = -(-Tq // LANES) * LANES
    tq = _pick_tile(Tq_pad, tq)
    tk = tk if Tk % tk == 0 else Tk
    nk = Tk // tk
    q = jnp.pad(q, ((0, 0), (0, Tq_pad - Tq), (0, 0)))
    vt = v if v.ndim == 5 else v.reshape(bsz, nk, tk, ns, LANES).transpose(0, 3, 1, 4, 2)
    assert vt.shape == (bsz, ns, nk, LANES, tk)
    assert ns % spb == 0
    sw = spb * LANES
    qspec = pl.BlockSpec((1, tq, sw), lambda b, s, i: (b, i, s))
    kspec = pl.BlockSpec((1, Tk, sw), lambda b, s, i: (b, 0, s))
    vspec = pl.BlockSpec((1, spb, nk, LANES, tk), lambda b, s, i: (b, s, 0, 0, 0))
    if fox:
        cq, ck = extra
        extra = (jnp.pad(cq, ((0, 0), (0, 0), (0, 0), (0, Tq_pad - Tq))), ck)
        especs = [pl.BlockSpec((1, spb, 2, tq), lambda b, s, i: (b, s, 0, i)),
                  pl.BlockSpec((1, Tk, 2 * sw), lambda b, s, i: (b, 0, s), pipeline_mode=pl.Buffered(1))]
    else:
        lp, sg = extra
        especs = [pl.BlockSpec(lp.shape, lambda b, s, i: (0, 0)), pl.BlockSpec(sg.shape, lambda b, s, i: (0, 0))]
    out = pl.pallas_call(
        functools.partial(_attn_kernel, tq=tq, tk=tk, nk_total=nk, kv_len=kv_len, pos0=pos0, fox=fox,
                          lam_init=lam_init),
        grid=(bsz, ns // spb, Tq_pad // tq),
        in_specs=[qspec, kspec, vspec] + especs,
        out_specs=qspec,
        out_shape=jax.ShapeDtypeStruct((bsz, Tq_pad, w), BF16),
        scratch_shapes=[pltpu.VMEM((2 * spb, 1, tq), F32), pltpu.VMEM((2 * spb, 1, tq), F32),
                        pltpu.VMEM((spb if fox else 2 * spb, LANES, tq), F32), pltpu.VMEM((2 * spb, tk, tq), F32)],
        compiler_params=_params("parallel", "parallel", "arbitrary"),
        name="fox_attention" if fox else "diff_attention",
    )(q, k, vt, *extra)
    return out[:, :Tq]


def _rope_table_kernel(inv_ref, c_ref, s1_ref, s2_ref, *, pos0):
    T = c_ref.shape[0]
    pos = (pos0 + lax.broadcasted_iota(jnp.int32, (T, LANES), 0)).astype(F32)
    ang = pos * inv_ref[...]
    lane = lax.broadcasted_iota(jnp.int32, (T, LANES), 1) % HALF
    sn = jnp.sin(ang)
    c_ref[...] = jnp.cos(ang)
    s1_ref[...] = jnp.where(lane < ROT_DIM // 2, -sn, 0.0)
    s2_ref[...] = jnp.where(lane >= ROT_DIM // 2, sn, 0.0)


def _rope_tables(T, pos0):
    half = ROT_DIM // 2
    inv = ROPE_THETA ** (-jnp.arange(half, dtype=F32) * (2.0 / ROT_DIM))
    lane = jnp.arange(LANES) % HALF
    inv_lane = jnp.where(lane < ROT_DIM, inv[lane % half], 0.0).reshape(1, LANES).astype(F32)
    tab = jax.ShapeDtypeStruct((T, LANES), F32)
    return pl.pallas_call(
        functools.partial(_rope_table_kernel, pos0=pos0),
        out_shape=[tab, tab, tab],
        name="rope_tables",
    )(inv_lane)


def _inproj_odd_kernel(*refs, w, vt_tiles):
    x_ref, g_ref, w_ref, c_ref, s1_ref, s2_ref = refs[:6]
    q_ref, k_ref, v_ref, kb_ref, vx_ref = refs[-5:]
    hn = _rms(x_ref[...], g_ref[...]).astype(BF16)
    cs, s1, s2 = c_ref[...], s1_ref[...], s2_ref[...]
    half = ROT_DIM // 2

    def rope(t):
        out = []
        for h in range(w // LANES):
            s = t[:, h * LANES:(h + 1) * LANES]
            out.append(s * cs + pltpu.roll(s, LANES - half, 1) * s1 + pltpu.roll(s, half, 1) * s2)
        return jnp.concatenate(out, axis=1)

    q_ref[...] = (rope(_mm(hn, w_ref[:, 0:w])) * Q_SCALE).astype(BF16)
    _emit_kv(rope(_mm(hn, w_ref[:, w:2 * w])), _mm(hn, w_ref[:, 2 * w:3 * w]), k_ref, v_ref, kb_ref, vx_ref,
             DIFF_HEADS, vt_tiles)


def _inproj_odd(x2, g, w_in, tabs, tab_blocks, tm, layer, stacks, n_layers, vt_tiles_per_seq):
    n, d = x2.shape
    w = w_in.shape[1] // 3
    row = lambda c: pl.BlockSpec((tm, c), lambda i: (i, 0))
    tab = pl.BlockSpec((tm, LANES), lambda i: (i % tab_blocks, 0))
    kv_specs, kv_shapes, extra_in = _kv_out(n, w, DIFF_HEADS, tm, layer, stacks, n_layers, vt_tiles_per_seq)
    return pl.pallas_call(
        functools.partial(_inproj_odd_kernel, w=w, vt_tiles=bool(vt_tiles_per_seq)),
        grid=(n // tm,),
        in_specs=[row(d), _resident((1, d)), _resident((d, 3 * w)), tab, tab, tab]
        + [pl.BlockSpec(memory_space=pl.ANY)] * len(extra_in),
        out_specs=[row(w)] + kv_specs,
        out_shape=[jax.ShapeDtypeStruct((n, w), BF16)] + kv_shapes,
        input_output_aliases={6 + a: 1 + a for a in range(len(extra_in))},
        compiler_params=_params("parallel"),
        name="inproj_odd",
    )(x2, g, w_in, *tabs, *extra_in)


def _history_kv_kernel(ck_ref, cv_ref, kn_ref, vn_ref, k_ref, vt_ref, v_sc, *, time_minor):
    w, tp = k_ref.shape[2], k_ref.shape[1]
    new = kn_ref.shape[1]
    past = ck_ref.shape[4] if time_minor else ck_ref.shape[2]
    slots = (w // LANES, LANES)
    k_ref[0, past:past + new, :] = kn_ref[0]
    k_ref[0, past + new:tp, :] = jnp.zeros((tp - past - new, w), BF16)
    if time_minor:
        k_ref[0, 0:past, :] = ck_ref[0, 0].reshape(w, past).T.astype(BF16)
        vt_ref[0, :, 0, :, 0:past] = cv_ref[0, 0].reshape(*slots, past).astype(BF16)
        v_sc[0:new, :] = vn_ref[0].astype(F32)
        v_sc[new:tp - past, :] = jnp.zeros((tp - past - new, w), F32)
        vt_ref[0, :, 0, :, past:tp] = v_sc[...].T.astype(BF16).reshape(*slots, tp - past)
    else:
        k_ref[0, 0:past, :] = ck_ref[0, 0].reshape(past, w).astype(BF16)
        v_sc[0:past, :] = cv_ref[0, 0].reshape(past, w)
        v_sc[past:past + new, :] = vn_ref[0].astype(F32)
        v_sc[past + new:tp, :] = jnp.zeros((tp - past - new, w), F32)
        vt_ref[0, :, 0] = v_sc[...].T.astype(BF16).reshape(*slots, tp)


def _history_kv(cache_k, cache_v, layer, k_new, v_new, tp):
    _, bsz, past, heads, dh = cache_k.shape
    T, w = k_new.shape[1:]
    time_minor = dh < LANES and past % LANES == 0 and (tp - past) % LANES == 0
    if time_minor:
        cache_k, cache_v = (jnp.transpose(c, (0, 1, 3, 4, 2)) for c in (cache_k, cache_v))
    cspec = pl.BlockSpec((1, 1) + cache_k.shape[2:], lambda b: (layer, b, 0, 0, 0))
    nspec = pl.BlockSpec((1, T, w), lambda b: (b, 0, 0))
    return pl.pallas_call(
        functools.partial(_history_kv_kernel, time_minor=time_minor),
        grid=(bsz,),
        in_specs=[cspec, cspec, nspec, nspec],
        out_specs=[pl.BlockSpec((1, tp, w), lambda b: (b, 0, 0)),
                   pl.BlockSpec((1, w // LANES, 1, LANES, tp), lambda b: (b, 0, 0, 0, 0))],
        out_shape=[jax.ShapeDtypeStruct((bsz, tp, w), BF16),
                   jax.ShapeDtypeStruct((bsz, w // LANES, 1, LANES, tp), BF16)],
        scratch_shapes=[pltpu.VMEM((tp - past if time_minor else tp, w), F32)],
        compiler_params=_params("parallel"),
        name="history_kv",
    )(cache_k, cache_v, k_new, v_new)


def _trunk(x, pos0, caches, wts, tm):
    bsz, T, d = x.shape
    n = bsz * T
    depth = wts["norm_g"].shape[0]
    w = d // 2
    x2 = x.reshape(n, d)
    tt = _pick_tile(T, 256)
    flat = lambda a: a.reshape(n, a.shape[-1])
    seq = lambda a: a.reshape(bsz, T, a.shape[-1])
    n_keys = T if caches is None else T + caches[0].shape[2]
    key_pad = -n_keys % LANES
    pad_keys = lambda a: jnp.pad(a, ((0, 0), (0, key_pad), (0, 0))) if key_pad else a
    rope = None
    n_even, n_odd = (depth + 1) // 2, depth // 2
    vt_tiles_per_seq = T // tm if caches is None and tm == ATTN_TK and T % tm == 0 else 0
    fox_kv = diff_kv = None
    even, odd = [], []
    for l in range(depth):
        g = wts["norm_g"][l].reshape(6, 1, d)
        x2 = _ffn(x2, g[0], g[1], wts["w_ffn_in"], wts["w_ffn_out"], (l, 0), tm)
        if l % 2 == 0:
            e = l // 2
            u, gate, q, logf, logf1, *fox_kv, kb, vx = _inproj_even(
                x2, g[2], wts["w_in_even"][e], wts["b_fox_f"][e], tm, e, fox_kv, n_even, vt_tiles_per_seq)
            if caches is None:
                h0 = jnp.zeros((bsz, 1, w), F32)
                buf0 = jnp.zeros((bsz, 8, w), F32)
                kb_all, logf_all = seq(kb), seq(logf)
                vb_all = vx if vt_tiles_per_seq else seq(vx)
            else:
                flogf, lru_h, lru_conv = (c[e] for c in caches[2:5])
                h0 = lru_h.reshape(bsz, 1, w)
                buf0 = jnp.pad(lru_conv, ((0, 0), (8 - (CONV_W - 1), 0), (0, 0)))
                kb_all, vb_all = _history_kv(caches[0], caches[1], e, seq(kb), seq(vx), n_keys + key_pad)
                logf_all = jnp.concatenate([jnp.repeat(flogf, HALF, axis=-1), seq(logf)], axis=1)
            ya, h_last, nbuf = _lru(seq(u), seq(gate), h0, buf0, wts["lru_conv_w"][e], wts["lru_conv_b"][e],
                                    wts["lru_wr"][e], wts["lru_wi"][e], wts["lru_br"][e], wts["lru_bi"][e],
                                    wts["lru_lambda"][e], tt)
            Tk = logf_all.shape[1]
            Tp = Tk + key_pad
            ck, cq = _cumsum(pad_keys(logf_all), Tp if Tp <= 2 * CUMSUM_TILE else _pick_tile(Tp, CUMSUM_TILE))
            yb = _attention(seq(q), kb_all, vb_all, (cq[..., Tk - T:Tk], ck), pos0=pos0, kv_len=Tk, fox=True)
            mix = (g[3], wts["w_out_even"][e], [flat(ya), flat(yb)])
            even.append((seq(logf1)[..., :FOX_HEADS], h_last.reshape(bsz, w), nbuf))
        else:
            o = l // 2
            lam_init = 0.8 - 0.6 * math.exp(-0.3 * l)
            if rope is None:
                tabs = _rope_tables(T, pos0)
                if T % tm:
                    tabs = [jnp.tile(t, (n // T, 1)) for t in tabs]
                rope = (tabs, tabs[0].shape[0] // tm)
            q, *diff_kv, kb, vx = _inproj_odd(x2, g[2], wts["w_in_odd"][o], rope[0], rope[1], tm, o, diff_kv,
                                              n_odd, vt_tiles_per_seq)
            if caches is None:
                kb_all = seq(kb)
                vb_all = vx if vt_tiles_per_seq else seq(vx)
            else:
                kb_all, vb_all = _history_kv(caches[5], caches[6], o, seq(kb), seq(vx), n_keys + key_pad)
            y = _attention(seq(q), kb_all, vb_all, (wts["diff_lambda"][o], wts["diff_subln_g"][o]),
                           pos0=pos0, kv_len=n_keys, fox=False, lam_init=lam_init)
            mix = (g[3], wts["w_out_odd"][o], [flat(y)])
        x2 = _ffn(x2, g[4], g[5], wts["w_ffn_in"], wts["w_ffn_out"], (l, 1), tm, mix)
    per_seq = lambda a: (jnp.transpose(a, (0, 1, 4, 2, 3)) if a.ndim == 5
                         else a.reshape(a.shape[0], bsz, T, *a.shape[2:]))
    even_out = [per_seq(a) for a in fox_kv] + [jnp.stack(z) for z in zip(*even)]
    return x2.reshape(bsz, T, d), even_out, [per_seq(a) for a in diff_kv]


def _prepare_weights(norm_g, w_ffn_in, w_ffn_out, w_in_even, b_fox_f, lru_conv_w, lru_conv_b, lru_w_gates,
                     lru_b_gates, lru_lambda, w_out_even, w_in_odd, diff_lambda, diff_subln_g, w_out_odd):
    n_even, d, _ = w_in_even.shape
    w = d // 2
    blk = w // LRU_BLOCKS
    wf = jnp.repeat(w_in_even[:, :, 5 * w:], HALF, axis=-1)
    pad_lanes = lambda a: jnp.pad(a, [(0, 0)] * (a.ndim - 1) + [(0, LANES - a.shape[-1])])
    eye = jnp.eye(LRU_BLOCKS, dtype=F32)
    dense = lambda g: jnp.einsum("encg,nm->encmg", g, eye).reshape(n_even, w, w)
    return dict(
        norm_g=norm_g, w_ffn_in=w_ffn_in.astype(BF16), w_ffn_out=w_ffn_out.astype(BF16),
        w_in_even=jnp.concatenate([w_in_even[:, :, :5 * w], wf, pad_lanes(w_in_even[:, :, 5 * w:])],
                                  axis=-1).astype(BF16),
        b_fox_f=jnp.concatenate([jnp.repeat(b_fox_f, HALF, axis=-1), pad_lanes(b_fox_f)],
                                axis=-1).reshape(n_even, 1, w + LANES),
        lru_conv_w=lru_conv_w, lru_conv_b=lru_conv_b.reshape(n_even, 1, w),
        lru_wr=dense(lru_w_gates[..., :blk]).astype(BF16), lru_wi=dense(lru_w_gates[..., blk:]).astype(BF16),
        lru_br=lru_b_gates[..., :blk].reshape(n_even, 1, w), lru_bi=lru_b_gates[..., blk:].reshape(n_even, 1, w),
        lru_lambda=lru_lambda.reshape(n_even, 1, w), w_out_even=w_out_even.astype(BF16),
        w_in_odd=w_in_odd.astype(BF16), diff_lambda=diff_lambda, diff_subln_g=diff_subln_g.reshape(-1, 1, LANES),
        w_out_odd=w_out_odd.astype(BF16))


def kernel(x_prompt, x_sample, cache_fox_k, cache_fox_v, cache_fox_logf, state_lru_h, state_lru_conv, cache_diff_k, cache_diff_v, norm_g, w_ffn_in, w_ffn_out, w_in_even, b_fox_f, lru_conv_w, lru_conv_b, lru_w_gates, lru_b_gates, lru_lambda, w_out_even, w_in_odd, diff_lambda, diff_subln_g, w_out_odd):
    wts = _prepare_weights(norm_g, w_ffn_in, w_ffn_out, w_in_even, b_fox_f, lru_conv_w, lru_conv_b, lru_w_gates,
                           lru_b_gates, lru_lambda, w_out_even, w_in_odd, diff_lambda, diff_subln_g, w_out_odd)
    y_p, pe, po = _trunk(x_prompt, 0, None, wts, _pick_tile(x_prompt.shape[0] * x_prompt.shape[1], 512))
    caches = (cache_fox_k, cache_fox_v, cache_fox_logf, state_lru_h, state_lru_conv, cache_diff_k, cache_diff_v)
    y_s, se, so = _trunk(x_sample, cache_fox_k.shape[2], caches, wts,
                         _pick_tile(x_sample.shape[0] * x_sample.shape[1], 256))
    return (y_p, y_s, *pe, *po, *se, *so)
```

```python
import functools
import math

import jax
import jax.numpy as jnp
from jax import lax
from jax.experimental import pallas as pl
from jax.experimental.pallas import tpu as pltpu

F32 = jnp.float32
BF16 = jnp.bfloat16

EPS = 1e-6
NEG_INF = -1e30
LRU_C = 8.0
LRU_BLOCKS = 8
CONV_W = 4
FOX_HEADS = 8
DIFF_HEADS = 8
CHUNK = 64
ROT_DIM = 16
ROPE_THETA = 500000.0

LANES = 128
SUBLANES = 8
HALF = LANES // 2
FF_CHUNK = 256
AHEAD = 2
ATTN_TQ, ATTN_TK = 512, 512
CUMSUM_TILE = 1024
LOG2E = 1.0 / math.log(2.0)
Q_SCALE = HALF ** -0.5 * LOG2E
VMEM_LIMIT = 56 * 1024 * 1024


def _params(*sem):
    return pltpu.CompilerParams(dimension_semantics=sem, vmem_limit_bytes=VMEM_LIMIT)


def _resident(shape, lead=()):
    nd = len(shape)
    return pl.BlockSpec((None,) * len(lead) + tuple(shape), lambda *_: tuple(lead) + (0,) * nd,
                        pipeline_mode=pl.Buffered(1))


def _rms(x, g):
    return x * lax.rsqrt(jnp.mean(x * x, axis=-1, keepdims=True) + EPS) * g


def _mm(a, b):
    return jnp.dot(a, b, preferred_element_type=F32)


def _mm_nt(a, b):
    return lax.dot_general(a, b, (((1,), (1,)), ((), ())), preferred_element_type=F32)


def _log_sigmoid(x):
    return jnp.minimum(x, 0.0) - jnp.log1p(jnp.exp(-jnp.abs(x)))


def _softplus(x):
    return jnp.maximum(x, 0.0) + jnp.log1p(jnp.exp(-jnp.abs(x)))


def _pick_tile(n, target):
    t = min(n, target)
    while n % t:
        t //= 2
    return t


def _ffn_kernel(x_ref, gpre_ref, gpost_ref, wi_ref, wo_ref, *rest, dff):
    *mix, o_ref, act_ref = rest
    x = x_ref[...]
    if mix:
        gmix_ref, wmix_ref, *y_refs = mix
        y = jnp.concatenate([r[...] for r in y_refs], axis=1)
        x = x + _rms(_mm(y, wmix_ref[...]), gmix_ref[...])
    xn = _rms(x, gpre_ref[...]).astype(BF16)
    for c0 in range(0, dff, FF_CHUNK):
        g = _mm(xn, wi_ref[:, c0:c0 + FF_CHUNK])
        u = _mm(xn, wi_ref[:, dff + c0:dff + c0 + FF_CHUNK])
        act_ref[:, c0:c0 + FF_CHUNK] = ((g * jax.nn.sigmoid(g)) * u).astype(BF16)
    y = _mm(act_ref[...], wo_ref[...])
    o_ref[...] = x + 0.5 * _rms(y, gpost_ref[...])


def _ffn(x2, g_pre, g_post, w_in, w_out, which, tm, mix=None):
    n, d = x2.shape
    dff = w_out.shape[-2]
    assert dff % FF_CHUNK == 0 and n % tm == 0
    row = lambda c: pl.BlockSpec((tm, c), lambda i: (i, 0))
    mix_in, mix_specs = [], []
    if mix is not None:
        g_mix, w_mix, ys = mix
        mix_in = [g_mix, w_mix, *ys]
        mix_specs = [_resident((1, d)), _resident(w_mix.shape)] + [row(y.shape[1]) for y in ys]
    return pl.pallas_call(
        functools.partial(_ffn_kernel, dff=dff),
        grid=(n // tm,),
        in_specs=[row(d), _resident((1, d)), _resident((1, d)), _resident((d, 2 * dff), which),
                  _resident((dff, d), which)] + mix_specs,
        out_specs=row(d),
        out_shape=jax.ShapeDtypeStruct((n, d), F32),
        scratch_shapes=[pltpu.VMEM((tm, dff), BF16)],
        compiler_params=_params("parallel"),
        name="ffn_mix" if mix is not None else "ffn",
    )(x2, g_pre, g_post, w_in, w_out, *mix_in)


def _emit_kv(k, v, k_ref, v_ref, kb_ref, vx_ref, heads, vt_tiles):
    tm, w = k.shape
    vt = v.T if vt_tiles else None
    if k_ref.ndim == 5:
        k_out, v_out = (t.reshape(1, heads, w // heads, tm) for t in (k.T, vt))
    else:
        k_out, v_out = (t.reshape(tm, heads, w // heads) for t in (k, v))
    for l in range(k_ref.shape[0]):
        k_ref[l] = k_out if l == 0 else jnp.zeros_like(k_out)
        v_ref[l] = v_out if l == 0 else jnp.zeros_like(v_out)
    kb_ref[...] = k.astype(BF16)
    if vt_tiles:
        vx_ref[0, :, 0] = vt.astype(BF16).reshape(w // LANES, LANES, tm)
    else:
        vx_ref[...] = v.astype(BF16)


def _kv_out(n, w, heads, tm, layer, stacks, n_layers, vt_tiles_per_seq):
    dh = w // heads
    assert (stacks is None) == (layer == 0)
    own_layers = n_layers if stacks is None else 1
    nkb = vt_tiles_per_seq
    if nkb and dh < LANES:
        stacked = pl.BlockSpec((own_layers, 1, heads, dh, tm), lambda i: (layer, i // nkb, 0, 0, i % nkb))
        stacked_shape = jax.ShapeDtypeStruct((n_layers, n // (nkb * tm), heads, dh, nkb * tm), F32)
    else:
        stacked = pl.BlockSpec((own_layers, tm, heads, dh), lambda i: (layer, i, 0, 0))
        stacked_shape = jax.ShapeDtypeStruct((n_layers, n, heads, dh), F32)
    row = pl.BlockSpec((tm, w), lambda i: (i, 0))
    if vt_tiles_per_seq:
        vx = pl.BlockSpec((1, w // LANES, 1, LANES, tm), lambda i: (i // nkb, 0, i % nkb, 0, 0))
        vx_shape = jax.ShapeDtypeStruct((n // (nkb * tm), w // LANES, nkb, LANES, tm), BF16)
    else:
        vx, vx_shape = row, jax.ShapeDtypeStruct((n, w), BF16)
    specs = [stacked, stacked, row, vx]
    shapes = [stacked_shape, stacked_shape, jax.ShapeDtypeStruct((n, w), BF16), vx_shape]
    extra_in = [] if stacks is None else list(stacks)
    return specs, shapes, extra_in


def _inproj_even_kernel(*refs, w, vt_tiles):
    x_ref, g_ref, w_ref, bf_ref = refs[:4]
    u_ref, gate_ref, q_ref, logf_ref, logf1_ref, k_ref, v_ref, kb_ref, vx_ref = refs[-9:]
    hn = _rms(x_ref[...], g_ref[...]).astype(BF16)
    u_ref[...] = _mm(hn, w_ref[:, 0:w])
    gate_ref[...] = _mm(hn, w_ref[:, w:2 * w])
    q_ref[...] = (_mm(hn, w_ref[:, 2 * w:3 * w]) * Q_SCALE).astype(BF16)
    logf_ref[...] = _log_sigmoid(_mm(hn, w_ref[:, 5 * w:6 * w]) + bf_ref[:, 0:w])
    logf1_ref[...] = _log_sigmoid(_mm(hn, w_ref[:, 6 * w:6 * w + LANES]) + bf_ref[:, w:w + LANES])
    _emit_kv(_mm(hn, w_ref[:, 3 * w:4 * w]), _mm(hn, w_ref[:, 4 * w:5 * w]), k_ref, v_ref, kb_ref, vx_ref,
             FOX_HEADS, vt_tiles)


def _inproj_even(x2, g, w_wide, bf_wide, tm, layer, stacks, n_layers, vt_tiles_per_seq):
    n, d = x2.shape
    w = (w_wide.shape[1] - LANES) // 6
    row = lambda c: pl.BlockSpec((tm, c), lambda i: (i, 0))
    f = jax.ShapeDtypeStruct((n, w), F32)
    b = jax.ShapeDtypeStruct((n, w), BF16)
    kv_specs, kv_shapes, extra_in = _kv_out(n, w, FOX_HEADS, tm, layer, stacks, n_layers, vt_tiles_per_seq)
    return pl.pallas_call(
        functools.partial(_inproj_even_kernel, w=w, vt_tiles=bool(vt_tiles_per_seq)),
        grid=(n // tm,),
        in_specs=[row(d), _resident((1, d)), _resident(w_wide.shape), _resident(bf_wide.shape)]
        + [pl.BlockSpec(memory_space=pl.ANY)] * len(extra_in),
        out_specs=[row(w)] * 4 + [row(LANES)] + kv_specs,
        out_shape=[f, f, b, f, jax.ShapeDtypeStruct((n, LANES), F32)] + kv_shapes,
        input_output_aliases={4 + a: 5 + a for a in range(len(extra_in))},
        compiler_params=_params("parallel"),
        name="inproj_even",
    )(x2, g, w_wide, bf_wide, *extra_in)


def _scan_rows(a, b, h):
    tt, c = b.shape
    assert tt % SUBLANES == 0
    groups = (tt // SUBLANES, SUBLANES, c)
    b = b.reshape(groups)
    a = None if a is None else a.reshape(groups)
    sub = lax.broadcasted_iota(jnp.int32, groups, 1)
    d = 1
    while d < SUBLANES:
        keep = sub >= d
        if a is None:
            b = b + jnp.where(keep, pltpu.roll(b, d, 1), 0.0)
        else:
            b = a * jnp.where(keep, pltpu.roll(b, d, 1), 0.0) + b
            a = a * jnp.where(keep, pltpu.roll(a, d, 1), 1.0)
        d *= 2
    out = []
    for g in range(groups[0]):
        blk = b[g] + (h if a is None else a[g] * h)
        out.append(blk)
        h = blk[SUBLANES - 1:SUBLANES]
    return jnp.concatenate(out, axis=0)


def _lru_kernel(u_ref, gate_ref, h0_ref, buf0_ref, cw_ref, cb_ref, wr_ref, wi_ref, br_ref, bi_ref, lam_ref,
                ya_ref, hlast_ref, nbuf_ref, ubuf, hcar, *, tt):
    t = pl.program_id(1)
    pad = ubuf.shape[0] - tt

    @pl.when(t == 0)
    def _():
        ubuf[0:pad, :] = buf0_ref[0]
        hcar[...] = h0_ref[0]

    u = u_ref[0]
    ubuf[pad:pad + tt, :] = u
    uc = cb_ref[...]
    for j in range(CONV_W):
        o = pad - (CONV_W - 1) + j
        uc = uc + ubuf[o:o + tt, :] * cw_ref[j:j + 1, :]
    ucb = uc.astype(BF16)
    r = jax.nn.sigmoid(_mm(ucb, wr_ref[...]) + br_ref[...])
    ig = jax.nn.sigmoid(_mm(ucb, wi_ref[...]) + bi_ref[...])
    log_a = (-LRU_C * r) * _softplus(-lam_ref[...])
    a = jnp.exp(log_a)
    bx = jnp.sqrt(-jnp.tanh(log_a) * (a * a + 1.0)) * (ig * uc)
    bx = _scan_rows(a, bx, hcar[...])
    hcar[...] = bx[tt - 1:tt, :]
    ya_ref[0] = (bx * jax.nn.gelu(gate_ref[0])).astype(BF16)
    ubuf[0:pad, :] = ubuf[tt:tt + pad, :]

    @pl.when(t == pl.num_programs(1) - 1)
    def _():
        hlast_ref[0] = bx[tt - 1:tt, :]
        nbuf_ref[0] = ubuf[pad - (CONV_W - 1):pad, :]


def _lru(u, gate, h0, buf0, cw, cb, wr, wi, br, bi, lam, tt):
    bsz, T, w = u.shape
    pad = 8
    seq = pl.BlockSpec((1, tt, w), lambda b, t: (b, t, 0))
    per_b = lambda r: pl.BlockSpec((1, r, w), lambda b, t: (b, 0, 0))
    return pl.pallas_call(
        functools.partial(_lru_kernel, tt=tt),
        grid=(bsz, T // tt),
        in_specs=[seq, seq, per_b(1), per_b(pad), _resident((CONV_W, w)), _resident((1, w)), _resident((w, w)),
                  _resident((w, w)), _resident((1, w)), _resident((1, w)), _resident((1, w))],
        out_specs=[seq, per_b(1), per_b(CONV_W - 1)],
        out_shape=[jax.ShapeDtypeStruct((bsz, T, w), BF16), jax.ShapeDtypeStruct((bsz, 1, w), F32),
                   jax.ShapeDtypeStruct((bsz, CONV_W - 1, w), F32)],
        scratch_shapes=[pltpu.VMEM((tt + pad, w), F32), pltpu.VMEM((1, w), F32)],
        compiler_params=_params("parallel", "arbitrary"),
        name="conv_rglru",
    )(u, gate, h0, buf0, cw, cb, wr, wi, br, bi, lam)


def _cumsum_kernel(x_ref, ck_ref, cq_ref, car, *, tt):
    @pl.when(pl.program_id(1) == 0)
    def _():
        car[...] = jnp.zeros_like(car)

    x = _scan_rows(None, x_ref[0], car[...])
    car[...] = x[tt - 1:tt, :]
    x = x * LOG2E
    lo = lax.broadcasted_iota(jnp.int32, (1, LANES), 1) < HALF
    for s in range(x.shape[1] // LANES):
        xs = x[:, s * LANES:(s + 1) * LANES]
        sw = pltpu.roll(xs, HALF, 1)
        ck_ref[0, :, (2 * s) * LANES:(2 * s + 1) * LANES] = jnp.where(lo, xs, sw)
        ck_ref[0, :, (2 * s + 1) * LANES:(2 * s + 2) * LANES] = jnp.where(lo, sw, xs)
        tr = xs.T
        cq_ref[0, s, 0:1, :] = tr[0:1, :]
        cq_ref[0, s, 1:2, :] = tr[HALF:HALF + 1, :]


def _cumsum(x, tt):
    bsz, T, w = x.shape
    ns = w // LANES
    return pl.pallas_call(
        functools.partial(_cumsum_kernel, tt=tt),
        grid=(bsz, T // tt),
        in_specs=[pl.BlockSpec((1, tt, w), lambda b, t: (b, t, 0))],
        out_specs=[pl.BlockSpec((1, tt, 2 * w), lambda b, t: (b, t, 0)),
                   pl.BlockSpec((1, ns, 2, tt), lambda b, t: (b, 0, 0, t))],
        out_shape=[jax.ShapeDtypeStruct((bsz, T, 2 * w), F32), jax.ShapeDtypeStruct((bsz, ns, 2, T), F32)],
        scratch_shapes=[pltpu.VMEM((1, w), F32)],
        compiler_params=_params("parallel", "arbitrary"),
        name="logf_cumsum",
    )(x)


def _attn_kernel(*refs, tq, tk, nk_total, kv_len, pos0, fox, lam_init):
    if fox:
        q_ref, k_ref, vt_ref, cq_ref, ck_ref, o_ref, m_sc, l_sc, acc_sc, s_sc = refs
    else:
        q_ref, k_ref, vt_ref, lp_ref, sg_ref, o_ref, m_sc, l_sc, acc_sc, s_sc = refs
    i = pl.program_id(2)
    q0 = pos0 + i * tq
    shift = CHUNK.bit_length() - 1
    lo = lax.broadcasted_iota(jnp.int32, (1, LANES), 1) < HALF
    spb = q_ref.shape[2] // LANES
    qm = []
    for sl in range(spb):
        q = q_ref[0, :, sl * LANES:(sl + 1) * LANES]
        qm += [jnp.where(lo, q, jnp.zeros_like(q)), jnp.where(lo, jnp.zeros_like(q), q)]
    if fox:
        n_full = (q0 + 1) // tk
        n_need = (q0 + tq - 1) // tk + 1
    else:
        n_full = (((q0 >> shift) + 1) << shift) // tk
        n_need = (((((q0 + tq - 1) >> shift) + 1) << shift) + tk - 1) // tk
    n_need = jnp.minimum(n_need, nk_total)
    n_full = jnp.minimum(jnp.minimum(n_full, kv_len // tk), n_need)

    m_sc[...] = jnp.full_like(m_sc, NEG_INF)
    l_sc[...] = jnp.zeros_like(l_sc)
    acc_sc[...] = jnp.zeros_like(acc_sc)

    n_maps = 2 * spb

    def scores(g, j):
        k0 = j * tk if isinstance(j, int) else pl.multiple_of(j * tk, tk)
        lanes = slice((g // 2) * LANES, (g // 2 + 1) * LANES)
        hk = tk // 2

        def half(r):
            s = _mm_nt(k_ref[0, pl.ds(k0 + r, hk), lanes], qm[g])
            if fox:
                ck = ck_ref[0, pl.ds(k0 + r, hk), g * LANES:(g + 1) * LANES]
                s = s - jnp.concatenate([ck] * (tq // LANES), axis=1)
            return s

        return jnp.concatenate([half(0), half(hk)], axis=0)

    assert AHEAD < n_maps
    for a in range(AHEAD):
        s_sc[a] = scores(a, 0)

    hq, hk = tq // 2, tk // 2
    diag = (tq == tk and pos0 % tk == 0 and kv_len == nk_total * tk and hq % LANES == 0 and hk % CHUNK == 0)

    def right_half(full, extra, op):
        return jnp.concatenate([full[:, :hq], op(full[:, hq:], extra)], axis=1)

    def tile(j, masked):
        k0 = pl.multiple_of(j * tk, tk)
        split = masked and diag
        if split:
            kr = lax.broadcasted_iota(jnp.int32, (hk, hq), 0)
            qr = lax.broadcasted_iota(jnp.int32, (hk, hq), 1)
            tri = (qr >= kr) if fox else ((qr >> shift) >= (kr >> shift))
        elif masked:
            kpos = k0 + lax.broadcasted_iota(jnp.int32, (tk, tq), 0)
            qpos = q0 + lax.broadcasted_iota(jnp.int32, (tk, tq), 1)
            vis = (qpos >= kpos) if fox else ((qpos >> shift) >= (kpos >> shift))
            if kv_len < nk_total * tk:
                vis = vis & (kpos < kv_len)
        for g in range(n_maps):
            sl, c = divmod(g, 2)
            dj, g_ahead = divmod(g + AHEAD, n_maps)
            s_sc[g_ahead] = scores(g_ahead, jnp.minimum(j + dj, nk_total - 1) if dj else j)
            if split:
                s = s_sc[g, :hk, :]
                s = jnp.concatenate([jnp.where(tri, s[:, :hq], NEG_INF), s[:, hq:]], axis=1)
                s_late = jnp.where(tri, s_sc[g, hk:, hq:], NEG_INF)
            else:
                s = s_sc[g]
                if masked:
                    s = jnp.where(vis, s, NEG_INF)
            m_prev = m_sc[g]
            m_tile = jnp.max(s, axis=0, keepdims=True)
            if split:
                m_tile = right_half(m_tile, jnp.max(s_late, axis=0, keepdims=True), jnp.maximum)
            if fox:
                cq = cq_ref[0, sl, c:c + 1, :]
                m_new = jnp.maximum(m_prev, m_tile + cq)
                m_sub = m_new - cq
            else:
                m_new = m_sub = jnp.maximum(m_prev, m_tile)
            p = jnp.exp2(s - m_sub)
            p_sum = jnp.sum(p, axis=0, keepdims=True)
            alpha = jnp.exp2(m_prev - m_new)
            vt = vt_ref[0, sl, j]
            r = slice(c * HALF, (c + 1) * HALF) if fox else slice(0, LANES)
            if split:
                p_late = jnp.exp2(s_late - m_sub[:, hq:])
                p_sum = right_half(p_sum, jnp.sum(p_late, axis=0, keepdims=True), jnp.add)
                pv = right_half(_mm(vt[r, :hk], p.astype(BF16)), _mm(vt[r, hk:], p_late.astype(BF16)), jnp.add)
            else:
                pv = _mm(vt[r, :], p.astype(BF16))
            l_sc[g] = alpha * l_sc[g] + p_sum
            m_sc[g] = m_new
            if fox:
                acc_sc[sl, r, :] = alpha * acc_sc[sl, r, :] + pv
            else:
                acc_sc[g] = alpha * acc_sc[g] + pv

    def full_body(j, carry):
        tile(j, False)
        return carry

    def masked_body(j, carry):
        tile(j, True)
        return carry

    lax.fori_loop(0, n_full, full_body, 0)
    lax.fori_loop(n_full, n_need, masked_body, 0)

    for sl in range(spb):
        a, b = 2 * sl, 2 * sl + 1
        if fox:
            ot = jnp.concatenate([acc_sc[sl, 0:HALF, :] / l_sc[a], acc_sc[sl, HALF:LANES, :] / l_sc[b]], axis=0)
            o = ot.T
        else:
            lp = lp_ref[...]
            lam = (jnp.exp(jnp.sum(lp[0:1] * lp[1:2], axis=-1, keepdims=True))
                   - jnp.exp(jnp.sum(lp[2:3] * lp[3:4], axis=-1, keepdims=True)) + lam_init)
            ot = acc_sc[a] / l_sc[a] - lam * (acc_sc[b] / l_sc[b])
            ot = ot * lax.rsqrt(jnp.mean(ot * ot, axis=0, keepdims=True) + EPS)
            o = (ot.T * sg_ref[...]) * (1.0 - lam_init)
        o_ref[0, :, sl * LANES:(sl + 1) * LANES] = o.astype(BF16)


def _attention(q, k, v, extra, *, pos0, kv_len, fox, lam_init=0.0, tq=ATTN_TQ, tk=ATTN_TK, spb=4):
    bsz, Tq, w = q.shape
    Tk = k.shape[1]
    ns = w // LANES
    Tq_pad = -(-Tq // LANES) * LANES
    tq = _pick_tile(Tq_pad, tq)
    tk = tk if Tk % tk == 0 else Tk
    nk = Tk // tk
    q = jnp.pad(q, ((0, 0), (0, Tq_pad - Tq), (0, 0)))
    vt = v if v.ndim == 5 else v.reshape(bsz, nk, tk, ns, LANES).transpose(0, 3, 1, 4, 2)
    assert vt.shape == (bsz, ns, nk, LANES, tk)
    assert ns % spb == 0
    sw = spb * LANES
    qspec = pl.BlockSpec((1, tq, sw), lambda b, s, i: (b, i, s))
    kspec = pl.BlockSpec((1, Tk, sw), lambda b, s, i: (b, 0, s))
    vspec = pl.BlockSpec((1, spb, nk, LANES, tk), lambda b, s, i: (b, s, 0, 0, 0))
    if fox:
        cq, ck = extra
        extra = (jnp.pad(cq, ((0, 0), (0, 0), (0, 0), (0, Tq_pad - Tq))), ck)
        especs = [pl.BlockSpec((1, spb, 2, tq), lambda b, s, i: (b, s, 0, i)),
                  pl.BlockSpec((1, Tk, 2 * sw), lambda b, s, i: (b, 0, s), pipeline_mode=pl.Buffered(1))]
    else:
        lp, sg = extra
        especs = [pl.BlockSpec(lp.shape, lambda b, s, i: (0, 0)), pl.BlockSpec(sg.shape, lambda b, s, i: (0, 0))]
    out = pl.pallas_call(
        functools.partial(_attn_kernel, tq=tq, tk=tk, nk_total=nk, kv_len=kv_len, pos0=pos0, fox=fox,
                          lam_init=lam_init),
        grid=(bsz, ns // spb, Tq_pad // tq),
        in_specs=[qspec, kspec, vspec] + especs,
        out_specs=qspec,
        out_shape=jax.ShapeDtypeStruct((bsz, Tq_pad, w), BF16),
        scratch_shapes=[pltpu.VMEM((2 * spb, 1, tq), F32), pltpu.VMEM((2 * spb, 1, tq), F32),
                        pltpu.VMEM((spb if fox else 2 * spb, LANES, tq), F32), pltpu.VMEM((2 * spb, tk, tq), F32)],
        compiler_params=_params("parallel", "parallel", "arbitrary"),
        name="fox_attention" if fox else "diff_attention",
    )(q, k, vt, *extra)
    return out[:, :Tq]


def _rope_table_kernel(inv_ref, c_ref, s1_ref, s2_ref, *, pos0):
    T = c_ref.shape[0]
    pos = (pos0 + lax.broadcasted_iota(jnp.int32, (T, LANES), 0)).astype(F32)
    ang = pos * inv_ref[...]
    lane = lax.broadcasted_iota(jnp.int32, (T, LANES), 1) % HALF
    sn = jnp.sin(ang)
    c_ref[...] = jnp.cos(ang)
    s1_ref[...] = jnp.where(lane < ROT_DIM // 2, -sn, 0.0)
    s2_ref[...] = jnp.where(lane >= ROT_DIM // 2, sn, 0.0)


def _rope_tables(T, pos0):
    half = ROT_DIM // 2
    inv = ROPE_THETA ** (-jnp.arange(half, dtype=F32) * (2.0 / ROT_DIM))
    lane = jnp.arange(LANES) % HALF
    inv_lane = jnp.where(lane < ROT_DIM, inv[lane % half], 0.0).reshape(1, LANES).astype(F32)
    tab = jax.ShapeDtypeStruct((T, LANES), F32)
    return pl.pallas_call(
        functools.partial(_rope_table_kernel, pos0=pos0),
        out_shape=[tab, tab, tab],
        name="rope_tables",
    )(inv_lane)


def _inproj_odd_kernel(*refs, w, vt_tiles):
    x_ref, g_ref, w_ref, c_ref, s1_ref, s2_ref = refs[:6]
    q_ref, k_ref, v_ref, kb_ref, vx_ref = refs[-5:]
    hn = _rms(x_ref[...], g_ref[...]).astype(BF16)
    cs, s1, s2 = c_ref[...], s1_ref[...], s2_ref[...]
    half = ROT_DIM // 2

    def rope(t):
        out = []
        for h in range(w // LANES):
            s = t[:, h * LANES:(h + 1) * LANES]
            out.append(s * cs + pltpu.roll(s, LANES - half, 1) * s1 + pltpu.roll(s, half, 1) * s2)
        return jnp.concatenate(out, axis=1)

    q_ref[...] = (rope(_mm(hn, w_ref[:, 0:w])) * Q_SCALE).astype(BF16)
    _emit_kv(rope(_mm(hn, w_ref[:, w:2 * w])), _mm(hn, w_ref[:, 2 * w:3 * w]), k_ref, v_ref, kb_ref, vx_ref,
             DIFF_HEADS, vt_tiles)


def _inproj_odd(x2, g, w_in, tabs, tab_blocks, tm, layer, stacks, n_layers, vt_tiles_per_seq):
    n, d = x2.shape
    w = w_in.shape[1] // 3
    row = lambda c: pl.BlockSpec((tm, c), lambda i: (i, 0))
    tab = pl.BlockSpec((tm, LANES), lambda i: (i % tab_blocks, 0))
    kv_specs, kv_shapes, extra_in = _kv_out(n, w, DIFF_HEADS, tm, layer, stacks, n_layers, vt_tiles_per_seq)
    return pl.pallas_call(
        functools.partial(_inproj_odd_kernel, w=w, vt_tiles=bool(vt_tiles_per_seq)),
        grid=(n // tm,),
        in_specs=[row(d), _resident((1, d)), _resident((d, 3 * w)), tab, tab, tab]
        + [pl.BlockSpec(memory_space=pl.ANY)] * len(extra_in),
        out_specs=[row(w)] + kv_specs,
        out_shape=[jax.ShapeDtypeStruct((n, w), BF16)] + kv_shapes,
        input_output_aliases={6 + a: 1 + a for a in range(len(extra_in))},
        compiler_params=_params("parallel"),
        name="inproj_odd",
    )(x2, g, w_in, *tabs, *extra_in)


def _history_kv_kernel(ck_ref, cv_ref, kn_ref, vn_ref, k_ref, vt_ref, v_sc, *, time_minor):
    w, tp = k_ref.shape[2], k_ref.shape[1]
    new = kn_ref.shape[1]
    past = ck_ref.shape[4] if time_minor else ck_ref.shape[2]
    slots = (w // LANES, LANES)
    k_ref[0, past:past + new, :] = kn_ref[0]
    k_ref[0, past + new:tp, :] = jnp.zeros((tp - past - new, w), BF16)
    if time_minor:
        k_ref[0, 0:past, :] = ck_ref[0, 0].reshape(w, past).T.astype(BF16)
        vt_ref[0, :, 0, :, 0:past] = cv_ref[0, 0].reshape(*slots, past).astype(BF16)
        v_sc[0:new, :] = vn_ref[0].astype(F32)
        v_sc[new:tp - past, :] = jnp.zeros((tp - past - new, w), F32)
        vt_ref[0, :, 0, :, past:tp] = v_sc[...].T.astype(BF16).reshape(*slots, tp - past)
    else:
        k_ref[0, 0:past, :] = ck_ref[0, 0].reshape(past, w).astype(BF16)
        v_sc[0:past, :] = cv_ref[0, 0].reshape(past, w)
        v_sc[past:past + new, :] = vn_ref[0].astype(F32)
        v_sc[past + new:tp, :] = jnp.zeros((tp - past - new, w), F32)
        vt_ref[0, :, 0] = v_sc[...].T.astype(BF16).reshape(*slots, tp)


def _history_kv(cache_k, cache_v, layer, k_new, v_new, tp):
    _, bsz, past, heads, dh = cache_k.shape
    T, w = k_new.shape[1:]
    time_minor = dh < LANES and past % LANES == 0 and (tp - past) % LANES == 0
    if time_minor:
        cache_k, cache_v = (jnp.transpose(c, (0, 1, 3, 4, 2)) for c in (cache_k, cache_v))
    cspec = pl.BlockSpec((1, 1) + cache_k.shape[2:], lambda b: (layer, b, 0, 0, 0))
    nspec = pl.BlockSpec((1, T, w), lambda b: (b, 0, 0))
    return pl.pallas_call(
        functools.partial(_history_kv_kernel, time_minor=time_minor),
        grid=(bsz,),
        in_specs=[cspec, cspec, nspec, nspec],
        out_specs=[pl.BlockSpec((1, tp, w), lambda b: (b, 0, 0)),
                   pl.BlockSpec((1, w // LANES, 1, LANES, tp), lambda b: (b, 0, 0, 0, 0))],
        out_shape=[jax.ShapeDtypeStruct((bsz, tp, w), BF16),
                   jax.ShapeDtypeStruct((bsz, w // LANES, 1, LANES, tp), BF16)],
        scratch_shapes=[pltpu.VMEM((tp - past if time_minor else tp, w), F32)],
        compiler_params=_params("parallel"),
        name="history_kv",
    )(cache_k, cache_v, k_new, v_new)


def _trunk(x, pos0, caches, wts, tm):
    bsz, T, d = x.shape
    n = bsz * T
    depth = wts["norm_g"].shape[0]
    w = d // 2
    x2 = x.reshape(n, d)
    tt = _pick_tile(T, 256)
    flat = lambda a: a.reshape(n, a.shape[-1])
    seq = lambda a: a.reshape(bsz, T, a.shape[-1])
    n_keys = T if caches is None else T + caches[0].shape[2]
    key_pad = -n_keys % LANES
    pad_keys = lambda a: jnp.pad(a, ((0, 0), (0, key_pad), (0, 0))) if key_pad else a
    rope = None
    n_even, n_odd = (depth + 1) // 2, depth // 2
    vt_tiles_per_seq = T // tm if caches is None and tm == ATTN_TK and T % tm == 0 else 0
    fox_kv = diff_kv = None
    even, odd = [], []
    for l in range(depth):
        g = wts["norm_g"][l].reshape(6, 1, d)
        x2 = _ffn(x2, g[0], g[1], wts["w_ffn_in"], wts["w_ffn_out"], (l, 0), tm)
        if l % 2 == 0:
            e = l // 2
            u, gate, q, logf, logf1, *fox_kv, kb, vx = _inproj_even(
                x2, g[2], wts["w_in_even"][e], wts["b_fox_f"][e], tm, e, fox_kv, n_even, vt_tiles_per_seq)
            if caches is None:
                h0 = jnp.zeros((bsz, 1, w), F32)
                buf0 = jnp.zeros((bsz, 8, w), F32)
                kb_all, logf_all = seq(kb), seq(logf)
                vb_all = vx if vt_tiles_per_seq else seq(vx)
            else:
                flogf, lru_h, lru_conv = (c[e] for c in caches[2:5])
                h0 = lru_h.reshape(bsz, 1, w)
                buf0 = jnp.pad(lru_conv, ((0, 0), (8 - (CONV_W - 1), 0), (0, 0)))
                kb_all, vb_all = _history_kv(caches[0], caches[1], e, seq(kb), seq(vx), n_keys + key_pad)
                logf_all = jnp.concatenate([jnp.repeat(flogf, HALF, axis=-1), seq(logf)], axis=1)
            ya, h_last, nbuf = _lru(seq(u), seq(gate), h0, buf0, wts["lru_conv_w"][e], wts["lru_conv_b"][e],
                                    wts["lru_wr"][e], wts["lru_wi"][e], wts["lru_br"][e], wts["lru_bi"][e],
                                    wts["lru_lambda"][e], tt)
            Tk = logf_all.shape[1]
            Tp = Tk + key_pad
            ck, cq = _cumsum(pad_keys(logf_all), Tp if Tp <= 2 * CUMSUM_TILE else _pick_tile(Tp, CUMSUM_TILE))
            yb = _attention(seq(q), kb_all, vb_all, (cq[..., Tk - T:Tk], ck), pos0=pos0, kv_len=Tk, fox=True)
            mix = (g[3], wts["w_out_even"][e], [flat(ya), flat(yb)])
            even.append((seq(logf1)[..., :FOX_HEADS], h_last.reshape(bsz, w), nbuf))
        else:
            o = l // 2
            lam_init = 0.8 - 0.6 * math.exp(-0.3 * l)
            if rope is None:
                tabs = _rope_tables(T, pos0)
                if T % tm:
                    tabs = [jnp.tile(t, (n // T, 1)) for t in tabs]
                rope = (tabs, tabs[0].shape[0] // tm)
            q, *diff_kv, kb, vx = _inproj_odd(x2, g[2], wts["w_in_odd"][o], rope[0], rope[1], tm, o, diff_kv,
                                              n_odd, vt_tiles_per_seq)
            if caches is None:
                kb_all = seq(kb)
                vb_all = vx if vt_tiles_per_seq else seq(vx)
            else:
                kb_all, vb_all = _history_kv(caches[5], caches[6], o, seq(kb), seq(vx), n_keys + key_pad)
            y = _attention(seq(q), kb_all, vb_all, (wts["diff_lambda"][o], wts["diff_subln_g"][o]),
                           pos0=pos0, kv_len=n_keys, fox=False, lam_init=lam_init)
            mix = (g[3], wts["w_out_odd"][o], [flat(y)])
        x2 = _ffn(x2, g[4], g[5], wts["w_ffn_in"], wts["w_ffn_out"], (l, 1), tm, mix)
    per_seq = lambda a: (jnp.transpose(a, (0, 1, 4, 2, 3)) if a.ndim == 5
                         else a.reshape(a.shape[0], bsz, T, *a.shape[2:]))
    even_out = [per_seq(a) for a in fox_kv] + [jnp.stack(z) for z in zip(*even)]
    return x2.reshape(bsz, T, d), even_out, [per_seq(a) for a in diff_kv]


def _prepare_weights(norm_g, w_ffn_in, w_ffn_out, w_in_even, b_fox_f, lru_conv_w, lru_conv_b, lru_w_gates,
                     lru_b_gates, lru_lambda, w_out_even, w_in_odd, diff_lambda, diff_subln_g, w_out_odd):
    n_even, d, _ = w_in_even.shape
    w = d // 2
    blk = w // LRU_BLOCKS
    wf = jnp.repeat(w_in_even[:, :, 5 * w:], HALF, axis=-1)
    pad_lanes = lambda a: jnp.pad(a, [(0, 0)] * (a.ndim - 1) + [(0, LANES - a.shape[-1])])
    eye = jnp.eye(LRU_BLOCKS, dtype=F32)
    dense = lambda g: jnp.einsum("encg,nm->encmg", g, eye).reshape(n_even, w, w)
    return dict(
        norm_g=norm_g, w_ffn_in=w_ffn_in.astype(BF16), w_ffn_out=w_ffn_out.astype(BF16),
        w_in_even=jnp.concatenate([w_in_even[:, :, :5 * w], wf, pad_lanes(w_in_even[:, :, 5 * w:])],
                                  axis=-1).astype(BF16),
        b_fox_f=jnp.concatenate([jnp.repeat(b_fox_f, HALF, axis=-1), pad_lanes(b_fox_f)],
                                axis=-1).reshape(n_even, 1, w + LANES),
        lru_conv_w=lru_conv_w, lru_conv_b=lru_conv_b.reshape(n_even, 1, w),
        lru_wr=dense(lru_w_gates[..., :blk]).astype(BF16), lru_wi=dense(lru_w_gates[..., blk:]).astype(BF16),
        lru_br=lru_b_gates[..., :blk].reshape(n_even, 1, w), lru_bi=lru_b_gates[..., blk:].reshape(n_even, 1, w),
        lru_lambda=lru_lambda.reshape(n_even, 1, w), w_out_even=w_out_even.astype(BF16),
        w_in_odd=w_in_odd.astype(BF16), diff_lambda=diff_lambda, diff_subln_g=diff_subln_g.reshape(-1, 1, LANES),
        w_out_odd=w_out_odd.astype(BF16))


def kernel(x_prompt, x_sample, cache_fox_k, cache_fox_v, cache_fox_logf, state_lru_h, state_lru_conv, cache_diff_k, cache_diff_v, norm_g, w_ffn_in, w_ffn_out, w_in_even, b_fox_f, lru_conv_w, lru_conv_b, lru_w_gates, lru_b_gates, lru_lambda, w_out_even, w_in_odd, diff_lambda, diff_subln_g, w_out_odd):
    wts = _prepare_weights(norm_g, w_ffn_in, w_ffn_out, w_in_even, b_fox_f, lru_conv_w, lru_conv_b, lru_w_gates,
                           lru_b_gates, lru_lambda, w_out_even, w_in_odd, diff_lambda, diff_subln_g, w_out_odd)
    y_p, pe, po = _trunk(x_prompt, 0, None, wts, _pick_tile(x_prompt.shape[0] * x_prompt.shape[1], 512))
    caches = (cache_fox_k, cache_fox_v, cache_fox_logf, state_lru_h, state_lru_conv, cache_diff_k, cache_diff_v)
    y_s, se, so = _trunk(x_sample, cache_fox_k.shape[2], caches, wts,
                         _pick_tile(x_sample.shape[0] * x_sample.shape[1], 256))
    return (y_p, y_s, *pe, *po, *se, *so)
```

```python
import functools
import math

import jax
import jax.numpy as jnp
from jax import lax
from jax.experimental import pallas as pl
from jax.experimental.pallas import tpu as pltpu

F32 = jnp.float32
BF16 = jnp.bfloat16

EPS = 1e-6
NEG_INF = -1e30
LRU_C = 8.0
LRU_BLOCKS = 8
CONV_W = 4
FOX_HEADS = 8
DIFF_HEADS = 8
CHUNK = 64
ROT_DIM = 16
ROPE_THETA = 500000.0

LANES = 128
SUBLANES = 8
HALF = LANES // 2
FF_CHUNK = 256
AHEAD = 2
ATTN_TQ, ATTN_TK = 512, 512
CUMSUM_TILE = 1024
LOG2E = 1.0 / math.log(2.0)
Q_SCALE = HALF ** -0.5 * LOG2E
VMEM_LIMIT = 56 * 1024 * 1024


def _params(*sem):
    return pltpu.CompilerParams(dimension_semantics=sem, vmem_limit_bytes=VMEM_LIMIT)


def _resident(shape, lead=()):
    nd = len(shape)
    return pl.BlockSpec((None,) * len(lead) + tuple(shape), lambda *_: tuple(lead) + (0,) * nd,
                        pipeline_mode=pl.Buffered(1))


def _rms(x, g):
    return x * lax.rsqrt(jnp.mean(x * x, axis=-1, keepdims=True) + EPS) * g


def _mm(a, b):
    return jnp.dot(a, b, preferred_element_type=F32)


def _mm_nt(a, b):
    return lax.dot_general(a, b, (((1,), (1,)), ((), ())), preferred_element_type=F32)


def _log_sigmoid(x):
    return jnp.minimum(x, 0.0) - jnp.log1p(jnp.exp(-jnp.abs(x)))


def _softplus(x):
    return jnp.maximum(x, 0.0) + jnp.log1p(jnp.exp(-jnp.abs(x)))


def _pick_tile(n, target):
    t = min(n, target)
    while n % t:
        t //= 2
    return t


def _ffn_kernel(x_ref, gpre_ref, gpost_ref, wi_ref, wo_ref, *rest, dff):
    *mix, o_ref, act_ref = rest
    x = x_ref[...]
    if mix:
        gmix_ref, wmix_ref, *y_refs = mix
        y = jnp.concatenate([r[...] for r in y_refs], axis=1)
        x = x + _rms(_mm(y, wmix_ref[...]), gmix_ref[...])
    xn = _rms(x, gpre_ref[...]).astype(BF16)
    for c0 in range(0, dff, FF_CHUNK):
        g = _mm(xn, wi_ref[:, c0:c0 + FF_CHUNK])
        u = _mm(xn, wi_ref[:, dff + c0:dff + c0 + FF_CHUNK])
        act_ref[:, c0:c0 + FF_CHUNK] = ((g * jax.nn.sigmoid(g)) * u).astype(BF16)
    y = _mm(act_ref[...], wo_ref[...])
    o_ref[...] = x + 0.5 * _rms(y, gpost_ref[...])


def _ffn(x2, g_pre, g_post, w_in, w_out, which, tm, mix=None):
    n, d = x2.shape
    dff = w_out.shape[-2]
    assert dff % FF_CHUNK == 0 and n % tm == 0
    row = lambda c: pl.BlockSpec((tm, c), lambda i: (i, 0))
    mix_in, mix_specs = [], []
    if mix is not None:
        g_mix, w_mix, ys = mix
        mix_in = [g_mix, w_mix, *ys]
        mix_specs = [_resident((1, d)), _resident(w_mix.shape)] + [row(y.shape[1]) for y in ys]
    return pl.pallas_call(
        functools.partial(_ffn_kernel, dff=dff),
        grid=(n // tm,),
        in_specs=[row(d), _resident((1, d)), _resident((1, d)), _resident((d, 2 * dff), which),
                  _resident((dff, d), which)] + mix_specs,
        out_specs=row(d),
        out_shape=jax.ShapeDtypeStruct((n, d), F32),
        scratch_shapes=[pltpu.VMEM((tm, dff), BF16)],
        compiler_params=_params("parallel"),
        name="ffn_mix" if mix is not None else "ffn",
    )(x2, g_pre, g_post, w_in, w_out, *mix_in)


def _emit_kv(k, v, k_ref, v_ref, kb_ref, vx_ref, heads, vt_tiles):
    tm, w = k.shape
    vt = v.T if vt_tiles else None
    if k_ref.ndim == 5:
        k_out, v_out = (t.reshape(1, heads, w // heads, tm) for t in (k.T, vt))
    else:
        k_out, v_out = (t.reshape(tm, heads, w // heads) for t in (k, v))
    for l in range(k_ref.shape[0]):
        k_ref[l] = k_out if l == 0 else jnp.zeros_like(k_out)
        v_ref[l] = v_out if l == 0 else jnp.zeros_like(v_out)
    kb_ref[...] = k.astype(BF16)
    if vt_tiles:
        vx_ref[0, :, 0] = vt.astype(BF16).reshape(w // LANES, LANES, tm)
    else:
        vx_ref[...] = v.astype(BF16)


def _kv_out(n, w, heads, tm, layer, stacks, n_layers, vt_tiles_per_seq):
    dh = w // heads
    assert (stacks is None) == (layer == 0)
    own_layers = n_layers if stacks is None else 1
    nkb = vt_tiles_per_seq
    if nkb and dh < LANES:
        stacked = pl.BlockSpec((own_layers, 1, heads, dh, tm), lambda i: (layer, i // nkb, 0, 0, i % nkb))
        stacked_shape = jax.ShapeDtypeStruct((n_layers, n // (nkb * tm), heads, dh, nkb * tm), F32)
    else:
        stacked = pl.BlockSpec((own_layers, tm, heads, dh), lambda i: (layer, i, 0, 0))
        stacked_shape = jax.ShapeDtypeStruct((n_layers, n, heads, dh), F32)
    row = pl.BlockSpec((tm, w), lambda i: (i, 0))
    if vt_tiles_per_seq:
        vx = pl.BlockSpec((1, w // LANES, 1, LANES, tm), lambda i: (i // nkb, 0, i % nkb, 0, 0))
        vx_shape = jax.ShapeDtypeStruct((n // (nkb * tm), w // LANES, nkb, LANES, tm), BF16)
    else:
        vx, vx_shape = row, jax.ShapeDtypeStruct((n, w), BF16)
    specs = [stacked, stacked, row, vx]
    shapes = [stacked_shape, stacked_shape, jax.ShapeDtypeStruct((n, w), BF16), vx_shape]
    extra_in = [] if stacks is None else list(stacks)
    return specs, shapes, extra_in


def _inproj_even_kernel(*refs, w, vt_tiles):
    x_ref, g_ref, w_ref, bf_ref = refs[:4]
    u_ref, gate_ref, q_ref, logf_ref, k_ref, v_ref, kb_ref, vx_ref = refs[-8:]
    hn = _rms(x_ref[...], g_ref[...]).astype(BF16)
    u_ref[...] = _mm(hn, w_ref[:, 0:w])
    gate_ref[...] = _mm(hn, w_ref[:, w:2 * w])
    q_ref[...] = (_mm(hn, w_ref[:, 2 * w:3 * w]) * Q_SCALE).astype(BF16)
    logf_ref[...] = _log_sigmoid(_mm(hn, w_ref[:, 5 * w:5 * w + LANES]) + bf_ref[...])
    _emit_kv(_mm(hn, w_ref[:, 3 * w:4 * w]), _mm(hn, w_ref[:, 4 * w:5 * w]), k_ref, v_ref, kb_ref, vx_ref,
             FOX_HEADS, vt_tiles)


def _inproj_even(x2, g, w_wide, bf_wide, tm, layer, stacks, n_layers, vt_tiles_per_seq):
    n, d = x2.shape
    w = (w_wide.shape[1] - LANES) // 5
    row = lambda c: pl.BlockSpec((tm, c), lambda i: (i, 0))
    f = jax.ShapeDtypeStruct((n, w), F32)
    b = jax.ShapeDtypeStruct((n, w), BF16)
    kv_specs, kv_shapes, extra_in = _kv_out(n, w, FOX_HEADS, tm, layer, stacks, n_layers, vt_tiles_per_seq)
    return pl.pallas_call(
        functools.partial(_inproj_even_kernel, w=w, vt_tiles=bool(vt_tiles_per_seq)),
        grid=(n // tm,),
        in_specs=[row(d), _resident((1, d)), _resident(w_wide.shape), _resident(bf_wide.shape)]
        + [pl.BlockSpec(memory_space=pl.ANY)] * len(extra_in),
        out_specs=[row(w)] * 3 + [row(LANES)] + kv_specs,
        out_shape=[f, f, b, jax.ShapeDtypeStruct((n, LANES), F32)] + kv_shapes,
        input_output_aliases={4 + a: 4 + a for a in range(len(extra_in))},
        compiler_params=_params("parallel"),
        name="inproj_even",
    )(x2, g, w_wide, bf_wide, *extra_in)


def _scan_rows(a, b, h):
    tt, c = b.shape
    assert tt % SUBLANES == 0
    groups = (tt // SUBLANES, SUBLANES, c)
    b = b.reshape(groups)
    a = None if a is None else a.reshape(groups)
    sub = lax.broadcasted_iota(jnp.int32, groups, 1)
    d = 1
    while d < SUBLANES:
        keep = sub >= d
        if a is None:
            b = b + jnp.where(keep, pltpu.roll(b, d, 1), 0.0)
        else:
            b = a * jnp.where(keep, pltpu.roll(b, d, 1), 0.0) + b
            a = a * jnp.where(keep, pltpu.roll(a, d, 1), 1.0)
        d *= 2
    out = []
    for g in range(groups[0]):
        blk = b[g] + (h if a is None else a[g] * h)
        out.append(blk)
        h = blk[SUBLANES - 1:SUBLANES]
    return jnp.concatenate(out, axis=0)


def _lru_kernel(u_ref, gate_ref, h0_ref, buf0_ref, cw_ref, cb_ref, wr_ref, wi_ref, br_ref, bi_ref, lam_ref,
                ya_ref, hlast_ref, nbuf_ref, ubuf, hcar, *, tt):
    t = pl.program_id(1)
    pad = ubuf.shape[0] - tt

    @pl.when(t == 0)
    def _():
        ubuf[0:pad, :] = buf0_ref[0]
        hcar[...] = h0_ref[0]

    u = u_ref[0]
    ubuf[pad:pad + tt, :] = u
    uc = cb_ref[...]
    for j in range(CONV_W):
        o = pad - (CONV_W - 1) + j
        uc = uc + ubuf[o:o + tt, :] * cw_ref[j:j + 1, :]
    ucb = uc.astype(BF16)
    r = jax.nn.sigmoid(_mm(ucb, wr_ref[...]) + br_ref[...])
    ig = jax.nn.sigmoid(_mm(ucb, wi_ref[...]) + bi_ref[...])
    log_a = (-LRU_C * r) * _softplus(-lam_ref[...])
    a = jnp.exp(log_a)
    bx = jnp.sqrt(-jnp.tanh(log_a) * (a * a + 1.0)) * (ig * uc)
    bx = _scan_rows(a, bx, hcar[...])
    hcar[...] = bx[tt - 1:tt, :]
    ya_ref[0] = (bx * jax.nn.gelu(gate_ref[0])).astype(BF16)
    ubuf[0:pad, :] = ubuf[tt:tt + pad, :]

    @pl.when(t == pl.num_programs(1) - 1)
    def _():
        hlast_ref[0] = bx[tt - 1:tt, :]
        nbuf_ref[0] = ubuf[pad - (CONV_W - 1):pad, :]


def _lru(u, gate, h0, buf0, cw, cb, wr, wi, br, bi, lam, tt):
    bsz, T, w = u.shape
    pad = 8
    seq = pl.BlockSpec((1, tt, w), lambda b, t: (b, t, 0))
    per_b = lambda r: pl.BlockSpec((1, r, w), lambda b, t: (b, 0, 0))
    return pl.pallas_call(
        functools.partial(_lru_kernel, tt=tt),
        grid=(bsz, T // tt),
        in_specs=[seq, seq, per_b(1), per_b(pad), _resident((CONV_W, w)), _resident((1, w)), _resident((w, w)),
                  _resident((w, w)), _resident((1, w)), _resident((1, w)), _resident((1, w))],
        out_specs=[seq, per_b(1), per_b(CONV_W - 1)],
        out_shape=[jax.ShapeDtypeStruct((bsz, T, w), BF16), jax.ShapeDtypeStruct((bsz, 1, w), F32),
                   jax.ShapeDtypeStruct((bsz, CONV_W - 1, w), F32)],
        scratch_shapes=[pltpu.VMEM((tt + pad, w), F32), pltpu.VMEM((1, w), F32)],
        compiler_params=_params("parallel", "arbitrary"),
        name="conv_rglru",
    )(u, gate, h0, buf0, cw, cb, wr, wi, br, bi, lam)


def _cumsum_kernel(x_ref, ck_ref, cq_ref, car, *, tt):
    @pl.when(pl.program_id(1) == 0)
    def _():
        car[...] = jnp.zeros_like(car)

    x = _scan_rows(None, x_ref[0], car[...])
    car[...] = x[tt - 1:tt, :]
    x = x * LOG2E
    tr = x.T
    for h in range(FOX_HEADS):
        ck_ref[0, :, h * LANES:(h + 1) * LANES] = jnp.broadcast_to(x[:, h:h + 1], (tt, LANES))
        cq_ref[0, h // 2, h % 2:h % 2 + 1, :] = tr[h:h + 1, :]


def _cumsum(x, tt):
    bsz, T, _ = x.shape
    ns = FOX_HEADS // 2
    return pl.pallas_call(
        functools.partial(_cumsum_kernel, tt=tt),
        grid=(bsz, T // tt),
        in_specs=[pl.BlockSpec((1, tt, LANES), lambda b, t: (b, t, 0))],
        out_specs=[pl.BlockSpec((1, tt, FOX_HEADS * LANES), lambda b, t: (b, t, 0)),
                   pl.BlockSpec((1, ns, 2, tt), lambda b, t: (b, 0, 0, t))],
        out_shape=[jax.ShapeDtypeStruct((bsz, T, FOX_HEADS * LANES), F32),
                   jax.ShapeDtypeStruct((bsz, ns, 2, T), F32)],
        scratch_shapes=[pltpu.VMEM((1, LANES), F32)],
        compiler_params=_params("parallel", "arbitrary"),
        name="logf_cumsum",
    )(x)


def _attn_kernel(*refs, tq, tk, nk_total, kv_len, pos0, fox, lam_init):
    if fox:
        q_ref, k_ref, vt_ref, cq_ref, ck_ref, o_ref, m_sc, l_sc, acc_sc, s_sc = refs
    else:
        q_ref, k_ref, vt_ref, lp_ref, sg_ref, o_ref, m_sc, l_sc, acc_sc, s_sc = refs
    i = pl.program_id(2)
    q0 = pos0 + i * tq
    shift = CHUNK.bit_length() - 1
    lo = lax.broadcasted_iota(jnp.int32, (1, LANES), 1) < HALF
    spb = q_ref.shape[2] // LANES
    qm = []
    for sl in range(spb):
        q = q_ref[0, :, sl * LANES:(sl + 1) * LANES]
        qm += [jnp.where(lo, q, jnp.zeros_like(q)), jnp.where(lo, jnp.zeros_like(q), q)]
    if fox:
        n_full = (q0 + 1) // tk
        n_need = (q0 + tq - 1) // tk + 1
    else:
        n_full = (((q0 >> shift) + 1) << shift) // tk
        n_need = (((((q0 + tq - 1) >> shift) + 1) << shift) + tk - 1) // tk
    n_need = jnp.minimum(n_need, nk_total)
    n_full = jnp.minimum(jnp.minimum(n_full, kv_len // tk), n_need)

    m_sc[...] = jnp.full_like(m_sc, NEG_INF)
    l_sc[...] = jnp.zeros_like(l_sc)
    acc_sc[...] = jnp.zeros_like(acc_sc)

    n_maps = 2 * spb

    def scores(g, j):
        k0 = j * tk if isinstance(j, int) else pl.multiple_of(j * tk, tk)
        lanes = slice((g // 2) * LANES, (g // 2 + 1) * LANES)
        hk = tk // 2

        def half(r):
            s = _mm_nt(k_ref[0, pl.ds(k0 + r, hk), lanes], qm[g])
            if fox:
                ck = ck_ref[0, pl.ds(k0 + r, hk), g * LANES:(g + 1) * LANES]
                s = s - jnp.concatenate([ck] * (tq // LANES), axis=1)
            return s

        return jnp.concatenate([half(0), half(hk)], axis=0)

    assert AHEAD < n_maps
    for a in range(AHEAD):
        s_sc[a] = scores(a, 0)

    hq, hk = tq // 2, tk // 2
    diag = (tq == tk and pos0 % tk == 0 and kv_len == nk_total * tk and hq % LANES == 0 and hk % CHUNK == 0)

    def right_half(full, extra, op):
        return jnp.concatenate([full[:, :hq], op(full[:, hq:], extra)], axis=1)

    def tile(j, masked):
        k0 = pl.multiple_of(j * tk, tk)
        split = masked and diag
        if split:
            kr = lax.broadcasted_iota(jnp.int32, (hk, hq), 0)
            qr = lax.broadcasted_iota(jnp.int32, (hk, hq), 1)
            tri = (qr >= kr) if fox else ((qr >> shift) >= (kr >> shift))
        elif masked:
            kpos = k0 + lax.broadcasted_iota(jnp.int32, (tk, tq), 0)
            qpos = q0 + lax.broadcasted_iota(jnp.int32, (tk, tq), 1)
            vis = (qpos >= kpos) if fox else ((qpos >> shift) >= (kpos >> shift))
            if kv_len < nk_total * tk:
                vis = vis & (kpos < kv_len)
        for g in range(n_maps):
            sl, c = divmod(g, 2)
            dj, g_ahead = divmod(g + AHEAD, n_maps)
            s_sc[g_ahead] = scores(g_ahead, jnp.minimum(j + dj, nk_total - 1) if dj else j)
            if split:
                s = s_sc[g, :hk, :]
                s = jnp.concatenate([jnp.where(tri, s[:, :hq], NEG_INF), s[:, hq:]], axis=1)
                s_late = jnp.where(tri, s_sc[g, hk:, hq:], NEG_INF)
            else:
                s = s_sc[g]
                if masked:
                    s = jnp.where(vis, s, NEG_INF)
            m_prev = m_sc[g]
            m_tile = jnp.max(s, axis=0, keepdims=True)
            if split:
                m_tile = right_half(m_tile, jnp.max(s_late, axis=0, keepdims=True), jnp.maximum)
            if fox:
                cq = cq_ref[0, sl, c:c + 1, :]
                m_new = jnp.maximum(m_prev, m_tile + cq)
                m_sub = m_new - cq
            else:
                m_new = m_sub = jnp.maximum(m_prev, m_tile)
            p = jnp.exp2(s - m_sub)
            p_sum = jnp.sum(p, axis=0, keepdims=True)
            alpha = jnp.exp2(m_prev - m_new)
            vt = vt_ref[0, sl, j]
            r = slice(c * HALF, (c + 1) * HALF) if fox else slice(0, LANES)
            if split:
                p_late = jnp.exp2(s_late - m_sub[:, hq:])
                p_sum = right_half(p_sum, jnp.sum(p_late, axis=0, keepdims=True), jnp.add)
                pv = right_half(_mm(vt[r, :hk], p.astype(BF16)), _mm(vt[r, hk:], p_late.astype(BF16)), jnp.add)
            else:
                pv = _mm(vt[r, :], p.astype(BF16))
            l_sc[g] = alpha * l_sc[g] + p_sum
            m_sc[g] = m_new
            if fox:
                acc_sc[sl, r, :] = alpha * acc_sc[sl, r, :] + pv
            else:
                acc_sc[g] = alpha * acc_sc[g] + pv

    def full_body(j, carry):
        tile(j, False)
        return carry

    def masked_body(j, carry):
        tile(j, True)
        return carry

    lax.fori_loop(0, n_full, full_body, 0)
    lax.fori_loop(n_full, n_need, masked_body, 0)

    for sl in range(spb):
        a, b = 2 * sl, 2 * sl + 1
        if fox:
            ot = jnp.concatenate([acc_sc[sl, 0:HALF, :] / l_sc[a], acc_sc[sl, HALF:LANES, :] / l_sc[b]], axis=0)
            o = ot.T
        else:
            lp = lp_ref[...]
            lam = (jnp.exp(jnp.sum(lp[0:1] * lp[1:2], axis=-1, keepdims=True))
                   - jnp.exp(jnp.sum(lp[2:3] * lp[3:4], axis=-1, keepdims=True)) + lam_init)
            ot = acc_sc[a] / l_sc[a] - lam * (acc_sc[b] / l_sc[b])
            ot = ot * lax.rsqrt(jnp.mean(ot * ot, axis=0, keepdims=True) + EPS)
            o = (ot.T * sg_ref[...]) * (1.0 - lam_init)
        o_ref[0, :, sl * LANES:(sl + 1) * LANES] = o.astype(BF16)


def _attention(q, k, v, extra, *, pos0, kv_len, fox, lam_init=0.0, tq=ATTN_TQ, tk=ATTN_TK, spb=4):
    bsz, Tq, w = q.shape
    Tk = k.shape[1]
    ns = w // LANES
    Tq_pad = -(-Tq // LANES) * LANES
    tq = _pick_tile(Tq_pad, tq)
    tk = tk if Tk % tk == 0 else Tk
    nk = Tk // tk
    q = jnp.pad(q, ((0, 0), (0, Tq_pad - Tq), (0, 0)))
    vt = v if v.ndim == 5 else v.reshape(bsz, nk, tk, ns, LANES).transpose(0, 3, 1, 4, 2)
    assert vt.shape == (bsz, ns, nk, LANES, tk)
    assert ns % spb == 0
    sw = spb * LANES
    qspec = pl.BlockSpec((1, tq, sw), lambda b, s, i: (b, i, s))
    kspec = pl.BlockSpec((1, Tk, sw), lambda b, s, i: (b, 0, s))
    vspec = pl.BlockSpec((1, spb, nk, LANES, tk), lambda b, s, i: (b, s, 0, 0, 0))
    if fox:
        cq, ck = extra
        extra = (jnp.pad(cq, ((0, 0), (0, 0), (0, 0), (0, Tq_pad - Tq))), ck)
        especs = [pl.BlockSpec((1, spb, 2, tq), lambda b, s, i: (b, s, 0, i)),
                  pl.BlockSpec((1, Tk, 2 * sw), lambda b, s, i: (b, 0, s), pipeline_mode=pl.Buffered(1))]
    else:
        lp, sg = extra
        especs = [pl.BlockSpec(lp.shape, lambda b, s, i: (0, 0)), pl.BlockSpec(sg.shape, lambda b, s, i: (0, 0))]
    out = pl.pallas_call(
        functools.partial(_attn_kernel, tq=tq, tk=tk, nk_total=nk, kv_len=kv_len, pos0=pos0, fox=fox,
                          lam_init=lam_init),
        grid=(bsz, ns // spb, Tq_pad // tq),
        in_specs=[qspec, kspec, vspec] + especs,
        out_specs=qspec,
        out_shape=jax.ShapeDtypeStruct((bsz, Tq_pad, w), BF16),
        scratch_shapes=[pltpu.VMEM((2 * spb, 1, tq), F32), pltpu.VMEM((2 * spb, 1, tq), F32),
                        pltpu.VMEM((spb if fox else 2 * spb, LANES, tq), F32), pltpu.VMEM((2 * spb, tk, tq), F32)],
        compiler_params=_params("parallel", "parallel", "arbitrary"),
        name="fox_attention" if fox else "diff_attention",
    )(q, k, vt, *extra)
    return out[:, :Tq]


def _rope_table_kernel(inv_ref, c_ref, s1_ref, s2_ref, *, pos0):
    T = c_ref.shape[0]
    pos = (pos0 + lax.broadcasted_iota(jnp.int32, (T, LANES), 0)).astype(F32)
    ang = pos * inv_ref[...]
    lane = lax.broadcasted_iota(jnp.int32, (T, LANES), 1) % HALF
    sn = jnp.sin(ang)
    c_ref[...] = jnp.cos(ang)
    s1_ref[...] = jnp.where(lane < ROT_DIM // 2, -sn, 0.0)
    s2_ref[...] = jnp.where(lane >= ROT_DIM // 2, sn, 0.0)


def _rope_tables(T, pos0):
    half = ROT_DIM // 2
    inv = ROPE_THETA ** (-jnp.arange(half, dtype=F32) * (2.0 / ROT_DIM))
    lane = jnp.arange(LANES) % HALF
    inv_lane = jnp.where(lane < ROT_DIM, inv[lane % half], 0.0).reshape(1, LANES).astype(F32)
    tab = jax.ShapeDtypeStruct((T, LANES), F32)
    return pl.pallas_call(
        functools.partial(_rope_table_kernel, pos0=pos0),
        out_shape=[tab, tab, tab],
        name="rope_tables",
    )(inv_lane)


def _inproj_odd_kernel(*refs, w, vt_tiles):
    x_ref, g_ref, w_ref, c_ref, s1_ref, s2_ref = refs[:6]
    q_ref, k_ref, v_ref, kb_ref, vx_ref = refs[-5:]
    hn = _rms(x_ref[...], g_ref[...]).astype(BF16)
    cs, s1, s2 = c_ref[...], s1_ref[...], s2_ref[...]
    half = ROT_DIM // 2

    def rope(t):
        out = []
        for h in range(w // LANES):
            s = t[:, h * LANES:(h + 1) * LANES]
            out.append(s * cs + pltpu.roll(s, LANES - half, 1) * s1 + pltpu.roll(s, half, 1) * s2)
        return jnp.concatenate(out, axis=1)

    q_ref[...] = (rope(_mm(hn, w_ref[:, 0:w])) * Q_SCALE).astype(BF16)
    _emit_kv(rope(_mm(hn, w_ref[:, w:2 * w])), _mm(hn, w_ref[:, 2 * w:3 * w]), k_ref, v_ref, kb_ref, vx_ref,
             DIFF_HEADS, vt_tiles)


def _inproj_odd(x2, g, w_in, tabs, tab_blocks, tm, layer, stacks, n_layers, vt_tiles_per_seq):
    n, d = x2.shape
    w = w_in.shape[1] // 3
    row = lambda c: pl.BlockSpec((tm, c), lambda i: (i, 0))
    tab = pl.BlockSpec((tm, LANES), lambda i: (i % tab_blocks, 0))
    kv_specs, kv_shapes, extra_in = _kv_out(n, w, DIFF_HEADS, tm, layer, stacks, n_layers, vt_tiles_per_seq)
    return pl.pallas_call(
        functools.partial(_inproj_odd_kernel, w=w, vt_tiles=bool(vt_tiles_per_seq)),
        grid=(n // tm,),
        in_specs=[row(d), _resident((1, d)), _resident((d, 3 * w)), tab, tab, tab]
        + [pl.BlockSpec(memory_space=pl.ANY)] * len(extra_in),
        out_specs=[row(w)] + kv_specs,
        out_shape=[jax.ShapeDtypeStruct((n, w), BF16)] + kv_shapes,
        input_output_aliases={6 + a: 1 + a for a in range(len(extra_in))},
        compiler_params=_params("parallel"),
        name="inproj_odd",
    )(x2, g, w_in, *tabs, *extra_in)


def _history_kv_kernel(ck_ref, cv_ref, kn_ref, vn_ref, k_ref, vt_ref, v_sc, *, time_minor):
    w, tp = k_ref.shape[2], k_ref.shape[1]
    new = kn_ref.shape[1]
    past = ck_ref.shape[4] if time_minor else ck_ref.shape[2]
    slots = (w // LANES, LANES)
    k_ref[0, past:past + new, :] = kn_ref[0]
    k_ref[0, past + new:tp, :] = jnp.zeros((tp - past - new, w), BF16)
    if time_minor:
        k_ref[0, 0:past, :] = ck_ref[0, 0].reshape(w, past).T.astype(BF16)
        vt_ref[0, :, 0, :, 0:past] = cv_ref[0, 0].reshape(*slots, past).astype(BF16)
        v_sc[0:new, :] = vn_ref[0].astype(F32)
        v_sc[new:tp - past, :] = jnp.zeros((tp - past - new, w), F32)
        vt_ref[0, :, 0, :, past:tp] = v_sc[...].T.astype(BF16).reshape(*slots, tp - past)
    else:
        k_ref[0, 0:past, :] = ck_ref[0, 0].reshape(past, w).astype(BF16)
        v_sc[0:past, :] = cv_ref[0, 0].reshape(past, w)
        v_sc[past:past + new, :] = vn_ref[0].astype(F32)
        v_sc[past + new:tp, :] = jnp.zeros((tp - past - new, w), F32)
        vt_ref[0, :, 0] = v_sc[...].T.astype(BF16).reshape(*slots, tp)


def _history_kv(cache_k, cache_v, layer, k_new, v_new, tp):
    _, bsz, past, heads, dh = cache_k.shape
    T, w = k_new.shape[1:]
    time_minor = dh < LANES and past % LANES == 0 and (tp - past) % LANES == 0
    if time_minor:
        cache_k, cache_v = (jnp.transpose(c, (0, 1, 3, 4, 2)) for c in (cache_k, cache_v))
    cspec = pl.BlockSpec((1, 1) + cache_k.shape[2:], lambda b: (layer, b, 0, 0, 0))
    nspec = pl.BlockSpec((1, T, w), lambda b: (b, 0, 0))
    return pl.pallas_call(
        functools.partial(_history_kv_kernel, time_minor=time_minor),
        grid=(bsz,),
        in_specs=[cspec, cspec, nspec, nspec],
        out_specs=[pl.BlockSpec((1, tp, w), lambda b: (b, 0, 0)),
                   pl.BlockSpec((1, w // LANES, 1, LANES, tp), lambda b: (b, 0, 0, 0, 0))],
        out_shape=[jax.ShapeDtypeStruct((bsz, tp, w), BF16),
                   jax.ShapeDtypeStruct((bsz, w // LANES, 1, LANES, tp), BF16)],
        scratch_shapes=[pltpu.VMEM((tp - past if time_minor else tp, w), F32)],
        compiler_params=_params("parallel"),
        name="history_kv",
    )(cache_k, cache_v, k_new, v_new)


def _trunk(x, pos0, caches, wts, tm):
    bsz, T, d = x.shape
    n = bsz * T
    depth = wts["norm_g"].shape[0]
    w = d // 2
    x2 = x.reshape(n, d)
    tt = _pick_tile(T, 256)
    flat = lambda a: a.reshape(n, a.shape[-1])
    seq = lambda a: a.reshape(bsz, T, a.shape[-1])
    n_keys = T if caches is None else T + caches[0].shape[2]
    key_pad = -n_keys % LANES
    pad_keys = lambda a: jnp.pad(a, ((0, 0), (0, key_pad), (0, 0))) if key_pad else a
    rope = None
    n_even, n_odd = (depth + 1) // 2, depth // 2
    vt_tiles_per_seq = T // tm if caches is None and tm == ATTN_TK and T % tm == 0 else 0
    fox_kv = diff_kv = None
    even, odd = [], []
    for l in range(depth):
        g = wts["norm_g"][l].reshape(6, 1, d)
        x2 = _ffn(x2, g[0], g[1], wts["w_ffn_in"], wts["w_ffn_out"], (l, 0), tm)
        if l % 2 == 0:
            e = l // 2
            u, gate, q, logf, *fox_kv, kb, vx = _inproj_even(
                x2, g[2], wts["w_in_even"][e], wts["b_fox_f"][e], tm, e, fox_kv, n_even, vt_tiles_per_seq)
            if caches is None:
                h0 = jnp.zeros((bsz, 1, w), F32)
                buf0 = jnp.zeros((bsz, 8, w), F32)
                kb_all, logf_all = seq(kb), seq(logf)
                vb_all = vx if vt_tiles_per_seq else seq(vx)
            else:
                flogf, lru_h, lru_conv = (c[e] for c in caches[2:5])
                h0 = lru_h.reshape(bsz, 1, w)
                buf0 = jnp.pad(lru_conv, ((0, 0), (8 - (CONV_W - 1), 0), (0, 0)))
                kb_all, vb_all = _history_kv(caches[0], caches[1], e, seq(kb), seq(vx), n_keys + key_pad)
                past_logf = jnp.pad(flogf, ((0, 0), (0, 0), (0, LANES - flogf.shape[-1])))
                logf_all = jnp.concatenate([past_logf, seq(logf)], axis=1)
            ya, h_last, nbuf = _lru(seq(u), seq(gate), h0, buf0, wts["lru_conv_w"][e], wts["lru_conv_b"][e],
                                    wts["lru_wr"][e], wts["lru_wi"][e], wts["lru_br"][e], wts["lru_bi"][e],
                                    wts["lru_lambda"][e], tt)
            Tk = logf_all.shape[1]
            Tp = Tk + key_pad
            ck, cq = _cumsum(pad_keys(logf_all), Tp if Tp <= 2 * CUMSUM_TILE else _pick_tile(Tp, CUMSUM_TILE))
            yb = _attention(seq(q), kb_all, vb_all, (cq[..., Tk - T:Tk], ck), pos0=pos0, kv_len=Tk, fox=True)
            mix = (g[3], wts["w_out_even"][e], [flat(ya), flat(yb)])
            even.append((seq(logf)[..., :FOX_HEADS], h_last.reshape(bsz, w), nbuf))
        else:
            o = l // 2
            lam_init = 0.8 - 0.6 * math.exp(-0.3 * l)
            if rope is None:
                tabs = _rope_tables(T, pos0)
                if T % tm:
                    tabs = [jnp.tile(t, (n // T, 1)) for t in tabs]
                rope = (tabs, tabs[0].shape[0] // tm)
            q, *diff_kv, kb, vx = _inproj_odd(x2, g[2], wts["w_in_odd"][o], rope[0], rope[1], tm, o, diff_kv,
                                              n_odd, vt_tiles_per_seq)
            if caches is None:
                kb_all = seq(kb)
                vb_all = vx if vt_tiles_per_seq else seq(vx)
            else:
                kb_all, vb_all = _history_kv(caches[5], caches[6], o, seq(kb), seq(vx), n_keys + key_pad)
            y = _attention(seq(q), kb_all, vb_all, (wts["diff_lambda"][o], wts["diff_subln_g"][o]),
                           pos0=pos0, kv_len=n_keys, fox=False, lam_init=lam_init)
            mix = (g[3], wts["w_out_odd"][o], [flat(y)])
        x2 = _ffn(x2, g[4], g[5], wts["w_ffn_in"], wts["w_ffn_out"], (l, 1), tm, mix)
    per_seq = lambda a: (jnp.transpose(a, (0, 1, 4, 2, 3)) if a.ndim == 5
                         else a.reshape(a.shape[0], bsz, T, *a.shape[2:]))
    even_out = [per_seq(a) for a in fox_kv] + [jnp.stack(z) for z in zip(*even)]
    return x2.reshape(bsz, T, d), even_out, [per_seq(a) for a in diff_kv]


def _prepare_weights(norm_g, w_ffn_in, w_ffn_out, w_in_even, b_fox_f, lru_conv_w, lru_conv_b, lru_w_gates,
                     lru_b_gates, lru_lambda, w_out_even, w_in_odd, diff_lambda, diff_subln_g, w_out_odd):
    n_even, d, _ = w_in_even.shape
    w = d // 2
    blk = w // LRU_BLOCKS
    pad_lanes = lambda a: jnp.pad(a, [(0, 0)] * (a.ndim - 1) + [(0, LANES - a.shape[-1])])
    eye = jnp.eye(LRU_BLOCKS, dtype=F32)
    dense = lambda g: jnp.einsum("encg,nm->encmg", g, eye).reshape(n_even, w, w)
    return dict(
        norm_g=norm_g, w_ffn_in=w_ffn_in.astype(BF16), w_ffn_out=w_ffn_out.astype(BF16),
        w_in_even=jnp.concatenate([w_in_even[:, :, :5 * w], pad_lanes(w_in_even[:, :, 5 * w:])],
                                  axis=-1).astype(BF16),
        b_fox_f=pad_lanes(b_fox_f).reshape(n_even, 1, LANES),
        lru_conv_w=lru_conv_w, lru_conv_b=lru_conv_b.reshape(n_even, 1, w),
        lru_wr=dense(lru_w_gates[..., :blk]).astype(BF16), lru_wi=dense(lru_w_gates[..., blk:]).astype(BF16),
        lru_br=lru_b_gates[..., :blk].reshape(n_even, 1, w), lru_bi=lru_b_gates[..., blk:].reshape(n_even, 1, w),
        lru_lambda=lru_lambda.reshape(n_even, 1, w), w_out_even=w_out_even.astype(BF16),
        w_in_odd=w_in_odd.astype(BF16), diff_lambda=diff_lambda, diff_subln_g=diff_subln_g.reshape(-1, 1, LANES),
        w_out_odd=w_out_odd.astype(BF16))


def kernel(x_prompt, x_sample, cache_fox_k, cache_fox_v, cache_fox_logf, state_lru_h, state_lru_conv, cache_diff_k, cache_diff_v, norm_g, w_ffn_in, w_ffn_out, w_in_even, b_fox_f, lru_conv_w, lru_conv_b, lru_w_gates, lru_b_gates, lru_lambda, w_out_even, w_in_odd, diff_lambda, diff_subln_g, w_out_odd):
    wts = _prepare_weights(norm_g, w_ffn_in, w_ffn_out, w_in_even, b_fox_f, lru_conv_w, lru_conv_b, lru_w_gates,
                           lru_b_gates, lru_lambda, w_out_even, w_in_odd, diff_lambda, diff_subln_g, w_out_odd)
    y_p, pe, po = _trunk(x_prompt, 0, None, wts, _pick_tile(x_prompt.shape[0] * x_prompt.shape[1], 512))
    caches = (cache_fox_k, cache_fox_v, cache_fox_logf, state_lru_h, state_lru_conv, cache_diff_k, cache_diff_v)
    y_s, se, so = _trunk(x_sample, cache_fox_k.shape[2], caches, wts,
                         _pick_tile(x_sample.shape[0] * x_sample.shape[1], 256))
    return (y_p, y_s, *pe, *po, *se, *so)
```

```python
import functools
import math

import jax
import jax.numpy as jnp
from jax import lax
from jax.experimental import pallas as pl
from jax.experimental.pallas import tpu as pltpu

F32 = jnp.float32
BF16 = jnp.bfloat16

EPS = 1e-6
NEG_INF = -1e30
LRU_C = 8.0
LRU_BLOCKS = 8
CONV_W = 4
FOX_HEADS = 8
DIFF_HEADS = 8
CHUNK = 64
ROT_DIM = 16
ROPE_THETA = 500000.0

LANES = 128
SUBLANES = 8
HALF = LANES // 2
FF_CHUNK = 256
AHEAD = 2
ATTN_TQ, ATTN_TK = 512, 512
CUMSUM_TILE = 1024
LOG2E = 1.0 / math.log(2.0)
Q_SCALE = HALF ** -0.5 * LOG2E
VMEM_LIMIT = 56 * 1024 * 1024


def _params(*sem):
    return pltpu.CompilerParams(dimension_semantics=sem, vmem_limit_bytes=VMEM_LIMIT)


def _resident(shape, lead=()):
    nd = len(shape)
    return pl.BlockSpec((None,) * len(lead) + tuple(shape), lambda *_: tuple(lead) + (0,) * nd,
                        pipeline_mode=pl.Buffered(1))


def _rms(x, g):
    return x * lax.rsqrt(jnp.mean(x * x, axis=-1, keepdims=True) + EPS) * g


def _mm(a, b):
    return jnp.dot(a, b, preferred_element_type=F32)


def _mm_nt(a, b):
    return lax.dot_general(a, b, (((1,), (1,)), ((), ())), preferred_element_type=F32)


def _log_sigmoid(x):
    return jnp.minimum(x, 0.0) - jnp.log1p(jnp.exp(-jnp.abs(x)))


def _softplus(x):
    return jnp.maximum(x, 0.0) + jnp.log1p(jnp.exp(-jnp.abs(x)))


def _pick_tile(n, target):
    t = min(n, target)
    while n % t:
        t //= 2
    return t


def _ffn_kernel(x_ref, gpre_ref, gpost_ref, wi_ref, wo_ref, *rest, dff):
    *mix, o_ref, act_ref = rest
    x = x_ref[...]
    if mix:
        gmix_ref, wmix_ref, *y_refs = mix
        y = jnp.concatenate([r[...] for r in y_refs], axis=1)
        x = x + _rms(_mm(y, wmix_ref[...]), gmix_ref[...])
    xn = _rms(x, gpre_ref[...]).astype(BF16)
    for c0 in range(0, dff, FF_CHUNK):
        g = _mm(xn, wi_ref[:, c0:c0 + FF_CHUNK])
        u = _mm(xn, wi_ref[:, dff + c0:dff + c0 + FF_CHUNK])
        act_ref[:, c0:c0 + FF_CHUNK] = ((g * jax.nn.sigmoid(g)) * u).astype(BF16)
    y = _mm(act_ref[...], wo_ref[...])
    o_ref[...] = x + 0.5 * _rms(y, gpost_ref[...])


def _ffn(x2, g_pre, g_post, w_in, w_out, which, tm, mix=None):
    n, d = x2.shape
    dff = w_out.shape[-2]
    assert dff % FF_CHUNK == 0 and n % tm == 0
    row = lambda c: pl.BlockSpec((tm, c), lambda i: (i, 0))
    mix_in, mix_specs = [], []
    if mix is not None:
        g_mix, w_mix, ys = mix
        mix_in = [g_mix, w_mix, *ys]
        mix_specs = [_resident((1, d)), _resident(w_mix.shape)] + [row(y.shape[1]) for y in ys]
    return pl.pallas_call(
        functools.partial(_ffn_kernel, dff=dff),
        grid=(n // tm,),
        in_specs=[row(d), _resident((1, d)), _resident((1, d)), _resident((d, 2 * dff), which),
                  _resident((dff, d), which)] + mix_specs,
        out_specs=row(d),
        out_shape=jax.ShapeDtypeStruct((n, d), F32),
        scratch_shapes=[pltpu.VMEM((tm, dff), BF16)],
        compiler_params=_params("parallel"),
        name="ffn_mix" if mix is not None else "ffn",
    )(x2, g_pre, g_post, w_in, w_out, *mix_in)


def _emit_kv(k, v, k_ref, v_ref, kb_ref, vx_ref, heads, vt_tiles):
    tm, w = k.shape
    vt = v.T if vt_tiles else None
    if k_ref.ndim == 5:
        k_out, v_out = (t.reshape(1, heads, w // heads, tm) for t in (k.T, vt))
    else:
        k_out, v_out = (t.reshape(tm, heads, w // heads) for t in (k, v))
    for l in range(k_ref.shape[0]):
        k_ref[l] = k_out if l == 0 else jnp.zeros_like(k_out)
        v_ref[l] = v_out if l == 0 else jnp.zeros_like(v_out)
    kb_ref[...] = k.astype(BF16)
    if vt_tiles:
        vx_ref[0, :, 0] = vt.astype(BF16).reshape(w // LANES, LANES, tm)
    else:
        vx_ref[...] = v.astype(BF16)


def _kv_out(n, w, heads, tm, layer, stacks, n_layers, vt_tiles_per_seq):
    dh = w // heads
    assert (stacks is None) == (layer == 0)
    own_layers = n_layers if stacks is None else 1
    nkb = vt_tiles_per_seq
    if nkb and dh < LANES:
        stacked = pl.BlockSpec((own_layers, 1, heads, dh, tm), lambda i: (layer, i // nkb, 0, 0, i % nkb))
        stacked_shape = jax.ShapeDtypeStruct((n_layers, n // (nkb * tm), heads, dh, nkb * tm), F32)
    else:
        stacked = pl.BlockSpec((own_layers, tm, heads, dh), lambda i: (layer, i, 0, 0))
        stacked_shape = jax.ShapeDtypeStruct((n_layers, n, heads, dh), F32)
    row = pl.BlockSpec((tm, w), lambda i: (i, 0))
    if vt_tiles_per_seq:
        vx = pl.BlockSpec((1, w // LANES, 1, LANES, tm), lambda i: (i // nkb, 0, i % nkb, 0, 0))
        vx_shape = jax.ShapeDtypeStruct((n // (nkb * tm), w // LANES, nkb, LANES, tm), BF16)
    else:
        vx, vx_shape = row, jax.ShapeDtypeStruct((n, w), BF16)
    specs = [stacked, stacked, row, vx]
    shapes = [stacked_shape, stacked_shape, jax.ShapeDtypeStruct((n, w), BF16), vx_shape]
    extra_in = [] if stacks is None else list(stacks)
    return specs, shapes, extra_in


def _inproj_even_kernel(*refs, w, vt_tiles):
    x_ref, g_ref, w_ref, bf_ref = refs[:4]
    u_ref, gate_ref, q_ref, logf_ref, k_ref, v_ref, kb_ref, vx_ref = refs[-8:]
    hn = _rms(x_ref[...], g_ref[...]).astype(BF16)
    u_ref[...] = _mm(hn, w_ref[:, 0:w])
    gate_ref[...] = _mm(hn, w_ref[:, w:2 * w])
    q_ref[...] = (_mm(hn, w_ref[:, 2 * w:3 * w]) * Q_SCALE).astype(BF16)
    logf_ref[...] = _log_sigmoid(_mm(hn, w_ref[:, 5 * w:5 * w + LANES]) + bf_ref[...])
    _emit_kv(_mm(hn, w_ref[:, 3 * w:4 * w]), _mm(hn, w_ref[:, 4 * w:5 * w]), k_ref, v_ref, kb_ref, vx_ref,
             FOX_HEADS, vt_tiles)


def _inproj_even(x2, g, w_wide, bf_wide, tm, layer, stacks, n_layers, vt_tiles_per_seq):
    n, d = x2.shape
    w = (w_wide.shape[1] - LANES) // 5
    row = lambda c: pl.BlockSpec((tm, c), lambda i: (i, 0))
    f = jax.ShapeDtypeStruct((n, w), F32)
    b = jax.ShapeDtypeStruct((n, w), BF16)
    kv_specs, kv_shapes, extra_in = _kv_out(n, w, FOX_HEADS, tm, layer, stacks, n_layers, vt_tiles_per_seq)
    return pl.pallas_call(
        functools.partial(_inproj_even_kernel, w=w, vt_tiles=bool(vt_tiles_per_seq)),
        grid=(n // tm,),
        in_specs=[row(d), _resident((1, d)), _resident(w_wide.shape), _resident(bf_wide.shape)]
        + [pl.BlockSpec(memory_space=pl.ANY)] * len(extra_in),
        out_specs=[row(w)] * 3 + [row(LANES)] + kv_specs,
        out_shape=[f, f, b, jax.ShapeDtypeStruct((n, LANES), F32)] + kv_shapes,
        input_output_aliases={4 + a: 4 + a for a in range(len(extra_in))},
        compiler_params=_params("parallel"),
        name="inproj_even",
    )(x2, g, w_wide, bf_wide, *extra_in)


def _scan_rows(a, b, h):
    tt, c = b.shape
    assert tt % SUBLANES == 0
    groups = (tt // SUBLANES, SUBLANES, c)
    b = b.reshape(groups)
    a = None if a is None else a.reshape(groups)
    sub = lax.broadcasted_iota(jnp.int32, groups, 1)
    d = 1
    while d < SUBLANES:
        keep = sub >= d
        if a is None:
            b = b + jnp.where(keep, pltpu.roll(b, d, 1), 0.0)
        else:
            b = a * jnp.where(keep, pltpu.roll(b, d, 1), 0.0) + b
            a = a * jnp.where(keep, pltpu.roll(a, d, 1), 1.0)
        d *= 2
    out = []
    for g in range(groups[0]):
        blk = b[g] + (h if a is None else a[g] * h)
        out.append(blk)
        h = blk[SUBLANES - 1:SUBLANES]
    return jnp.concatenate(out, axis=0)


def _lru_kernel(u_ref, gate_ref, h0_ref, buf0_ref, cw_ref, cb_ref, wr_ref, wi_ref, br_ref, bi_ref, lam_ref,
                ya_ref, hlast_ref, nbuf_ref, ubuf, hcar, *, tt):
    t = pl.program_id(1)
    pad = ubuf.shape[0] - tt

    @pl.when(t == 0)
    def _():
        ubuf[0:pad, :] = buf0_ref[0]
        hcar[...] = h0_ref[0]

    u = u_ref[0]
    ubuf[pad:pad + tt, :] = u
    uc = cb_ref[...]
    for j in range(CONV_W):
        o = pad - (CONV_W - 1) + j
        uc = uc + ubuf[o:o + tt, :] * cw_ref[j:j + 1, :]
    ucb = uc.astype(BF16)
    r = jax.nn.sigmoid(_mm(ucb, wr_ref[...]) + br_ref[...])
    ig = jax.nn.sigmoid(_mm(ucb, wi_ref[...]) + bi_ref[...])
    log_a = (-LRU_C * r) * _softplus(-lam_ref[...])
    a = jnp.exp(log_a)
    bx = jnp.sqrt(-jnp.tanh(log_a) * (a * a + 1.0)) * (ig * uc)
    bx = _scan_rows(a, bx, hcar[...])
    hcar[...] = bx[tt - 1:tt, :]
    ya_ref[0] = (bx * jax.nn.gelu(gate_ref[0])).astype(BF16)
    ubuf[0:pad, :] = ubuf[tt:tt + pad, :]

    @pl.when(t == pl.num_programs(1) - 1)
    def _():
        hlast_ref[0] = bx[tt - 1:tt, :]
        nbuf_ref[0] = ubuf[pad - (CONV_W - 1):pad, :]


def _lru(u, gate, h0, buf0, cw, cb, wr, wi, br, bi, lam, tt):
    bsz, T, w = u.shape
    pad = 8
    seq = pl.BlockSpec((1, tt, w), lambda b, t: (b, t, 0))
    per_b = lambda r: pl.BlockSpec((1, r, w), lambda b, t: (b, 0, 0))
    return pl.pallas_call(
        functools.partial(_lru_kernel, tt=tt),
        grid=(bsz, T // tt),
        in_specs=[seq, seq, per_b(1), per_b(pad), _resident((CONV_W, w)), _resident((1, w)), _resident((w, w)),
                  _resident((w, w)), _resident((1, w)), _resident((1, w)), _resident((1, w))],
        out_specs=[seq, per_b(1), per_b(CONV_W - 1)],
        out_shape=[jax.ShapeDtypeStruct((bsz, T, w), BF16), jax.ShapeDtypeStruct((bsz, 1, w), F32),
                   jax.ShapeDtypeStruct((bsz, CONV_W - 1, w), F32)],
        scratch_shapes=[pltpu.VMEM((tt + pad, w), F32), pltpu.VMEM((1, w), F32)],
        compiler_params=_params("parallel", "arbitrary"),
        name="conv_rglru",
    )(u, gate, h0, buf0, cw, cb, wr, wi, br, bi, lam)


def _cumsum_kernel(x_ref, ck_ref, cq_ref, car, *, tt):
    @pl.when(pl.program_id(1) == 0)
    def _():
        car[...] = jnp.zeros_like(car)

    x = _scan_rows(None, x_ref[0], car[...])
    car[...] = x[tt - 1:tt, :]
    x = x * LOG2E
    tr = x.T
    for h in range(FOX_HEADS):
        ck_ref[0, :, h * LANES:(h + 1) * LANES] = jnp.broadcast_to(x[:, h:h + 1], (tt, LANES))
        cq_ref[0, h // 2, h % 2:h % 2 + 1, :] = tr[h:h + 1, :]


def _cumsum(x, tt):
    bsz, T, _ = x.shape
    ns = FOX_HEADS // 2
    return pl.pallas_call(
        functools.partial(_cumsum_kernel, tt=tt),
        grid=(bsz, T // tt),
        in_specs=[pl.BlockSpec((1, tt, LANES), lambda b, t: (b, t, 0))],
        out_specs=[pl.BlockSpec((1, tt, FOX_HEADS * LANES), lambda b, t: (b, t, 0)),
                   pl.BlockSpec((1, ns, 2, tt), lambda b, t: (b, 0, 0, t))],
        out_shape=[jax.ShapeDtypeStruct((bsz, T, FOX_HEADS * LANES), F32),
                   jax.ShapeDtypeStruct((bsz, ns, 2, T), F32)],
        scratch_shapes=[pltpu.VMEM((1, LANES), F32)],
        compiler_params=_params("parallel", "arbitrary"),
        name="logf_cumsum",
    )(x)


def _attn_kernel(*refs, tq, tk, nk_total, kv_len, pos0, fox, lam_init):
    if fox:
        q_ref, k_ref, vt_ref, cq_ref, ck_ref, o_ref, m_sc, l_sc, acc_sc, s_sc = refs
    else:
        q_ref, k_ref, vt_ref, lp_ref, sg_ref, o_ref, m_sc, l_sc, acc_sc, s_sc = refs
    i = pl.program_id(2)
    q0 = pos0 + i * tq
    shift = CHUNK.bit_length() - 1
    lo = lax.broadcasted_iota(jnp.int32, (1, LANES), 1) < HALF
    spb = q_ref.shape[2] // LANES
    qm = []
    for sl in range(spb):
        q = q_ref[0, :, sl * LANES:(sl + 1) * LANES]
        qm += [jnp.where(lo, q, jnp.zeros_like(q)), jnp.where(lo, jnp.zeros_like(q), q)]
    if fox:
        n_full = (q0 + 1) // tk
        n_need = (q0 + tq - 1) // tk + 1
    else:
        n_full = (((q0 >> shift) + 1) << shift) // tk
        n_need = (((((q0 + tq - 1) >> shift) + 1) << shift) + tk - 1) // tk
    n_need = jnp.minimum(n_need, nk_total)
    n_full = jnp.minimum(jnp.minimum(n_full, kv_len // tk), n_need)

    m_sc[...] = jnp.full_like(m_sc, NEG_INF)
    l_sc[...] = jnp.zeros_like(l_sc)
    acc_sc[...] = jnp.zeros_like(acc_sc)

    n_maps = 2 * spb

    def scores(g, j):
        k0 = j * tk if isinstance(j, int) else pl.multiple_of(j * tk, tk)
        lanes = slice((g // 2) * LANES, (g // 2 + 1) * LANES)
        hk = tk // 2

        def half(r):
            s = _mm_nt(k_ref[0, pl.ds(k0 + r, hk), lanes], qm[g])
            if fox:
                ck = ck_ref[0, pl.ds(k0 + r, hk), g * LANES:(g + 1) * LANES]
                s = s - jnp.concatenate([ck] * (tq // LANES), axis=1)
            return s

        return jnp.concatenate([half(0), half(hk)], axis=0)

    assert AHEAD < n_maps
    for a in range(AHEAD):
        s_sc[a] = scores(a, 0)

    hq, hk = tq // 2, tk // 2
    diag = (tq == tk and pos0 % tk == 0 and kv_len == nk_total * tk and hq % LANES == 0 and hk % CHUNK == 0)

    def right_half(full, extra, op):
        return jnp.concatenate([full[:, :hq], op(full[:, hq:], extra)], axis=1)

    def tile(j, masked):
        k0 = pl.multiple_of(j * tk, tk)
        split = masked and diag
        if split:
            kr = lax.broadcasted_iota(jnp.int32, (hk, hq), 0)
            qr = lax.broadcasted_iota(jnp.int32, (hk, hq), 1)
            tri = (qr >= kr) if fox else ((qr >> shift) >= (kr >> shift))
        elif masked:
            kpos = k0 + lax.broadcasted_iota(jnp.int32, (tk, tq), 0)
            qpos = q0 + lax.broadcasted_iota(jnp.int32, (tk, tq), 1)
            vis = (qpos >= kpos) if fox else ((qpos >> shift) >= (kpos >> shift))
            if kv_len < nk_total * tk:
                vis = vis & (kpos < kv_len)
        for g in range(n_maps):
            sl, c = divmod(g, 2)
            dj, g_ahead = divmod(g + AHEAD, n_maps)
            s_sc[g_ahead] = scores(g_ahead, jnp.minimum(j + dj, nk_total - 1) if dj else j)
            if split:
                s = s_sc[g, :hk, :]
                s = jnp.concatenate([jnp.where(tri, s[:, :hq], NEG_INF), s[:, hq:]], axis=1)
                s_late = jnp.where(tri, s_sc[g, hk:, hq:], NEG_INF)
            else:
                s = s_sc[g]
                if masked:
                    s = jnp.where(vis, s, NEG_INF)
            m_prev = m_sc[g]
            m_tile = jnp.max(s, axis=0, keepdims=True)
            if split:
                m_tile = right_half(m_tile, jnp.max(s_late, axis=0, keepdims=True), jnp.maximum)
            if fox:
                cq = cq_ref[0, sl, c:c + 1, :]
                m_new = jnp.maximum(m_prev, m_tile + cq)
                m_sub = m_new - cq
            else:
                m_new = m_sub = jnp.maximum(m_prev, m_tile)
            p = jnp.exp2(s - m_sub)
            p_sum = jnp.sum(p, axis=0, keepdims=True)
            alpha = jnp.exp2(m_prev - m_new)
            vt = vt_ref[0, sl, j]
            r = slice(c * HALF, (c + 1) * HALF) if fox else slice(0, LANES)
            if split:
                p_late = jnp.exp2(s_late - m_sub[:, hq:])
                p_sum = right_half(p_sum, jnp.sum(p_late, axis=0, keepdims=True), jnp.add)
                pv = right_half(_mm(vt[r, :hk], p.astype(BF16)), _mm(vt[r, hk:], p_late.astype(BF16)), jnp.add)
            else:
                pv = _mm(vt[r, :], p.astype(BF16))
            l_sc[g] = alpha * l_sc[g] + p_sum
            m_sc[g] = m_new
            if fox:
                acc_sc[sl, r, :] = alpha * acc_sc[sl, r, :] + pv
            else:
                acc_sc[g] = alpha * acc_sc[g] + pv

    def full_body(j, carry):
        tile(j, False)
        return carry

    def masked_body(j, carry):
        tile(j, True)
        return carry

    lax.fori_loop(0, n_full, full_body, 0)
    lax.fori_loop(n_full, n_need, masked_body, 0)

    for sl in range(spb):
        a, b = 2 * sl, 2 * sl + 1
        if fox:
            ot = jnp.concatenate([acc_sc[sl, 0:HALF, :] / l_sc[a], acc_sc[sl, HALF:LANES, :] / l_sc[b]], axis=0)
            o = ot.T
        else:
            lp = lp_ref[...]
            lam = (jnp.exp(jnp.sum(lp[0:1] * lp[1:2], axis=-1, keepdims=True))
                   - jnp.exp(jnp.sum(lp[2:3] * lp[3:4], axis=-1, keepdims=True)) + lam_init)
            ot = acc_sc[a] / l_sc[a] - lam * (acc_sc[b] / l_sc[b])
            ot = ot * lax.rsqrt(jnp.mean(ot * ot, axis=0, keepdims=True) + EPS)
            o = (ot.T * sg_ref[...]) * (1.0 - lam_init)
        o_ref[0, :, sl * LANES:(sl + 1) * LANES] = o.astype(BF16)


def _attention(q, k, v, extra, *, pos0, kv_len, fox, lam_init=0.0, tq=ATTN_TQ, tk=ATTN_TK, spb=4):
    bsz, Tq, w = q.shape
    Tk = k.shape[1]
    ns = w // LANES
    Tq_pad = -(-Tq // LANES) * LANES
    tq = _pick_tile(Tq_pad, tq)
    tk = tk if Tk % tk == 0 else Tk
    nk = Tk // tk
    q = jnp.pad(q, ((0, 0), (0, Tq_pad - Tq), (0, 0)))
    vt = v if v.ndim == 5 else v.reshape(bsz, nk, tk, ns, LANES).transpose(0, 3, 1, 4, 2)
    assert vt.shape == (bsz, ns, nk, LANES, tk)
    assert ns % spb == 0
    sw = spb * LANES
    qspec = pl.BlockSpec((1, tq, sw), lambda b, s, i: (b, i, s))
    kspec = pl.BlockSpec((1, Tk, sw), lambda b, s, i: (b, 0, s))
    vspec = pl.BlockSpec((1, spb, nk, LANES, tk), lambda b, s, i: (b, s, 0, 0, 0))
    if fox:
        cq, ck = extra
        extra = (jnp.pad(cq, ((0, 0), (0, 0), (0, 0), (0, Tq_pad - Tq))), ck)
        ck_bytes = Tk * 2 * sw * 4
        ck_mode = dict(pipeline_mode=pl.Buffered(1)) if 2 * ck_bytes > VMEM_LIMIT // 4 else {}
        especs = [pl.BlockSpec((1, spb, 2, tq), lambda b, s, i: (b, s, 0, i)),
                  pl.BlockSpec((1, Tk, 2 * sw), lambda b, s, i: (b, 0, s), **ck_mode)]
    else:
        lp, sg = extra
        especs = [pl.BlockSpec(lp.shape, lambda b, s, i: (0, 0)), pl.BlockSpec(sg.shape, lambda b, s, i: (0, 0))]
    out = pl.pallas_call(
        functools.partial(_attn_kernel, tq=tq, tk=tk, nk_total=nk, kv_len=kv_len, pos0=pos0, fox=fox,
                          lam_init=lam_init),
        grid=(bsz, ns // spb, Tq_pad // tq),
        in_specs=[qspec, kspec, vspec] + especs,
        out_specs=qspec,
        out_shape=jax.ShapeDtypeStruct((bsz, Tq_pad, w), BF16),
        scratch_shapes=[pltpu.VMEM((2 * spb, 1, tq), F32), pltpu.VMEM((2 * spb, 1, tq), F32),
                        pltpu.VMEM((spb if fox else 2 * spb, LANES, tq), F32), pltpu.VMEM((2 * spb, tk, tq), F32)],
        compiler_params=_params("parallel", "parallel", "arbitrary"),
        name="fox_attention" if fox else "diff_attention",
    )(q, k, vt, *extra)
    return out[:, :Tq]


def _rope_table_kernel(inv_ref, c_ref, s1_ref, s2_ref, *, pos0):
    T = c_ref.shape[0]
    pos = (pos0 + lax.broadcasted_iota(jnp.int32, (T, LANES), 0)).astype(F32)
    ang = pos * inv_ref[...]
    lane = lax.broadcasted_iota(jnp.int32, (T, LANES), 1) % HALF
    sn = jnp.sin(ang)
    c_ref[...] = jnp.cos(ang)
    s1_ref[...] = jnp.where(lane < ROT_DIM // 2, -sn, 0.0)
    s2_ref[...] = jnp.where(lane >= ROT_DIM // 2, sn, 0.0)


def _rope_tables(T, pos0):
    half = ROT_DIM // 2
    inv = ROPE_THETA ** (-jnp.arange(half, dtype=F32) * (2.0 / ROT_DIM))
    lane = jnp.arange(LANES) % HALF
    inv_lane = jnp.where(lane < ROT_DIM, inv[lane % half], 0.0).reshape(1, LANES).astype(F32)
    tab = jax.ShapeDtypeStruct((T, LANES), F32)
    return pl.pallas_call(
        functools.partial(_rope_table_kernel, pos0=pos0),
        out_shape=[tab, tab, tab],
        name="rope_tables",
    )(inv_lane)


def _inproj_odd_kernel(*refs, w, vt_tiles):
    x_ref, g_ref, w_ref, c_ref, s1_ref, s2_ref = refs[:6]
    q_ref, k_ref, v_ref, kb_ref, vx_ref = refs[-5:]
    hn = _rms(x_ref[...], g_ref[...]).astype(BF16)
    cs, s1, s2 = c_ref[...], s1_ref[...], s2_ref[...]
    half = ROT_DIM // 2

    def rope(t):
        out = []
        for h in range(w // LANES):
            s = t[:, h * LANES:(h + 1) * LANES]
            out.append(s * cs + pltpu.roll(s, LANES - half, 1) * s1 + pltpu.roll(s, half, 1) * s2)
        return jnp.concatenate(out, axis=1)

    q_ref[...] = (rope(_mm(hn, w_ref[:, 0:w])) * Q_SCALE).astype(BF16)
    _emit_kv(rope(_mm(hn, w_ref[:, w:2 * w])), _mm(hn, w_ref[:, 2 * w:3 * w]), k_ref, v_ref, kb_ref, vx_ref,
             DIFF_HEADS, vt_tiles)


def _inproj_odd(x2, g, w_in, tabs, tab_blocks, tm, layer, stacks, n_layers, vt_tiles_per_seq):
    n, d = x2.shape
    w = w_in.shape[1] // 3
    row = lambda c: pl.BlockSpec((tm, c), lambda i: (i, 0))
    tab = pl.BlockSpec((tm, LANES), lambda i: (i % tab_blocks, 0))
    kv_specs, kv_shapes, extra_in = _kv_out(n, w, DIFF_HEADS, tm, layer, stacks, n_layers, vt_tiles_per_seq)
    return pl.pallas_call(
        functools.partial(_inproj_odd_kernel, w=w, vt_tiles=bool(vt_tiles_per_seq)),
        grid=(n // tm,),
        in_specs=[row(d), _resident((1, d)), _resident((d, 3 * w)), tab, tab, tab]
        + [pl.BlockSpec(memory_space=pl.ANY)] * len(extra_in),
        out_specs=[row(w)] + kv_specs,
        out_shape=[jax.ShapeDtypeStruct((n, w), BF16)] + kv_shapes,
        input_output_aliases={6 + a: 1 + a for a in range(len(extra_in))},
        compiler_params=_params("parallel"),
        name="inproj_odd",
    )(x2, g, w_in, *tabs, *extra_in)


def _history_kv_kernel(ck_ref, cv_ref, kn_ref, vn_ref, k_ref, vt_ref, v_sc, *, time_minor):
    w, tp = k_ref.shape[2], k_ref.shape[1]
    new = kn_ref.shape[1]
    past = ck_ref.shape[4] if time_minor else ck_ref.shape[2]
    slots = (w // LANES, LANES)
    k_ref[0, past:past + new, :] = kn_ref[0]
    k_ref[0, past + new:tp, :] = jnp.zeros((tp - past - new, w), BF16)
    if time_minor:
        k_ref[0, 0:past, :] = ck_ref[0, 0].reshape(w, past).T.astype(BF16)
        vt_ref[0, :, 0, :, 0:past] = cv_ref[0, 0].reshape(*slots, past).astype(BF16)
        v_sc[0:new, :] = vn_ref[0].astype(F32)
        v_sc[new:tp - past, :] = jnp.zeros((tp - past - new, w), F32)
        vt_ref[0, :, 0, :, past:tp] = v_sc[...].T.astype(BF16).reshape(*slots, tp - past)
    else:
        k_ref[0, 0:past, :] = ck_ref[0, 0].reshape(past, w).astype(BF16)
        v_sc[0:past, :] = cv_ref[0, 0].reshape(past, w)
        v_sc[past:past + new, :] = vn_ref[0].astype(F32)
        v_sc[past + new:tp, :] = jnp.zeros((tp - past - new, w), F32)
        vt_ref[0, :, 0] = v_sc[...].T.astype(BF16).reshape(*slots, tp)


def _history_kv(cache_k, cache_v, layer, k_new, v_new, tp):
    _, bsz, past, heads, dh = cache_k.shape
    T, w = k_new.shape[1:]
    time_minor = dh < LANES and past % LANES == 0 and (tp - past) % LANES == 0
    if time_minor:
        cache_k, cache_v = (jnp.transpose(c, (0, 1, 3, 4, 2)) for c in (cache_k, cache_v))
    cspec = pl.BlockSpec((1, 1) + cache_k.shape[2:], lambda b: (layer, b, 0, 0, 0))
    nspec = pl.BlockSpec((1, T, w), lambda b: (b, 0, 0))
    return pl.pallas_call(
        functools.partial(_history_kv_kernel, time_minor=time_minor),
        grid=(bsz,),
        in_specs=[cspec, cspec, nspec, nspec],
        out_specs=[pl.BlockSpec((1, tp, w), lambda b: (b, 0, 0)),
                   pl.BlockSpec((1, w // LANES, 1, LANES, tp), lambda b: (b, 0, 0, 0, 0))],
        out_shape=[jax.ShapeDtypeStruct((bsz, tp, w), BF16),
                   jax.ShapeDtypeStruct((bsz, w // LANES, 1, LANES, tp), BF16)],
        scratch_shapes=[pltpu.VMEM((tp - past if time_minor else tp, w), F32)],
        compiler_params=_params("parallel"),
        name="history_kv",
    )(cache_k, cache_v, k_new, v_new)


def _trunk(x, pos0, caches, wts, tm):
    bsz, T, d = x.shape
    n = bsz * T
    depth = wts["norm_g"].shape[0]
    w = d // 2
    x2 = x.reshape(n, d)
    tt = _pick_tile(T, 512)
    flat = lambda a: a.reshape(n, a.shape[-1])
    seq = lambda a: a.reshape(bsz, T, a.shape[-1])
    n_keys = T if caches is None else T + caches[0].shape[2]
    key_pad = -n_keys % LANES
    pad_keys = lambda a: jnp.pad(a, ((0, 0), (0, key_pad), (0, 0))) if key_pad else a
    rope = None
    n_even, n_odd = (depth + 1) // 2, depth // 2
    vt_tiles_per_seq = T // tm if caches is None and tm == ATTN_TK and T % tm == 0 else 0
    fox_kv = diff_kv = None
    even, odd = [], []
    for l in range(depth):
        g = wts["norm_g"][l].reshape(6, 1, d)
        x2 = _ffn(x2, g[0], g[1], wts["w_ffn_in"], wts["w_ffn_out"], (l, 0), tm)
        if l % 2 == 0:
            e = l // 2
            u, gate, q, logf, *fox_kv, kb, vx = _inproj_even(
                x2, g[2], wts["w_in_even"][e], wts["b_fox_f"][e], tm, e, fox_kv, n_even, vt_tiles_per_seq)
            if caches is None:
                h0 = jnp.zeros((bsz, 1, w), F32)
                buf0 = jnp.zeros((bsz, 8, w), F32)
                kb_all, logf_all = seq(kb), seq(logf)
                vb_all = vx if vt_tiles_per_seq else seq(vx)
            else:
                flogf, lru_h, lru_conv = (c[e] for c in caches[2:5])
                h0 = lru_h.reshape(bsz, 1, w)
                buf0 = jnp.pad(lru_conv, ((0, 0), (8 - (CONV_W - 1), 0), (0, 0)))
                kb_all, vb_all = _history_kv(caches[0], caches[1], e, seq(kb), seq(vx), n_keys + key_pad)
                past_logf = jnp.pad(flogf, ((0, 0), (0, 0), (0, LANES - flogf.shape[-1])))
                logf_all = jnp.concatenate([past_logf, seq(logf)], axis=1)
            ya, h_last, nbuf = _lru(seq(u), seq(gate), h0, buf0, wts["lru_conv_w"][e], wts["lru_conv_b"][e],
                                    wts["lru_wr"][e], wts["lru_wi"][e], wts["lru_br"][e], wts["lru_bi"][e],
                                    wts["lru_lambda"][e], tt)
            Tk = logf_all.shape[1]
            Tp = Tk + key_pad
            ck, cq = _cumsum(pad_keys(logf_all), Tp if Tp <= 2 * CUMSUM_TILE else _pick_tile(Tp, CUMSUM_TILE))
            yb = _attention(seq(q), kb_all, vb_all, (cq[..., Tk - T:Tk], ck), pos0=pos0, kv_len=Tk, fox=True)
            mix = (g[3], wts["w_out_even"][e], [flat(ya), flat(yb)])
            even.append((seq(logf)[..., :FOX_HEADS], h_last.reshape(bsz, w), nbuf))
        else:
            o = l // 2
            lam_init = 0.8 - 0.6 * math.exp(-0.3 * l)
            if rope is None:
                tabs = _rope_tables(T, pos0)
                if T % tm:
                    tabs = [jnp.tile(t, (n // T, 1)) for t in tabs]
                rope = (tabs, tabs[0].shape[0] // tm)
            q, *diff_kv, kb, vx = _inproj_odd(x2, g[2], wts["w_in_odd"][o], rope[0], rope[1], tm, o, diff_kv,
                                              n_odd, vt_tiles_per_seq)
            if caches is None:
                kb_all = seq(kb)
                vb_all = vx if vt_tiles_per_seq else seq(vx)
            else:
                kb_all, vb_all = _history_kv(caches[5], caches[6], o, seq(kb), seq(vx), n_keys + key_pad)
            y = _attention(seq(q), kb_all, vb_all, (wts["diff_lambda"][o], wts["diff_subln_g"][o]),
                           pos0=pos0, kv_len=n_keys, fox=False, lam_init=lam_init)
            mix = (g[3], wts["w_out_odd"][o], [flat(y)])
        x2 = _ffn(x2, g[4], g[5], wts["w_ffn_in"], wts["w_ffn_out"], (l, 1), tm, mix)
    per_seq = lambda a: (jnp.transpose(a, (0, 1, 4, 2, 3)) if a.ndim == 5
                         else a.reshape(a.shape[0], bsz, T, *a.shape[2:]))
    even_out = [per_seq(a) for a in fox_kv] + [jnp.stack(z) for z in zip(*even)]
    return x2.reshape(bsz, T, d), even_out, [per_seq(a) for a in diff_kv]


def _prepare_weights(norm_g, w_ffn_in, w_ffn_out, w_in_even, b_fox_f, lru_conv_w, lru_conv_b, lru_w_gates,
                     lru_b_gates, lru_lambda, w_out_even, w_in_odd, diff_lambda, diff_subln_g, w_out_odd):
    n_even, d, _ = w_in_even.shape
    w = d // 2
    blk = w // LRU_BLOCKS
    pad_lanes = lambda a: jnp.pad(a, [(0, 0)] * (a.ndim - 1) + [(0, LANES - a.shape[-1])])
    eye = jnp.eye(LRU_BLOCKS, dtype=F32)
    dense = lambda g: jnp.einsum("encg,nm->encmg", g, eye).reshape(n_even, w, w)
    return dict(
        norm_g=norm_g, w_ffn_in=w_ffn_in.astype(BF16), w_ffn_out=w_ffn_out.astype(BF16),
        w_in_even=jnp.concatenate([w_in_even[:, :, :5 * w], pad_lanes(w_in_even[:, :, 5 * w:])],
                                  axis=-1).astype(BF16),
        b_fox_f=pad_lanes(b_fox_f).reshape(n_even, 1, LANES),
        lru_conv_w=lru_conv_w, lru_conv_b=lru_conv_b.reshape(n_even, 1, w),
        lru_wr=dense(lru_w_gates[..., :blk]).astype(BF16), lru_wi=dense(lru_w_gates[..., blk:]).astype(BF16),
        lru_br=lru_b_gates[..., :blk].reshape(n_even, 1, w), lru_bi=lru_b_gates[..., blk:].reshape(n_even, 1, w),
        lru_lambda=lru_lambda.reshape(n_even, 1, w), w_out_even=w_out_even.astype(BF16),
        w_in_odd=w_in_odd.astype(BF16), diff_lambda=diff_lambda, diff_subln_g=diff_subln_g.reshape(-1, 1, LANES),
        w_out_odd=w_out_odd.astype(BF16))


def kernel(x_prompt, x_sample, cache_fox_k, cache_fox_v, cache_fox_logf, state_lru_h, state_lru_conv, cache_diff_k, cache_diff_v, norm_g, w_ffn_in, w_ffn_out, w_in_even, b_fox_f, lru_conv_w, lru_conv_b, lru_w_gates, lru_b_gates, lru_lambda, w_out_even, w_in_odd, diff_lambda, diff_subln_g, w_out_odd):
    wts = _prepare_weights(norm_g, w_ffn_in, w_ffn_out, w_in_even, b_fox_f, lru_conv_w, lru_conv_b, lru_w_gates,
                           lru_b_gates, lru_lambda, w_out_even, w_in_odd, diff_lambda, diff_subln_g, w_out_odd)
    y_p, pe, po = _trunk(x_prompt, 0, None, wts, _pick_tile(x_prompt.shape[0] * x_prompt.shape[1], 512))
    caches = (cache_fox_k, cache_fox_v, cache_fox_logf, state_lru_h, state_lru_conv, cache_diff_k, cache_diff_v)
    y_s, se, so = _trunk(x_sample, cache_fox_k.shape[2], caches, wts,
                         _pick_tile(x_sample.shape[0] * x_sample.shape[1], 256))
    return (y_p, y_s, *pe, *po, *se, *so)
```

```python
import functools
import math

import jax
import jax.numpy as jnp
from jax import lax
from jax.experimental import pallas as pl
from jax.experimental.pallas import tpu as pltpu

F32 = jnp.float32
BF16 = jnp.bfloat16

EPS = 1e-6
NEG_INF = -1e30
LRU_C = 8.0
LRU_BLOCKS = 8
CONV_W = 4
FOX_HEADS = 8
DIFF_HEADS = 8
CHUNK = 64
ROT_DIM = 16
ROPE_THETA = 500000.0

LANES = 128
SUBLANES = 8
HALF = LANES // 2
FF_CHUNK = 256
FFN_SUB = 512
AHEAD = 2
ATTN_TQ, ATTN_TK = 512, 512
CUMSUM_TILE = 1024
LOG2E = 1.0 / math.log(2.0)
Q_SCALE = HALF ** -0.5 * LOG2E
VMEM_LIMIT = 56 * 1024 * 1024


def _params(*sem):
    return pltpu.CompilerParams(dimension_semantics=sem, vmem_limit_bytes=VMEM_LIMIT)


def _resident(shape, lead=()):
    nd = len(shape)
    return pl.BlockSpec((None,) * len(lead) + tuple(shape), lambda *_: tuple(lead) + (0,) * nd,
                        pipeline_mode=pl.Buffered(1))


def _rms(x, g):
    return x * lax.rsqrt(jnp.mean(x * x, axis=-1, keepdims=True) + EPS) * g


def _mm(a, b):
    return jnp.dot(a, b, preferred_element_type=F32)


def _mm_nt(a, b):
    return lax.dot_general(a, b, (((1,), (1,)), ((), ())), preferred_element_type=F32)


def _log_sigmoid(x):
    return jnp.minimum(x, 0.0) - jnp.log1p(jnp.exp(-jnp.abs(x)))


def _softplus(x):
    return jnp.maximum(x, 0.0) + jnp.log1p(jnp.exp(-jnp.abs(x)))


def _pick_tile(n, target):
    t = min(n, target)
    while n % t:
        t //= 2
    return t


def _ffn_kernel(x_ref, gpre_ref, gpost_ref, wi_ref, wo_ref, *rest, dff):
    *mix, o_ref, act_ref = rest
    sub = min(FFN_SUB, x_ref.shape[0])
    for r0 in range(0, x_ref.shape[0], sub):
        rows = slice(r0, r0 + sub)
        x = x_ref[rows, :]
        if mix:
            gmix_ref, wmix_ref, *y_refs = mix
            y = jnp.concatenate([r[rows, :] for r in y_refs], axis=1)
            x = x + _rms(_mm(y, wmix_ref[...]), gmix_ref[...])
        xn = _rms(x, gpre_ref[...]).astype(BF16)
        for c0 in range(0, dff, FF_CHUNK):
            g = _mm(xn, wi_ref[:, c0:c0 + FF_CHUNK])
            u = _mm(xn, wi_ref[:, dff + c0:dff + c0 + FF_CHUNK])
            act_ref[rows, c0:c0 + FF_CHUNK] = ((g * jax.nn.sigmoid(g)) * u).astype(BF16)
        y = _mm(act_ref[rows, :], wo_ref[...])
        o_ref[rows, :] = x + 0.5 * _rms(y, gpost_ref[...])


def _ffn(x2, g_pre, g_post, w_in, w_out, which, tm, mix=None):
    n, d = x2.shape
    dff = w_out.shape[-2]
    assert dff % FF_CHUNK == 0 and n % tm == 0
    row = lambda c: pl.BlockSpec((tm, c), lambda i: (i, 0))
    mix_in, mix_specs = [], []
    if mix is not None:
        g_mix, w_mix, ys = mix
        mix_in = [g_mix, w_mix, *ys]
        mix_specs = [_resident((1, d)), _resident(w_mix.shape)] + [row(y.shape[1]) for y in ys]
    return pl.pallas_call(
        functools.partial(_ffn_kernel, dff=dff),
        grid=(n // tm,),
        in_specs=[row(d), _resident((1, d)), _resident((1, d)), _resident((d, 2 * dff), which),
                  _resident((dff, d), which)] + mix_specs,
        out_specs=row(d),
        out_shape=jax.ShapeDtypeStruct((n, d), F32),
        scratch_shapes=[pltpu.VMEM((tm, dff), BF16)],
        compiler_params=_params("parallel"),
        name="ffn_mix" if mix is not None else "ffn",
    )(x2, g_pre, g_post, w_in, w_out, *mix_in)


def _emit_kv(k, v, k_ref, v_ref, kb_ref, vx_ref, heads, vt_tiles):
    tm, w = k.shape
    vt = v.T if vt_tiles else None
    if k_ref.ndim == 5:
        k_out, v_out = (t.reshape(1, heads, w // heads, tm) for t in (k.T, vt))
    else:
        k_out, v_out = (t.reshape(tm, heads, w // heads) for t in (k, v))
    for l in range(k_ref.shape[0]):
        k_ref[l] = k_out if l == 0 else jnp.zeros_like(k_out)
        v_ref[l] = v_out if l == 0 else jnp.zeros_like(v_out)
    kb_ref[...] = k.astype(BF16)
    if vt_tiles:
        vx_ref[0, :, 0] = vt.astype(BF16).reshape(w // LANES, LANES, tm)
    else:
        vx_ref[...] = v.astype(BF16)


def _kv_out(n, w, heads, tm, layer, stacks, n_layers, vt_tiles_per_seq):
    dh = w // heads
    assert (stacks is None) == (layer == 0)
    own_layers = n_layers if stacks is None else 1
    nkb = vt_tiles_per_seq
    if nkb and dh < LANES:
        stacked = pl.BlockSpec((own_layers, 1, heads, dh, tm), lambda i: (layer, i // nkb, 0, 0, i % nkb))
        stacked_shape = jax.ShapeDtypeStruct((n_layers, n // (nkb * tm), heads, dh, nkb * tm), F32)
    else:
        stacked = pl.BlockSpec((own_layers, tm, heads, dh), lambda i: (layer, i, 0, 0))
        stacked_shape = jax.ShapeDtypeStruct((n_layers, n, heads, dh), F32)
    row = pl.BlockSpec((tm, w), lambda i: (i, 0))
    if vt_tiles_per_seq:
        vx = pl.BlockSpec((1, w // LANES, 1, LANES, tm), lambda i: (i // nkb, 0, i % nkb, 0, 0))
        vx_shape = jax.ShapeDtypeStruct((n // (nkb * tm), w // LANES, nkb, LANES, tm), BF16)
    else:
        vx, vx_shape = row, jax.ShapeDtypeStruct((n, w), BF16)
    specs = [stacked, stacked, row, vx]
    shapes = [stacked_shape, stacked_shape, jax.ShapeDtypeStruct((n, w), BF16), vx_shape]
    extra_in = [] if stacks is None else list(stacks)
    return specs, shapes, extra_in


def _inproj_even_kernel(*refs, w, vt_tiles):
    x_ref, g_ref, w_ref, bf_ref = refs[:4]
    u_ref, gate_ref, q_ref, logf_ref, k_ref, v_ref, kb_ref, vx_ref = refs[-8:]
    hn = _rms(x_ref[...], g_ref[...]).astype(BF16)
    u_ref[...] = _mm(hn, w_ref[:, 0:w])
    gate_ref[...] = _mm(hn, w_ref[:, w:2 * w])
    q_ref[...] = (_mm(hn, w_ref[:, 2 * w:3 * w]) * Q_SCALE).astype(BF16)
    logf_ref[...] = _log_sigmoid(_mm(hn, w_ref[:, 5 * w:5 * w + LANES]) + bf_ref[...])
    _emit_kv(_mm(hn, w_ref[:, 3 * w:4 * w]), _mm(hn, w_ref[:, 4 * w:5 * w]), k_ref, v_ref, kb_ref, vx_ref,
             FOX_HEADS, vt_tiles)


def _inproj_even(x2, g, w_wide, bf_wide, tm, layer, stacks, n_layers, vt_tiles_per_seq):
    n, d = x2.shape
    w = (w_wide.shape[1] - LANES) // 5
    row = lambda c: pl.BlockSpec((tm, c), lambda i: (i, 0))
    f = jax.ShapeDtypeStruct((n, w), F32)
    b = jax.ShapeDtypeStruct((n, w), BF16)
    kv_specs, kv_shapes, extra_in = _kv_out(n, w, FOX_HEADS, tm, layer, stacks, n_layers, vt_tiles_per_seq)
    return pl.pallas_call(
        functools.partial(_inproj_even_kernel, w=w, vt_tiles=bool(vt_tiles_per_seq)),
        grid=(n // tm,),
        in_specs=[row(d), _resident((1, d)), _resident(w_wide.shape), _resident(bf_wide.shape)]
        + [pl.BlockSpec(memory_space=pl.ANY)] * len(extra_in),
        out_specs=[row(w)] * 3 + [row(LANES)] + kv_specs,
        out_shape=[f, f, b, jax.ShapeDtypeStruct((n, LANES), F32)] + kv_shapes,
        input_output_aliases={4 + a: 4 + a for a in range(len(extra_in))},
        compiler_params=_params("parallel"),
        name="inproj_even",
    )(x2, g, w_wide, bf_wide, *extra_in)


def _scan_rows(a, b, h):
    tt, c = b.shape
    assert tt % SUBLANES == 0
    groups = (tt // SUBLANES, SUBLANES, c)
    b = b.reshape(groups)
    a = None if a is None else a.reshape(groups)
    sub = lax.broadcasted_iota(jnp.int32, groups, 1)
    d = 1
    while d < SUBLANES:
        keep = sub >= d
        if a is None:
            b = b + jnp.where(keep, pltpu.roll(b, d, 1), 0.0)
        else:
            b = a * jnp.where(keep, pltpu.roll(b, d, 1), 0.0) + b
            a = a * jnp.where(keep, pltpu.roll(a, d, 1), 1.0)
        d *= 2
    out = []
    for g in range(groups[0]):
        blk = b[g] + (h if a is None else a[g] * h)
        out.append(blk)
        h = blk[SUBLANES - 1:SUBLANES]
    return jnp.concatenate(out, axis=0)


def _lru_kernel(u_ref, gate_ref, h0_ref, buf0_ref, cw_ref, cb_ref, wr_ref, wi_ref, br_ref, bi_ref, lam_ref,
                ya_ref, hlast_ref, nbuf_ref, ubuf, hcar, *, tt):
    t = pl.program_id(1)
    pad = ubuf.shape[0] - tt

    @pl.when(t == 0)
    def _():
        ubuf[0:pad, :] = buf0_ref[0]
        hcar[...] = h0_ref[0]

    u = u_ref[0]
    ubuf[pad:pad + tt, :] = u
    uc = cb_ref[...]
    for j in range(CONV_W):
        o = pad - (CONV_W - 1) + j
        uc = uc + ubuf[o:o + tt, :] * cw_ref[j:j + 1, :]
    ucb = uc.astype(BF16)
    r = jax.nn.sigmoid(_mm(ucb, wr_ref[...]) + br_ref[...])
    ig = jax.nn.sigmoid(_mm(ucb, wi_ref[...]) + bi_ref[...])
    log_a = (-LRU_C * r) * _softplus(-lam_ref[...])
    a = jnp.exp(log_a)
    bx = jnp.sqrt(-jnp.tanh(log_a) * (a * a + 1.0)) * (ig * uc)
    bx = _scan_rows(a, bx, hcar[...])
    hcar[...] = bx[tt - 1:tt, :]
    ya_ref[0] = (bx * jax.nn.gelu(gate_ref[0])).astype(BF16)
    ubuf[0:pad, :] = ubuf[tt:tt + pad, :]

    @pl.when(t == pl.num_programs(1) - 1)
    def _():
        hlast_ref[0] = bx[tt - 1:tt, :]
        nbuf_ref[0] = ubuf[pad - (CONV_W - 1):pad, :]


def _lru(u, gate, h0, buf0, cw, cb, wr, wi, br, bi, lam, tt):
    bsz, T, w = u.shape
    pad = 8
    seq = pl.BlockSpec((1, tt, w), lambda b, t: (b, t, 0))
    per_b = lambda r: pl.BlockSpec((1, r, w), lambda b, t: (b, 0, 0))
    return pl.pallas_call(
        functools.partial(_lru_kernel, tt=tt),
        grid=(bsz, T // tt),
        in_specs=[seq, seq, per_b(1), per_b(pad), _resident((CONV_W, w)), _resident((1, w)), _resident((w, w)),
                  _resident((w, w)), _resident((1, w)), _resident((1, w)), _resident((1, w))],
        out_specs=[seq, per_b(1), per_b(CONV_W - 1)],
        out_shape=[jax.ShapeDtypeStruct((bsz, T, w), BF16), jax.ShapeDtypeStruct((bsz, 1, w), F32),
                   jax.ShapeDtypeStruct((bsz, CONV_W - 1, w), F32)],
        scratch_shapes=[pltpu.VMEM((tt + pad, w), F32), pltpu.VMEM((1, w), F32)],
        compiler_params=_params("parallel", "arbitrary"),
        name="conv_rglru",
    )(u, gate, h0, buf0, cw, cb, wr, wi, br, bi, lam)


def _cumsum_kernel(x_ref, ck_ref, cq_ref, car, *, tt):
    @pl.when(pl.program_id(1) == 0)
    def _():
        car[...] = jnp.zeros_like(car)

    x = _scan_rows(None, x_ref[0], car[...])
    car[...] = x[tt - 1:tt, :]
    x = x * LOG2E
    tr = x.T
    for h in range(FOX_HEADS):
        ck_ref[0, :, h * LANES:(h + 1) * LANES] = jnp.broadcast_to(x[:, h:h + 1], (tt, LANES))
        cq_ref[0, h // 2, h % 2:h % 2 + 1, :] = tr[h:h + 1, :]


def _cumsum(x, tt):
    bsz, T, _ = x.shape
    ns = FOX_HEADS // 2
    return pl.pallas_call(
        functools.partial(_cumsum_kernel, tt=tt),
        grid=(bsz, T // tt),
        in_specs=[pl.BlockSpec((1, tt, LANES), lambda b, t: (b, t, 0))],
        out_specs=[pl.BlockSpec((1, tt, FOX_HEADS * LANES), lambda b, t: (b, t, 0)),
                   pl.BlockSpec((1, ns, 2, tt), lambda b, t: (b, 0, 0, t))],
        out_shape=[jax.ShapeDtypeStruct((bsz, T, FOX_HEADS * LANES), F32),
                   jax.ShapeDtypeStruct((bsz, ns, 2, T), F32)],
        scratch_shapes=[pltpu.VMEM((1, LANES), F32)],
        compiler_params=_params("parallel", "arbitrary"),
        name="logf_cumsum",
    )(x)


def _attn_kernel(*refs, tq, tk, nk_total, kv_len, pos0, fox, lam_init):
    if fox:
        q_ref, k_ref, vt_ref, cq_ref, ck_ref, o_ref, m_sc, l_sc, acc_sc, s_sc = refs
    else:
        q_ref, k_ref, vt_ref, lp_ref, sg_ref, o_ref, m_sc, l_sc, acc_sc, s_sc = refs
    i = pl.program_id(2)
    q0 = pos0 + i * tq
    shift = CHUNK.bit_length() - 1
    lo = lax.broadcasted_iota(jnp.int32, (1, LANES), 1) < HALF
    spb = q_ref.shape[2] // LANES
    qm = []
    for sl in range(spb):
        q = q_ref[0, :, sl * LANES:(sl + 1) * LANES]
        qm += [jnp.where(lo, q, jnp.zeros_like(q)), jnp.where(lo, jnp.zeros_like(q), q)]
    if fox:
        n_full = (q0 + 1) // tk
        n_need = (q0 + tq - 1) // tk + 1
    else:
        n_full = (((q0 >> shift) + 1) << shift) // tk
        n_need = (((((q0 + tq - 1) >> shift) + 1) << shift) + tk - 1) // tk
    n_need = jnp.minimum(n_need, nk_total)
    n_full = jnp.minimum(jnp.minimum(n_full, kv_len // tk), n_need)

    m_sc[...] = jnp.full_like(m_sc, NEG_INF)
    l_sc[...] = jnp.zeros_like(l_sc)
    acc_sc[...] = jnp.zeros_like(acc_sc)

    n_maps = 2 * spb

    def scores(g, j):
        k0 = j * tk if isinstance(j, int) else pl.multiple_of(j * tk, tk)
        lanes = slice((g // 2) * LANES, (g // 2 + 1) * LANES)
        hk = tk // 2

        def half(r):
            s = _mm_nt(k_ref[0, pl.ds(k0 + r, hk), lanes], qm[g])
            if fox:
                ck = ck_ref[0, pl.ds(k0 + r, hk), g * LANES:(g + 1) * LANES]
                s = s - jnp.concatenate([ck] * (tq // LANES), axis=1)
            return s

        return jnp.concatenate([half(0), half(hk)], axis=0)

    assert AHEAD < n_maps
    for a in range(AHEAD):
        s_sc[a] = scores(a, 0)

    hq, hk = tq // 2, tk // 2
    diag = (tq == tk and pos0 % tk == 0 and kv_len == nk_total * tk and hq % LANES == 0 and hk % CHUNK == 0)

    def right_half(full, extra, op):
        return jnp.concatenate([full[:, :hq], op(full[:, hq:], extra)], axis=1)

    def tile(j, masked):
        k0 = pl.multiple_of(j * tk, tk)
        split = masked and diag
        if split:
            kr = lax.broadcasted_iota(jnp.int32, (hk, hq), 0)
            qr = lax.broadcasted_iota(jnp.int32, (hk, hq), 1)
            tri = (qr >= kr) if fox else ((qr >> shift) >= (kr >> shift))
        elif masked:
            kpos = k0 + lax.broadcasted_iota(jnp.int32, (tk, tq), 0)
            qpos = q0 + lax.broadcasted_iota(jnp.int32, (tk, tq), 1)
            vis = (qpos >= kpos) if fox else ((qpos >> shift) >= (kpos >> shift))
            if kv_len < nk_total * tk:
                vis = vis & (kpos < kv_len)
        for g in range(n_maps):
            sl, c = divmod(g, 2)
            dj, g_ahead = divmod(g + AHEAD, n_maps)
            s_sc[g_ahead] = scores(g_ahead, jnp.minimum(j + dj, nk_total - 1) if dj else j)
            if split:
                s = s_sc[g, :hk, :]
                s = jnp.concatenate([jnp.where(tri, s[:, :hq], NEG_INF), s[:, hq:]], axis=1)
                s_late = jnp.where(tri, s_sc[g, hk:, hq:], NEG_INF)
            else:
                s = s_sc[g]
                if masked:
                    s = jnp.where(vis, s, NEG_INF)
            m_prev = m_sc[g]
            m_tile = jnp.max(s, axis=0, keepdims=True)
            if split:
                m_tile = right_half(m_tile, jnp.max(s_late, axis=0, keepdims=True), jnp.maximum)
            if fox:
                cq = cq_ref[0, sl, c:c + 1, :]
                m_new = jnp.maximum(m_prev, m_tile + cq)
                m_sub = m_new - cq
            else:
                m_new = m_sub = jnp.maximum(m_prev, m_tile)
            p = jnp.exp2(s - m_sub)
            p_sum = jnp.sum(p, axis=0, keepdims=True)
            alpha = jnp.exp2(m_prev - m_new)
            vt = vt_ref[0, sl, j]
            r = slice(c * HALF, (c + 1) * HALF) if fox else slice(0, LANES)
            if split:
                p_late = jnp.exp2(s_late - m_sub[:, hq:])
                p_sum = right_half(p_sum, jnp.sum(p_late, axis=0, keepdims=True), jnp.add)
                pv = right_half(_mm(vt[r, :hk], p.astype(BF16)), _mm(vt[r, hk:], p_late.astype(BF16)), jnp.add)
            else:
                pv = _mm(vt[r, :], p.astype(BF16))
            l_sc[g] = alpha * l_sc[g] + p_sum
            m_sc[g] = m_new
            if fox:
                acc_sc[sl, r, :] = alpha * acc_sc[sl, r, :] + pv
            else:
                acc_sc[g] = alpha * acc_sc[g] + pv

    def full_body(j, carry):
        tile(j, False)
        return carry

    def masked_body(j, carry):
        tile(j, True)
        return carry

    lax.fori_loop(0, n_full, full_body, 0)
    lax.fori_loop(n_full, n_need, masked_body, 0)

    for sl in range(spb):
        a, b = 2 * sl, 2 * sl + 1
        if fox:
            ot = jnp.concatenate([acc_sc[sl, 0:HALF, :] / l_sc[a], acc_sc[sl, HALF:LANES, :] / l_sc[b]], axis=0)
            o = ot.T
        else:
            lp = lp_ref[...]
            lam = (jnp.exp(jnp.sum(lp[0:1] * lp[1:2], axis=-1, keepdims=True))
                   - jnp.exp(jnp.sum(lp[2:3] * lp[3:4], axis=-1, keepdims=True)) + lam_init)
            ot = acc_sc[a] / l_sc[a] - lam * (acc_sc[b] / l_sc[b])
            ot = ot * lax.rsqrt(jnp.mean(ot * ot, axis=0, keepdims=True) + EPS)
            o = (ot.T * sg_ref[...]) * (1.0 - lam_init)
        o_ref[0, :, sl * LANES:(sl + 1) * LANES] = o.astype(BF16)


def _attention(q, k, v, extra, *, pos0, kv_len, fox, lam_init=0.0, tq=ATTN_TQ, tk=ATTN_TK, spb=4):
    bsz, Tq, w = q.shape
    Tk = k.shape[1]
    ns = w // LANES
    Tq_pad = -(-Tq // LANES) * LANES
    tq = _pick_tile(Tq_pad, tq)
    tk = tk if Tk % tk == 0 else Tk
    nk = Tk // tk
    q = jnp.pad(q, ((0, 0), (0, Tq_pad - Tq), (0, 0)))
    vt = v if v.ndim == 5 else v.reshape(bsz, nk, tk, ns, LANES).transpose(0, 3, 1, 4, 2)
    assert vt.shape == (bsz, ns, nk, LANES, tk)
    assert ns % spb == 0
    sw = spb * LANES
    qspec = pl.BlockSpec((1, tq, sw), lambda b, s, i: (b, i, s))
    kspec = pl.BlockSpec((1, Tk, sw), lambda b, s, i: (b, 0, s))
    vspec = pl.BlockSpec((1, spb, nk, LANES, tk), lambda b, s, i: (b, s, 0, 0, 0))
    if fox:
        cq, ck = extra
        extra = (jnp.pad(cq, ((0, 0), (0, 0), (0, 0), (0, Tq_pad - Tq))), ck)
        ck_bytes = Tk * 2 * sw * 4
        ck_mode = dict(pipeline_mode=pl.Buffered(1)) if 2 * ck_bytes > VMEM_LIMIT // 4 else {}
        especs = [pl.BlockSpec((1, spb, 2, tq), lambda b, s, i: (b, s, 0, i)),
                  pl.BlockSpec((1, Tk, 2 * sw), lambda b, s, i: (b, 0, s), **ck_mode)]
    else:
        lp, sg = extra
        especs = [pl.BlockSpec(lp.shape, lambda b, s, i: (0, 0)), pl.BlockSpec(sg.shape, lambda b, s, i: (0, 0))]
    out = pl.pallas_call(
        functools.partial(_attn_kernel, tq=tq, tk=tk, nk_total=nk, kv_len=kv_len, pos0=pos0, fox=fox,
                          lam_init=lam_init),
        grid=(bsz, ns // spb, Tq_pad // tq),
        in_specs=[qspec, kspec, vspec] + especs,
        out_specs=qspec,
        out_shape=jax.ShapeDtypeStruct((bsz, Tq_pad, w), BF16),
        scratch_shapes=[pltpu.VMEM((2 * spb, 1, tq), F32), pltpu.VMEM((2 * spb, 1, tq), F32),
                        pltpu.VMEM((spb if fox else 2 * spb, LANES, tq), F32), pltpu.VMEM((2 * spb, tk, tq), F32)],
        compiler_params=_params("parallel", "parallel", "arbitrary"),
        name="fox_attention" if fox else "diff_attention",
    )(q, k, vt, *extra)
    return out[:, :Tq]


def _rope_table_kernel(inv_ref, c_ref, s1_ref, s2_ref, *, pos0):
    T = c_ref.shape[0]
    pos = (pos0 + lax.broadcasted_iota(jnp.int32, (T, LANES), 0)).astype(F32)
    ang = pos * inv_ref[...]
    lane = lax.broadcasted_iota(jnp.int32, (T, LANES), 1) % HALF
    sn = jnp.sin(ang)
    c_ref[...] = jnp.cos(ang)
    s1_ref[...] = jnp.where(lane < ROT_DIM // 2, -sn, 0.0)
    s2_ref[...] = jnp.where(lane >= ROT_DIM // 2, sn, 0.0)


def _rope_tables(T, pos0):
    half = ROT_DIM // 2
    inv = ROPE_THETA ** (-jnp.arange(half, dtype=F32) * (2.0 / ROT_DIM))
    lane = jnp.arange(LANES) % HALF
    inv_lane = jnp.where(lane < ROT_DIM, inv[lane % half], 0.0).reshape(1, LANES).astype(F32)
    tab = jax.ShapeDtypeStruct((T, LANES), F32)
    return pl.pallas_call(
        functools.partial(_rope_table_kernel, pos0=pos0),
        out_shape=[tab, tab, tab],
        name="rope_tables",
    )(inv_lane)


def _inproj_odd_kernel(*refs, w, vt_tiles):
    x_ref, g_ref, w_ref, c_ref, s1_ref, s2_ref = refs[:6]
    q_ref, k_ref, v_ref, kb_ref, vx_ref = refs[-5:]
    hn = _rms(x_ref[...], g_ref[...]).astype(BF16)
    cs, s1, s2 = c_ref[...], s1_ref[...], s2_ref[...]
    half = ROT_DIM // 2

    def rope(t):
        out = []
        for h in range(w // LANES):
            s = t[:, h * LANES:(h + 1) * LANES]
            out.append(s * cs + pltpu.roll(s, LANES - half, 1) * s1 + pltpu.roll(s, half, 1) * s2)
        return jnp.concatenate(out, axis=1)

    q_ref[...] = (rope(_mm(hn, w_ref[:, 0:w])) * Q_SCALE).astype(BF16)
    _emit_kv(rope(_mm(hn, w_ref[:, w:2 * w])), _mm(hn, w_ref[:, 2 * w:3 * w]), k_ref, v_ref, kb_ref, vx_ref,
             DIFF_HEADS, vt_tiles)


def _inproj_odd(x2, g, w_in, tabs, tab_blocks, tm, layer, stacks, n_layers, vt_tiles_per_seq):
    n, d = x2.shape
    w = w_in.shape[1] // 3
    row = lambda c: pl.BlockSpec((tm, c), lambda i: (i, 0))
    tab = pl.BlockSpec((tm, LANES), lambda i: (i % tab_blocks, 0))
    kv_specs, kv_shapes, extra_in = _kv_out(n, w, DIFF_HEADS, tm, layer, stacks, n_layers, vt_tiles_per_seq)
    return pl.pallas_call(
        functools.partial(_inproj_odd_kernel, w=w, vt_tiles=bool(vt_tiles_per_seq)),
        grid=(n // tm,),
        in_specs=[row(d), _resident((1, d)), _resident((d, 3 * w)), tab, tab, tab]
        + [pl.BlockSpec(memory_space=pl.ANY)] * len(extra_in),
        out_specs=[row(w)] + kv_specs,
        out_shape=[jax.ShapeDtypeStruct((n, w), BF16)] + kv_shapes,
        input_output_aliases={6 + a: 1 + a for a in range(len(extra_in))},
        compiler_params=_params("parallel"),
        name="inproj_odd",
    )(x2, g, w_in, *tabs, *extra_in)


def _history_kv_kernel(ck_ref, cv_ref, kn_ref, vn_ref, k_ref, vt_ref, v_sc, *, time_minor):
    w, tp = k_ref.shape[2], k_ref.shape[1]
    new = kn_ref.shape[1]
    past = ck_ref.shape[4] if time_minor else ck_ref.shape[2]
    slots = (w // LANES, LANES)
    k_ref[0, past:past + new, :] = kn_ref[0]
    k_ref[0, past + new:tp, :] = jnp.zeros((tp - past - new, w), BF16)
    if time_minor:
        k_ref[0, 0:past, :] = ck_ref[0, 0].reshape(w, past).T.astype(BF16)
        vt_ref[0, :, 0, :, 0:past] = cv_ref[0, 0].reshape(*slots, past).astype(BF16)
        v_sc[0:new, :] = vn_ref[0].astype(F32)
        v_sc[new:tp - past, :] = jnp.zeros((tp - past - new, w), F32)
        vt_ref[0, :, 0, :, past:tp] = v_sc[...].T.astype(BF16).reshape(*slots, tp - past)
    else:
        k_ref[0, 0:past, :] = ck_ref[0, 0].reshape(past, w).astype(BF16)
        v_sc[0:past, :] = cv_ref[0, 0].reshape(past, w)
        v_sc[past:past + new, :] = vn_ref[0].astype(F32)
        v_sc[past + new:tp, :] = jnp.zeros((tp - past - new, w), F32)
        vt_ref[0, :, 0] = v_sc[...].T.astype(BF16).reshape(*slots, tp)


def _history_kv(cache_k, cache_v, layer, k_new, v_new, tp):
    _, bsz, past, heads, dh = cache_k.shape
    T, w = k_new.shape[1:]
    time_minor = dh < LANES and past % LANES == 0 and (tp - past) % LANES == 0
    if time_minor:
        cache_k, cache_v = (jnp.transpose(c, (0, 1, 3, 4, 2)) for c in (cache_k, cache_v))
    cspec = pl.BlockSpec((1, 1) + cache_k.shape[2:], lambda b: (layer, b, 0, 0, 0))
    nspec = pl.BlockSpec((1, T, w), lambda b: (b, 0, 0))
    return pl.pallas_call(
        functools.partial(_history_kv_kernel, time_minor=time_minor),
        grid=(bsz,),
        in_specs=[cspec, cspec, nspec, nspec],
        out_specs=[pl.BlockSpec((1, tp, w), lambda b: (b, 0, 0)),
                   pl.BlockSpec((1, w // LANES, 1, LANES, tp), lambda b: (b, 0, 0, 0, 0))],
        out_shape=[jax.ShapeDtypeStruct((bsz, tp, w), BF16),
                   jax.ShapeDtypeStruct((bsz, w // LANES, 1, LANES, tp), BF16)],
        scratch_shapes=[pltpu.VMEM((tp - past if time_minor else tp, w), F32)],
        compiler_params=_params("parallel"),
        name="history_kv",
    )(cache_k, cache_v, k_new, v_new)


def _trunk(x, pos0, caches, wts, tm):
    bsz, T, d = x.shape
    n = bsz * T
    depth = wts["norm_g"].shape[0]
    w = d // 2
    x2 = x.reshape(n, d)
    tt = _pick_tile(T, 512)
    tm_ffn = _pick_tile(n, 2 * FFN_SUB)
    flat = lambda a: a.reshape(n, a.shape[-1])
    seq = lambda a: a.reshape(bsz, T, a.shape[-1])
    n_keys = T if caches is None else T + caches[0].shape[2]
    key_pad = -n_keys % LANES
    pad_keys = lambda a: jnp.pad(a, ((0, 0), (0, key_pad), (0, 0))) if key_pad else a
    rope = None
    n_even, n_odd = (depth + 1) // 2, depth // 2
    vt_tiles_per_seq = T // tm if caches is None and tm == ATTN_TK and T % tm == 0 else 0
    fox_kv = diff_kv = None
    even, odd = [], []
    for l in range(depth):
        g = wts["norm_g"][l].reshape(6, 1, d)
        x2 = _ffn(x2, g[0], g[1], wts["w_ffn_in"], wts["w_ffn_out"], (l, 0), tm_ffn)
        if l % 2 == 0:
            e = l // 2
            u, gate, q, logf, *fox_kv, kb, vx = _inproj_even(
                x2, g[2], wts["w_in_even"][e], wts["b_fox_f"][e], tm, e, fox_kv, n_even, vt_tiles_per_seq)
            if caches is None:
                h0 = jnp.zeros((bsz, 1, w), F32)
                buf0 = jnp.zeros((bsz, 8, w), F32)
                kb_all, logf_all = seq(kb), seq(logf)
                vb_all = vx if vt_tiles_per_seq else seq(vx)
            else:
                flogf, lru_h, lru_conv = (c[e] for c in caches[2:5])
                h0 = lru_h.reshape(bsz, 1, w)
                buf0 = jnp.pad(lru_conv, ((0, 0), (8 - (CONV_W - 1), 0), (0, 0)))
                kb_all, vb_all = _history_kv(caches[0], caches[1], e, seq(kb), seq(vx), n_keys + key_pad)
                past_logf = jnp.pad(flogf, ((0, 0), (0, 0), (0, LANES - flogf.shape[-1])))
                logf_all = jnp.concatenate([past_logf, seq(logf)], axis=1)
            ya, h_last, nbuf = _lru(seq(u), seq(gate), h0, buf0, wts["lru_conv_w"][e], wts["lru_conv_b"][e],
                                    wts["lru_wr"][e], wts["lru_wi"][e], wts["lru_br"][e], wts["lru_bi"][e],
                                    wts["lru_lambda"][e], tt)
            Tk = logf_all.shape[1]
            Tp = Tk + key_pad
            ck, cq = _cumsum(pad_keys(logf_all), Tp if Tp <= 2 * CUMSUM_TILE else _pick_tile(Tp, CUMSUM_TILE))
            yb = _attention(seq(q), kb_all, vb_all, (cq[..., Tk - T:Tk], ck), pos0=pos0, kv_len=Tk, fox=True)
            mix = (g[3], wts["w_out_even"][e], [flat(ya), flat(yb)])
            even.append((seq(logf)[..., :FOX_HEADS], h_last.reshape(bsz, w), nbuf))
        else:
            o = l // 2
            lam_init = 0.8 - 0.6 * math.exp(-0.3 * l)
            if rope is None:
                tabs = _rope_tables(T, pos0)
                if T % tm:
                    tabs = [jnp.tile(t, (n // T, 1)) for t in tabs]
                rope = (tabs, tabs[0].shape[0] // tm)
            q, *diff_kv, kb, vx = _inproj_odd(x2, g[2], wts["w_in_odd"][o], rope[0], rope[1], tm, o, diff_kv,
                                              n_odd, vt_tiles_per_seq)
            if caches is None:
                kb_all = seq(kb)
                vb_all = vx if vt_tiles_per_seq else seq(vx)
            else:
                kb_all, vb_all = _history_kv(caches[5], caches[6], o, seq(kb), seq(vx), n_keys + key_pad)
            y = _attention(seq(q), kb_all, vb_all, (wts["diff_lambda"][o], wts["diff_subln_g"][o]),
                           pos0=pos0, kv_len=n_keys, fox=False, lam_init=lam_init)
            mix = (g[3], wts["w_out_odd"][o], [flat(y)])
        x2 = _ffn(x2, g[4], g[5], wts["w_ffn_in"], wts["w_ffn_out"], (l, 1), tm_ffn, mix)
    per_seq = lambda a: (jnp.transpose(a, (0, 1, 4, 2, 3)) if a.ndim == 5
                         else a.reshape(a.shape[0], bsz, T, *a.shape[2:]))
    even_out = [per_seq(a) for a in fox_kv] + [jnp.stack(z) for z in zip(*even)]
    return x2.reshape(bsz, T, d), even_out, [per_seq(a) for a in diff_kv]


def _prepare_weights(norm_g, w_ffn_in, w_ffn_out, w_in_even, b_fox_f, lru_conv_w, lru_conv_b, lru_w_gates,
                     lru_b_gates, lru_lambda, w_out_even, w_in_odd, diff_lambda, diff_subln_g, w_out_odd):
    n_even, d, _ = w_in_even.shape
    w = d // 2
    blk = w // LRU_BLOCKS
    pad_lanes = lambda a: jnp.pad(a, [(0, 0)] * (a.ndim - 1) + [(0, LANES - a.shape[-1])])
    eye = jnp.eye(LRU_BLOCKS, dtype=F32)
    dense = lambda g: jnp.einsum("encg,nm->encmg", g, eye).reshape(n_even, w, w)
    return dict(
        norm_g=norm_g, w_ffn_in=w_ffn_in.astype(BF16), w_ffn_out=w_ffn_out.astype(BF16),
        w_in_even=jnp.concatenate([w_in_even[:, :, :5 * w], pad_lanes(w_in_even[:, :, 5 * w:])],
                                  axis=-1).astype(BF16),
        b_fox_f=pad_lanes(b_fox_f).reshape(n_even, 1, LANES),
        lru_conv_w=lru_conv_w, lru_conv_b=lru_conv_b.reshape(n_even, 1, w),
        lru_wr=dense(lru_w_gates[..., :blk]).astype(BF16), lru_wi=dense(lru_w_gates[..., blk:]).astype(BF16),
        lru_br=lru_b_gates[..., :blk].reshape(n_even, 1, w), lru_bi=lru_b_gates[..., blk:].reshape(n_even, 1, w),
        lru_lambda=lru_lambda.reshape(n_even, 1, w), w_out_even=w_out_even.astype(BF16),
        w_in_odd=w_in_odd.astype(BF16), diff_lambda=diff_lambda, diff_subln_g=diff_subln_g.reshape(-1, 1, LANES),
        w_out_odd=w_out_odd.astype(BF16))


def kernel(x_prompt, x_sample, cache_fox_k, cache_fox_v, cache_fox_logf, state_lru_h, state_lru_conv, cache_diff_k, cache_diff_v, norm_g, w_ffn_in, w_ffn_out, w_in_even, b_fox_f, lru_conv_w, lru_conv_b, lru_w_gates, lru_b_gates, lru_lambda, w_out_even, w_in_odd, diff_lambda, diff_subln_g, w_out_odd):
    wts = _prepare_weights(norm_g, w_ffn_in, w_ffn_out, w_in_even, b_fox_f, lru_conv_w, lru_conv_b, lru_w_gates,
                           lru_b_gates, lru_lambda, w_out_even, w_in_odd, diff_lambda, diff_subln_g, w_out_odd)
    y_p, pe, po = _trunk(x_prompt, 0, None, wts, _pick_tile(x_prompt.shape[0] * x_prompt.shape[1], 512))
    caches = (cache_fox_k, cache_fox_v, cache_fox_logf, state_lru_h, state_lru_conv, cache_diff_k, cache_diff_v)
    y_s, se, so = _trunk(x_sample, cache_fox_k.shape[2], caches, wts,
                         _pick_tile(x_sample.shape[0] * x_sample.shape[1], 256))
    return (y_p, y_s, *pe, *po, *se, *so)
```
